```python
import math
import jax
import jax.numpy as jnp
from jax import lax
import numpy as np

D_MODEL = 1024
BATCH = 8
SEQ = 4096
DEPTH = 1

D_MIX = D_MODEL
DA_HEADS = 4
DA_HEAD_DIM = 64
DA_VAL_DIM = 2 * DA_HEAD_DIM
DA_QK = DA_HEADS * 2 * DA_HEAD_DIM
DA_WIDTH = DA_HEADS * DA_VAL_DIM
ML_HEADS = 4
ML_HEAD_DIM = (D_MIX - DA_WIDTH) // ML_HEADS
ML_WIDTH = ML_HEADS * ML_HEAD_DIM
ML_CONV = 4
ML_CHUNK = 64
IN_SPLITS = (DA_QK, DA_QK, DA_WIDTH, ML_WIDTH, ML_WIDTH, ML_WIDTH, ML_WIDTH, ML_HEADS, ML_HEADS)
N_IN = sum(IN_SPLITS)
MEM_LEN = 256
X_HEADS = 4
X_HEAD_DIM = D_MODEL // X_HEADS
N_GROUPS = 4
EXPERTS_PER_GROUP = 8
N_EXPERTS = N_GROUPS * EXPERTS_PER_GROUP
TOP_K = 2
D_EXPERT = 512
MOE_BLOCK = 128
Q_BLOCK = 128
ROPE_THETA = 10000.0
EPS = 1e-6

kernel_name = 'hybrid_diffattn_mlstm_hmoe_layer'


def rms_norm(x, g):
    xf = x.astype(jnp.float32)
    y = xf * lax.rsqrt(jnp.mean(xf * xf, axis=-1, keepdims=True) + EPS)
    return (y * g.astype(jnp.float32)).astype(x.dtype)


def rotary(t, pos):
    d = t.shape[-1]
    inv_freq = ROPE_THETA ** (-jnp.arange(0, d, 2, dtype=jnp.float32) / d)
    ang = pos.astype(jnp.float32)[:, None] * inv_freq[None, :]
    cos = jnp.cos(ang).astype(t.dtype)
    sin = jnp.sin(ang).astype(t.dtype)
    t1, t2 = t[..., : d // 2], t[..., d // 2:]
    return jnp.concatenate([t1 * cos - t2 * sin, t2 * cos + t1 * sin], axis=-1)


def diff_attention(q, k, v, lam, subln_g, lam_init):
    bsz, seq = q.shape[0], q.shape[1]
    pos = jnp.arange(seq)
    q = rotary(q.transpose(0, 2, 3, 1, 4), pos) * (DA_HEAD_DIM ** -0.5)
    k = rotary(k.transpose(0, 2, 3, 1, 4), pos)
    v = v.transpose(0, 2, 1, 3)
    n_blk = seq // Q_BLOCK
    q_blocks = q.reshape(bsz, DA_HEADS, 2, n_blk, Q_BLOCK, DA_HEAD_DIM).transpose(3, 0, 1, 2, 4, 5)
    k_pos = jnp.arange(seq)

    def one_block(args):
        q_blk, start = args
        s = jnp.einsum('bhmqd,bhmkd->bhmqk', q_blk, k).astype(jnp.float32)
        q_pos = start + jnp.arange(Q_BLOCK)
        causal = k_pos[None, :] <= q_pos[:, None]
        p = jax.nn.softmax(jnp.where(causal, s, -jnp.inf), axis=-1)
        a = p[:, :, 0] - lam * p[:, :, 1]
        return jnp.einsum('bhqk,bhkd->bhqd', a.astype(v.dtype), v)

    o = lax.map(one_block, (q_blocks, jnp.arange(n_blk) * Q_BLOCK))
    o = o.transpose(1, 0, 3, 2, 4).reshape(bsz, seq, DA_HEADS, DA_VAL_DIM)
    o = rms_norm(o, subln_g) * (1.0 - lam_init)
    return o.reshape(bsz, seq, DA_WIDTH)


def causal_depthwise_conv(u, w, b):
    c = u.shape[-1]
    y = lax.conv_general_dilated(u, w.astype(u.dtype), window_strides=(1,),
                                 padding=[(ML_CONV - 1, 0)],
                                 dimension_numbers=('NWC', 'WIO', 'NWC'),
                                 feature_group_count=c)
    return y + b.astype(u.dtype)


def mlstm_chunkwise(q, k, v, o_pre, i_pre, f_pre, norm_g):
    f32 = jnp.float32
    bsz, seq = q.shape[0], q.shape[1]
    n_ch = seq // ML_CHUNK

    def heads(t):
        return t.astype(f32).reshape(bsz, n_ch, ML_CHUNK, ML_HEADS, ML_HEAD_DIM).transpose(1, 0, 3, 2, 4)

    def gates(t):
        return t.astype(f32).reshape(bsz, n_ch, ML_CHUNK, ML_HEADS).transpose(1, 0, 3, 2)

    qc, kc, vc = heads(q), heads(k) * (ML_HEAD_DIM ** -0.5), heads(v)
    ic = gates(i_pre)
    lfc = jax.nn.log_sigmoid(gates(f_pre))
    tri = jnp.tril(jnp.ones((ML_CHUNK, ML_CHUNK), dtype=bool))

    def step(carry, xs):
        c_mat, n_vec, m_prev = carry
        qj, ks, vs, ig, lf = xs
        a = jnp.cumsum(lf, axis=-1)
        log_d = jnp.where(tri, a[..., :, None] - a[..., None, :] + ig[..., None, :], -jnp.inf)
        log_inter = a + m_prev[..., None]
        m_row = jnp.maximum(log_inter, jnp.max(log_d, axis=-1))
        w_intra = jnp.exp(log_d - m_row[..., None])
        w_inter = jnp.exp(log_inter - m_row)
        s = jnp.einsum('bhjd,bhsd->bhjs', qj, ks) * w_intra
        num = (w_inter[..., None] * jnp.einsum('bhjd,bhde->bhje', qj, c_mat)
               + jnp.einsum('bhjs,bhse->bhje', s, vs))
        den = w_inter * jnp.einsum('bhjd,bhd->bhj', qj, n_vec) + jnp.sum(s, axis=-1)
        h = num / jnp.maximum(jnp.abs(den), jnp.exp(-m_row))[..., None]
        g = a[..., -1]
        log_w = g[..., None] - a + ig
        m_new = jnp.maximum(g + m_prev, jnp.max(log_w, axis=-1))
        w_state = jnp.exp(log_w - m_new[..., None])
        decay = jnp.exp(g + m_prev - m_new)
        c_new = decay[..., None, None] * c_mat + jnp.einsum('bhs,bhsd,bhse->bhde', w_state, ks, vs)
        n_new = decay[..., None] * n_vec + jnp.einsum('bhs,bhsd->bhd', w_state, ks)
        return (c_new, n_new, m_new), h

    init = (jnp.zeros((bsz, ML_HEADS, ML_HEAD_DIM, ML_HEAD_DIM), f32),
            jnp.zeros((bsz, ML_HEADS, ML_HEAD_DIM), f32),
            jnp.zeros((bsz, ML_HEADS), f32))
    _, h = lax.scan(step, init, (qc, kc, vc, ic, lfc))
    h = h.transpose(1, 0, 3, 2, 4).reshape(bsz, seq, ML_HEADS, ML_HEAD_DIM)
    mu = jnp.mean(h, axis=-1, keepdims=True)
    var = jnp.mean(jnp.square(h - mu), axis=-1, keepdims=True)
    hn = (h - mu) * lax.rsqrt(var + EPS) * norm_g.astype(f32).reshape(ML_HEADS, ML_HEAD_DIM)
    o = jax.nn.sigmoid(o_pre.astype(f32)).reshape(bsz, seq, ML_HEADS, ML_HEAD_DIM)
    return (o * hn).reshape(bsz, seq, ML_WIDTH).astype(q.dtype)


def hybrid_mixer(h, w_in, w_out, da_lambda, da_subln_g, lam_init, ml_conv_w, ml_conv_b, ml_gate_b, ml_norm_g):
    bsz, seq = h.shape[0], h.shape[1]
    proj = h @ w_in
    dq, dk, dv, mq, mk, mv, mo, mi, mf = jnp.split(proj, np.cumsum(IN_SPLITS)[:-1].tolist(), axis=-1)
    lv = da_lambda.astype(jnp.float32)
    lam = jnp.exp(jnp.sum(lv[0] * lv[1])) - jnp.exp(jnp.sum(lv[2] * lv[3])) + lam_init
    y_da = diff_attention(dq.reshape(bsz, seq, DA_HEADS, 2, DA_HEAD_DIM),
                          dk.reshape(bsz, seq, DA_HEADS, 2, DA_HEAD_DIM),
                          dv.reshape(bsz, seq, DA_HEADS, DA_VAL_DIM), lam, da_subln_g, lam_init)
    qk = jax.nn.silu(causal_depthwise_conv(jnp.concatenate([mq, mk], axis=-1), ml_conv_w, ml_conv_b))
    mq_c, mk_c = jnp.split(qk, 2, axis=-1)
    y_ml = mlstm_chunkwise(mq_c, mk_c, mv, mo,
                           mi + ml_gate_b[0].astype(mi.dtype), mf + ml_gate_b[1].astype(mf.dtype), ml_norm_g)
    return jnp.concatenate([y_da, y_ml], axis=-1) @ w_out


def cross_attention(h, mem_n, wq, wk, wv, wo):
    bsz, seq = h.shape[0], h.shape[1]
    m_len = mem_n.shape[1]
    q = (h @ wq).reshape(bsz, seq, X_HEADS, X_HEAD_DIM)
    k = (mem_n @ wk).reshape(bsz, m_len, X_HEADS, X_HEAD_DIM)
    v = (mem_n @ wv).reshape(bsz, m_len, X_HEADS, X_HEAD_DIM)
    s = jnp.einsum('bqhd,bkhd->bhqk', q, k).astype(jnp.float32) * (X_HEAD_DIM ** -0.5)
    p = jax.nn.softmax(s, axis=-1)
    o = jnp.einsum('bhqk,bkhd->bqhd', p.astype(v.dtype), v).reshape(bsz, seq, D_MODEL)
    return o @ wo


def hierarchical_moe(h, w_rg, b_rg, w_re, b_re, w1, w3, w2):
    f32 = jnp.float32
    bsz, seq, d = h.shape
    n_tok = bsz * seq
    xt = h.reshape(n_tok, d)
    g_prob = jax.nn.softmax((xt @ w_rg).astype(f32) + b_rg.astype(f32), axis=-1)
    g_idx = jnp.argmax(g_prob, axis=-1).astype(jnp.int32)
    g_w = jnp.take_along_axis(g_prob, g_idx[:, None], axis=-1)[:, 0]
    e_logits = ((xt @ w_re).astype(f32) + b_re.astype(f32)).reshape(n_tok, N_GROUPS, EXPERTS_PER_GROUP)
    e_logits = jnp.take_along_axis(e_logits, g_idx[:, None, None], axis=1)[:, 0]
    top_p, top_i = lax.top_k(jax.nn.softmax(e_logits, axis=-1), TOP_K)
    comb = g_w[:, None] * top_p / jnp.sum(top_p, axis=-1, keepdims=True)
    expert = g_idx[:, None] * EXPERTS_PER_GROUP + top_i.astype(jnp.int32)
    n_asg = n_tok * TOP_K
    flat_e = expert.reshape(-1)
    flat_tok = jnp.repeat(jnp.arange(n_tok, dtype=jnp.int32), TOP_K)
    flat_w = comb.reshape(-1)
    order = jnp.argsort(flat_e)
    sorted_e, sorted_tok, sorted_w = flat_e[order], flat_tok[order], flat_w[order]
    counts = jnp.bincount(flat_e, length=N_EXPERTS).astype(jnp.int32)
    padded = (counts + MOE_BLOCK - 1) // MOE_BLOCK * MOE_BLOCK
    starts = jnp.cumsum(counts) - counts
    pad_ends = jnp.cumsum(padded)
    pad_starts = pad_ends - padded
    dest = pad_starts[sorted_e] + jnp.arange(n_asg, dtype=jnp.int32) - starts[sorted_e]
    n_blocks = -(-n_asg // MOE_BLOCK) + N_EXPERTS
    n_rows = n_blocks * MOE_BLOCK
    buf_tok = jnp.full((n_rows,), n_tok, jnp.int32).at[dest].set(sorted_tok)
    buf_w = jnp.zeros((n_rows,), f32).at[dest].set(sorted_w)
    block_e = jnp.clip(jnp.searchsorted(pad_ends, jnp.arange(n_blocks, dtype=jnp.int32) * MOE_BLOCK, side='right'),
                       0, N_EXPERTS - 1).astype(jnp.int32)
    xpad = jnp.concatenate([xt, jnp.zeros((1, d), xt.dtype)], axis=0)
    xb = xpad[buf_tok].reshape(n_blocks, MOE_BLOCK, d)

    def run_block(args):
        x_blk, e = args
        return (jax.nn.silu(x_blk @ w1[e]) * (x_blk @ w3[e])) @ w2[e]

    yb = lax.map(run_block, (xb, block_e)).reshape(n_rows, d)
    yb = yb * buf_w[:, None].astype(yb.dtype)
    out = jnp.zeros((n_tok + 1, d), yb.dtype).at[buf_tok].add(yb)[:n_tok]
    return out.reshape(bsz, seq, d)


def setup_inputs(seed: int = 0) -> dict:
    key = jax.random.key(seed)
    ks = jax.random.split(key, 32)
    f32 = jnp.float32
    L = DEPTH

    def nrm(k, shape, scale):
        return jax.random.normal(k, shape, f32) * scale

    def gain(k, shape):
        return 1.0 + 0.02 * jax.random.normal(k, shape, f32)

    f_bias = jnp.linspace(3.0, 6.0, ML_HEADS, dtype=f32)
    ml_gate_b = jnp.stack([nrm(ks[9], (L, ML_HEADS), 0.1),
                           f_bias[None, :] + nrm(ks[10], (L, ML_HEADS), 0.1)], axis=1)
    return {
        'x': nrm(ks[0], (BATCH, SEQ, D_MODEL), 1.0),
        'mem': nrm(ks[1], (BATCH, MEM_LEN, D_MODEL), 1.0),
        'norm_mix_g': gain(ks[2], (L, D_MODEL)),
        'w_in': nrm(ks[3], (L, D_MODEL, N_IN), D_MODEL ** -0.5),
        'w_out': nrm(ks[4], (L, D_MIX, D_MODEL), D_MIX ** -0.5),
        'da_lambda': nrm(ks[5], (L, 4, DA_HEAD_DIM), 0.1),
        'da_subln_g': gain(ks[6], (L, DA_VAL_DIM)),
        'ml_conv_w': nrm(ks[7], (L, ML_CONV, 1, 2 * ML_WIDTH), ML_CONV ** -0.5),
        'ml_conv_b': nrm(ks[8], (L, 2 * ML_WIDTH), 0.02),
        'ml_gate_b': ml_gate_b,
        'ml_norm_g': gain(ks[11], (L, ML_WIDTH)),
        'norm_x_g': gain(ks[12], (L, D_MODEL)),
        'norm_mem_g': gain(ks[13], (L, D_MODEL)),
        'w_xq': nrm(ks[14], (L, D_MODEL, D_MODEL), D_MODEL ** -0.5),
        'w_xk': nrm(ks[15], (L, D_MODEL, D_MODEL), D_MODEL ** -0.5),
        'w_xv': nrm(ks[16], (L, D_MODEL, D_MODEL), D_MODEL ** -0.5),
        'w_xo': nrm(ks[17], (L, D_MODEL, D_MODEL), D_MODEL ** -0.5),
        'norm_ffn_g': gain(ks[18], (L, D_MODEL)),
        'w_router_group': nrm(ks[19], (L, D_MODEL, N_GROUPS), D_MODEL ** -0.5),
        'b_router_group': nrm(ks[20], (L, N_GROUPS), 0.01),
        'w_router_expert': nrm(ks[21], (L, D_MODEL, N_EXPERTS), D_MODEL ** -0.5),
        'b_router_expert': nrm(ks[22], (L, N_EXPERTS), 0.01),
        'w1': nrm(ks[23], (L, N_EXPERTS, D_MODEL, D_EXPERT), D_MODEL ** -0.5),
        'w3': nrm(ks[24], (L, N_EXPERTS, D_MODEL, D_EXPERT), D_MODEL ** -0.5),
        'w2': nrm(ks[25], (L, N_EXPERTS, D_EXPERT, D_MODEL), D_EXPERT ** -0.5),
        'norm_final_g': gain(ks[26], (D_MODEL,)),
    }


def reference(x, mem, norm_mix_g, w_in, w_out, da_lambda, da_subln_g, ml_conv_w, ml_conv_b, ml_gate_b,
              ml_norm_g, norm_x_g, norm_mem_g, w_xq, w_xk, w_xv, w_xo, norm_ffn_g, w_router_group,
              b_router_group, w_router_expert, b_router_expert, w1, w3, w2, norm_final_g):
    for l in range(DEPTH):
        lam_init = 0.8 - 0.6 * math.exp(-0.3 * l)
        x = x + hybrid_mixer(rms_norm(x, norm_mix_g[l]), w_in[l], w_out[l], da_lambda[l], da_subln_g[l],
                             lam_init, ml_conv_w[l], ml_conv_b[l], ml_gate_b[l], ml_norm_g[l])
        x = x + cross_attention(rms_norm(x, norm_x_g[l]), rms_norm(mem, norm_mem_g[l]),
                                w_xq[l], w_xk[l], w_xv[l], w_xo[l])
        x = x + hierarchical_moe(rms_norm(x, norm_ffn_g[l]), w_router_group[l], b_router_group[l],
                                 w_router_expert[l], b_router_expert[l], w1[l], w3[l], w2[l])
    return rms_norm(x, norm_final_g)
```

```python
import functools
import math

import jax
import jax.numpy as jnp
from jax import lax
from jax.experimental import pallas as pl
from jax.experimental.pallas import tpu as pltpu

F32 = jnp.float32
BF16 = jnp.bfloat16

LANE = 128
SUBLANE = 8
VMEM_LIMIT_BYTES = 56 * 1024 * 1024

EPS = 1e-6
ROPE_THETA = 10000.0
DA_HEADS = 4
DA_HEAD_DIM = 64
DA_VAL_DIM = 2 * DA_HEAD_DIM
ML_HEADS = 4
ML_CONV = 4
X_HEADS = 4
N_GROUPS = 4
EXPERTS_PER_GROUP = 8
N_EXPERTS = N_GROUPS * EXPERTS_PER_GROUP
ROUTER_ROWS = 40

ROW_TILE = 512
ATTN_TILE = 256
ML_CHUNK = 256
MOE_TILE = 2048
MOE_ROWS = 128

NEG_INF = float("-inf")


def _rms(x, g):
    return x * lax.rsqrt(jnp.mean(x * x, axis=-1, keepdims=True) + EPS) * g


def _dot(a, b):
    return jnp.dot(a, b, preferred_element_type=F32)


def _dot_nt(a, b):
    return lax.dot_general(a, b, (((1,), (1,)), ((), ())), preferred_element_type=F32)


def _dot_tn(a, b):
    return lax.dot_general(a, b, (((0,), (0,)), ((), ())), preferred_element_type=F32)


def _split_bf16(x):
    hi = x.astype(BF16)
    lo = (x - hi.astype(F32)).astype(BF16)
    return hi, lo


def _params(*semantics):
    return pltpu.CompilerParams(dimension_semantics=semantics, vmem_limit_bytes=VMEM_LIMIT_BYTES)


def _inproj_body(x_ref, g_ref, w_ref, cos_ref, sin_ref, proj_ref, gcol_ref, grow_ref, *, rot_cols, q_cols):
    hb = _rms(x_ref[0], g_ref[...]).astype(BF16)
    cos = cos_ref[...]
    sin = sin_ref[...]
    lane = lax.broadcasted_iota(jnp.int32, cos.shape, 1)
    first_half = (lane % DA_HEAD_DIM) < (DA_HEAD_DIM // 2)
    n_main = proj_ref.shape[-1]
    for c in range(n_main // LANE):
        t = _dot(hb, w_ref[:, c * LANE:(c + 1) * LANE])
        if c * LANE < rot_cols:
            rot = jnp.where(first_half, pltpu.roll(t, LANE - DA_HEAD_DIM // 2, 1),
                            pltpu.roll(t, DA_HEAD_DIM // 2, 1))
            t = t * cos + rot * sin
            if c * LANE < q_cols:
                t = t * (DA_HEAD_DIM ** -0.5)
        proj_ref[0, :, c * LANE:(c + 1) * LANE] = t.astype(BF16)
    gates = _dot(hb, w_ref[:, n_main:n_main + LANE])
    gcol_ref[0] = gates
    grow_ref[0] = gates.T[:SUBLANE, :]


def _inproj(x, g, w_pad, cos_t, sin_t, n_main, rot_cols, q_cols):
    bsz, seq, d = x.shape
    tm = min(ROW_TILE, seq)
    body = functools.partial(_inproj_body, rot_cols=rot_cols, q_cols=q_cols)
    return pl.pallas_call(
        body,
        grid=(bsz, seq // tm),
        in_specs=[
            pl.BlockSpec((1, tm, d), lambda b, i: (b, i, 0)),
            pl.BlockSpec((1, d), lambda b, i: (0, 0)),
            pl.BlockSpec(w_pad.shape, lambda b, i: (0, 0)),
            pl.BlockSpec((tm, LANE), lambda b, i: (i, 0)),
            pl.BlockSpec((tm, LANE), lambda b, i: (i, 0)),
        ],
        out_specs=[
            pl.BlockSpec((1, tm, n_main), lambda b, i: (b, i, 0)),
            pl.BlockSpec((1, tm, LANE), lambda b, i: (b, i, 0)),
            pl.BlockSpec((1, SUBLANE, tm), lambda b, i: (b, 0, i)),
        ],
        out_shape=[
            jax.ShapeDtypeStruct((bsz, seq, n_main), BF16),
            jax.ShapeDtypeStruct((bsz, seq, LANE), F32),
            jax.ShapeDtypeStruct((bsz, SUBLANE, seq), F32),
        ],
        compiler_params=_params("parallel", "parallel"),
        name="inproj",
    )(x, g, w_pad, cos_t, sin_t)


def _diffattn_body(q_ref, k_ref, v_ref, lam_ref, g_ref, o_ref, m_scr, l_scr, acc_scr, *, lam_init):
    qi = pl.program_id(2)
    tq = q_ref.shape[1]
    q = q_ref[0]
    lane = lax.broadcasted_iota(jnp.int32, q.shape, 1)
    zero = jnp.zeros_like(q)
    q2 = jnp.concatenate([jnp.where(lane < DA_HEAD_DIM, q, zero), jnp.where(lane >= DA_HEAD_DIM, q, zero)], axis=0)

    m_scr[...] = jnp.full(m_scr.shape, NEG_INF, F32)
    l_scr[...] = jnp.zeros(l_scr.shape, F32)
    acc_scr[...] = jnp.zeros(acc_scr.shape, F32)

    def step(j, masked):
        start = pl.multiple_of(j * tq, tq)
        k = k_ref[0, pl.ds(start, tq), :]
        v = v_ref[0, pl.ds(start, tq), :]
        s = _dot_nt(q2, k)
        if masked:
            row = lax.broadcasted_iota(jnp.int32, s.shape, 0) % tq
            col = lax.broadcasted_iota(jnp.int32, s.shape, 1)
            s = jnp.where(col <= row, s, NEG_INF)
        m_prev = m_scr[...]
        m_new = jnp.maximum(m_prev, jnp.max(s, axis=-1, keepdims=True))
        alpha = jnp.exp(m_prev - m_new)
        p = jnp.exp(s - m_new)
        l_scr[...] = alpha * l_scr[...] + jnp.sum(p, axis=-1, keepdims=True)
        acc_scr[...] = alpha * acc_scr[...] + _dot(p.astype(BF16), v)
        m_scr[...] = m_new

    def full_step(j, carry):
        step(j, False)
        return carry

    lax.fori_loop(0, qi, full_step, 0)
    step(qi, True)

    lv = lam_ref[...]
    lam = (jnp.exp(jnp.sum(lv[0:1, :] * lv[1:2, :], axis=-1, keepdims=True))
           - jnp.exp(jnp.sum(lv[2:3, :] * lv[3:4, :], axis=-1, keepdims=True)) + lam_init)
    o = acc_scr[...] / l_scr[...]
    o = o[:tq, :] - lam * o[tq:, :]
    o_ref[0] = (_rms(o, g_ref[...]) * (1.0 - lam_init)).astype(o_ref.dtype)


def _diffattn(proj, da_lambda, subln_g, lam_init):
    bsz, seq, _ = proj.shape
    tq = min(ATTN_TILE, seq)
    body = functools.partial(_diffattn_body, lam_init=lam_init)
    return pl.pallas_call(
        body,
        grid=(bsz, DA_HEADS, seq // tq),
        in_specs=[
            pl.BlockSpec((1, tq, LANE), lambda b, h, i: (b, i, h)),
            pl.BlockSpec((1, seq, LANE), lambda b, h, i: (b, 0, DA_HEADS + h)),
            pl.BlockSpec((1, seq, LANE), lambda b, h, i: (b, 0, 2 * DA_HEADS + h)),
            pl.BlockSpec(da_lambda.shape, lambda b, h, i: (0, 0)),
            pl.BlockSpec((1, DA_VAL_DIM), lambda b, h, i: (0, 0)),
        ],
        out_specs=pl.BlockSpec((1, tq, LANE), lambda b, h, i: (b, i, h)),
        out_shape=jax.ShapeDtypeStruct((bsz, seq, DA_HEADS * DA_VAL_DIM), BF16),
        scratch_shapes=[
            pltpu.VMEM((2 * tq, 1), F32),
            pltpu.VMEM((2 * tq, 1), F32),
            pltpu.VMEM((2 * tq, DA_VAL_DIM), F32),
        ],
        compiler_params=_params("parallel", "parallel", "arbitrary"),
        name="diffattn",
    )(proj, proj, proj, da_lambda, subln_g)


def _log_sigmoid(x):
    return jnp.minimum(x, 0.0) - jnp.log(1.0 + jnp.exp(-jnp.abs(x)))


def _sigmoid(x):
    return 1.0 / (1.0 + jnp.exp(-x))


def _mlstm_body(q_ref, k_ref, v_ref, o_ref, gcol_ref, grow_ref, cw_ref, cb_ref, gbcol_ref, gbrow_ref, ng_ref,
                y_ref, xbuf, c_scr, m_scr):
    chunk = pl.program_id(1)
    L = q_ref.shape[1]
    width = q_ref.shape[2]
    dh = width // ML_HEADS
    tail = SUBLANE

    @pl.when(chunk == 0)
    def _():
        xbuf[0:tail, :] = jnp.zeros((tail, 2 * width), F32)
        c_scr[...] = jnp.zeros(c_scr.shape, F32)
        m_scr[...] = jnp.zeros(m_scr.shape, F32)

    xbuf[tail:tail + L, 0:width] = q_ref[0].astype(F32)
    xbuf[tail:tail + L, width:2 * width] = k_ref[0].astype(F32)
    conv = cb_ref[...]
    for j in range(ML_CONV):
        conv = conv + cw_ref[j:j + 1, :] * xbuf[pl.ds(tail - (ML_CONV - 1) + j, L), :]
    xbuf[0:tail, :] = xbuf[L:L + tail, :]
    qk = conv * _sigmoid(conv)

    gc = gcol_ref[0] + gbcol_ref[...]
    gr = grow_ref[0] + gbrow_ref[...]
    row_id = lax.broadcasted_iota(jnp.int32, (L, L), 0)
    col_id = lax.broadcasted_iota(jnp.int32, (L, L), 1)
    causal = col_id <= row_id
    tri = jnp.where(causal, 1.0, 0.0).astype(BF16)
    tri_t = jnp.where(row_id <= col_id, 1.0, 0.0).astype(BF16)
    lfc_hi, lfc_lo = _split_bf16(_log_sigmoid(gc))
    lfr_hi, lfr_lo = _split_bf16(_log_sigmoid(gr))
    a_c = _dot(tri, lfc_hi) + _dot(tri, lfc_lo)
    a_r = _dot(lfr_hi, tri_t) + _dot(lfr_lo, tri_t)

    lane = lax.broadcasted_iota(jnp.int32, (L, dh), 1)
    ones_col = jnp.where(lane == 0, 1.0, 0.0).astype(BF16)

    for h in range(ML_HEADS):
        fh = ML_HEADS + h
        m_prev = m_scr[h:h + 1, 0:1]
        a_col = a_c[:, fh:fh + 1]
        ig_col = gc[:, h:h + 1]
        a_row = a_r[fh:fh + 1, :]
        ig_row = gr[h:h + 1, :]
        log_d = jnp.where(causal, a_col - (a_row - ig_row), NEG_INF)
        log_inter = a_col + m_prev
        m_row = jnp.maximum(log_inter, jnp.max(log_d, axis=-1, keepdims=True))
        w_intra = jnp.exp(log_d - m_row)
        w_inter = jnp.exp(log_inter - m_row)

        qh = qk[:, h * dh:(h + 1) * dh].astype(BF16)
        kh = (qk[:, width + h * dh:width + (h + 1) * dh] * (dh ** -0.5)).astype(BF16)
        v_aug = jnp.concatenate([v_ref[0, :, h * dh:(h + 1) * dh], ones_col], axis=1)
        s = _dot_nt(qh, kh) * w_intra
        c_mat = c_scr[h]
        num = w_inter * _dot(qh, c_mat.astype(BF16)) + _dot(s.astype(BF16), v_aug)
        den = num[:, dh:dh + 1]
        hid = num[:, 0:dh] / jnp.maximum(jnp.abs(den), jnp.exp(-m_row))

        g_last = a_col[L - 1:L, :]
        log_w = g_last - a_col + ig_col
        m_new = jnp.maximum(g_last + m_prev, jnp.max(log_w, axis=0, keepdims=True))
        w_state = jnp.exp(log_w - m_new)
        decay = jnp.exp(g_last + m_prev - m_new)
        wv = (w_state * v_aug.astype(F32)).astype(BF16)
        c_scr[h] = decay * c_mat + _dot_tn(kh, wv)
        m_scr[h:h + 1, :] = jnp.broadcast_to(m_new, (1, m_scr.shape[1]))

        mu = jnp.mean(hid, axis=-1, keepdims=True)
        cen = hid - mu
        var = jnp.mean(cen * cen, axis=-1, keepdims=True)
        hn = cen * lax.rsqrt(var + EPS) * ng_ref[:, h * dh:(h + 1) * dh]
        gate = _sigmoid(o_ref[0, :, h * dh:(h + 1) * dh].astype(F32))
        y_ref[0, :, h * dh:(h + 1) * dh] = (gate * hn).astype(y_ref.dtype)


def _mlstm(proj, gcol, grow, conv_w, conv_b, gb_col, gb_row, norm_g, col0):
    bsz, seq, _ = proj.shape
    width = norm_g.shape[-1]
    dh = width // ML_HEADS
    L = min(ML_CHUNK, seq)
    blk0 = col0 // width
    spec = lambda k: pl.BlockSpec((1, L, width), lambda b, c: (b, c, blk0 + k))
    return pl.pallas_call(
        _mlstm_body,
        grid=(bsz, seq // L),
        in_specs=[
            spec(0), spec(1), spec(2), spec(3),
            pl.BlockSpec((1, L, LANE), lambda b, c: (b, c, 0)),
            pl.BlockSpec((1, SUBLANE, L), lambda b, c: (b, 0, c)),
            pl.BlockSpec(conv_w.shape, lambda b, c: (0, 0)),
            pl.BlockSpec(conv_b.shape, lambda b, c: (0, 0)),
            pl.BlockSpec(gb_col.shape, lambda b, c: (0, 0)),
            pl.BlockSpec(gb_row.shape, lambda b, c: (0, 0)),
            pl.BlockSpec(norm_g.shape, lambda b, c: (0, 0)),
        ],
        out_specs=pl.BlockSpec((1, L, width), lambda b, c: (b, c, 0)),
        out_shape=jax.ShapeDtypeStruct((bsz, seq, width), BF16),
        scratch_shapes=[
            pltpu.VMEM((L + 2 * SUBLANE, 2 * width), F32),
            pltpu.VMEM((ML_HEADS, dh, 2 * dh), F32),
            pltpu.VMEM((SUBLANE, LANE), F32),
        ],
        compiler_params=_params("parallel", "arbitrary"),
        name="mlstm",
    )(proj, proj, proj, proj, gcol, grow, conv_w, conv_b, gb_col, gb_row, norm_g)


def _memkv_body(mem_ref, g_ref, wk_ref, wv_ref, k_ref, v_ref):
    mb = _rms(mem_ref[0], g_ref[...]).astype(BF16)
    k_ref[0] = _dot(mb, wk_ref[...]).astype(BF16)
    v_ref[0] = _dot(mb, wv_ref[...]).astype(BF16)


def _memkv(mem, g, wk, wv):
    bsz, mlen, d = mem.shape
    full = lambda a: pl.BlockSpec(a.shape, lambda b: (0,) * a.ndim)
    blk = pl.BlockSpec((1, mlen, d), lambda b: (b, 0, 0))
    return pl.pallas_call(
        _memkv_body,
        grid=(bsz,),
        in_specs=[blk, full(g), full(wk), full(wv)],
        out_specs=[blk, blk],
        out_shape=[jax.ShapeDtypeStruct((bsz, mlen, d), BF16)] * 2,
        compiler_params=_params("parallel"),
        name="memkv",
    )(mem, g, wk, wv)


def _mix_body(x_ref, yda_ref, yml_ref, wo1_ref, wo2_ref, gx_ref, wq_ref, km_ref, vm_ref, wxo_ref, gf_ref,
              wrh_ref, wrl_ref, br_ref, x2_ref, h3_ref, route_ref):
    x1 = x_ref[0] + _dot(yda_ref[0], wo1_ref[...]) + _dot(yml_ref[0], wo2_ref[...])

    d = x1.shape[-1]
    dh = d // X_HEADS
    q = _dot(_rms(x1, gx_ref[...]).astype(BF16), wq_ref[...]).astype(BF16)
    heads = []
    for h in range(X_HEADS):
        sl = slice(h * dh, (h + 1) * dh)
        s = _dot_nt(q[:, sl], km_ref[0, :, sl]) * (dh ** -0.5)
        p = jnp.exp(s - jnp.max(s, axis=-1, keepdims=True))
        o = _dot(p.astype(BF16), vm_ref[0, :, sl]) / jnp.sum(p, axis=-1, keepdims=True)
        heads.append(o.astype(BF16))
    x2 = x1 + _dot(jnp.concatenate(heads, axis=1), wxo_ref[...])
    x2_ref[0] = x2

    h3 = _rms(x2, gf_ref[...])
    h3_ref[0] = h3
    h_hi, h_lo = _split_bf16(h3)
    logits = (_dot_nt(wrh_ref[...], h_hi) + _dot_nt(wrh_ref[...], h_lo) + _dot_nt(wrl_ref[...], h_hi)
              + br_ref[...])
    tm = logits.shape[1]
    gl = logits[0:N_GROUPS, :]
    g_iota = lax.broadcasted_iota(jnp.int32, gl.shape, 0)
    g_max = jnp.max(gl, axis=0, keepdims=True)
    g_idx = jnp.min(jnp.where(gl == g_max, g_iota, N_GROUPS), axis=0, keepdims=True)
    g_w = 1.0 / jnp.sum(jnp.exp(gl - g_max), axis=0, keepdims=True)
    el = jnp.zeros((EXPERTS_PER_GROUP, tm), F32)
    for g in range(N_GROUPS):
        lo = N_GROUPS + g * EXPERTS_PER_GROUP
        el = jnp.where(g_idx == g, logits[lo:lo + EXPERTS_PER_GROUP, :], el)
    e_iota = lax.broadcasted_iota(jnp.int32, el.shape, 0)
    e1 = jnp.max(el, axis=0, keepdims=True)
    i1 = jnp.min(jnp.where(el == e1, e_iota, EXPERTS_PER_GROUP), axis=0, keepdims=True)
    el2 = jnp.where(e_iota == i1, NEG_INF, el)
    e2 = jnp.max(el2, axis=0, keepdims=True)
    i2 = jnp.min(jnp.where(el2 == e2, e_iota, EXPERTS_PER_GROUP), axis=0, keepdims=True)
    p2 = jnp.exp(e2 - e1)
    w1 = g_w / (1.0 + p2)
    w2 = g_w * p2 / (1.0 + p2)
    base = g_idx * EXPERTS_PER_GROUP
    zero = jnp.zeros((1, tm), F32)
    route_ref[0] = jnp.concatenate(
        [(base + i1).astype(F32), (base + i2).astype(F32), w1, w2, zero, zero, zero, zero], axis=0)


def _mix(x, yda, yml, wo1, wo2, gx, wq, kmem, vmem, wxo, gf, wr_hi, wr_lo, br):
    bsz, seq, d = x.shape
    tm = min(ROW_TILE, seq)
    mlen = kmem.shape[1]
    full = lambda a: pl.BlockSpec(a.shape, lambda b, i: (0,) * a.ndim)
    tok = lambda w: pl.BlockSpec((1, tm, w), lambda b, i: (b, i, 0))
    memspec = pl.BlockSpec((1, mlen, d), lambda b, i: (b, 0, 0))
    return pl.pallas_call(
        _mix_body,
        grid=(bsz, seq // tm),
        in_specs=[tok(d), tok(yda.shape[-1]), tok(yml.shape[-1]), full(wo1), full(wo2), full(gx), full(wq),
                  memspec, memspec, full(wxo), full(gf), full(wr_hi), full(wr_lo), full(br)],
        out_specs=[tok(d), tok(d), pl.BlockSpec((1, SUBLANE, tm), lambda b, i: (b, 0, i))],
        out_shape=[
            jax.ShapeDtypeStruct((bsz, seq, d), F32),
            jax.ShapeDtypeStruct((bsz, seq, d), F32),
            jax.ShapeDtypeStruct((bsz, SUBLANE, seq), F32),
        ],
        compiler_params=_params("parallel", "parallel"),
        name="mix_xattn_router",
    )(x, yda, yml, wo1, wo2, gx, wq, kmem, vmem, wxo, gf, wr_hi, wr_lo, br)


def _moe_body(offs_ref, tok_ref, wt_ref, h_ref, w1_ref, w3_ref, w2_ref, out_ref, xg, yb, *, seg):
    tile = pl.program_id(0)
    e = pl.program_id(1)
    rows = xg.shape[0]

    @pl.when(jnp.logical_and(tile == 0, e == 0))
    def _():
        xg[...] = jnp.zeros(xg.shape, F32)

    @pl.when(e == 0)
    def _():
        out_ref[...] = jnp.zeros(out_ref.shape, F32)

    start = offs_ref[tile * (N_EXPERTS + 1) + e]
    end = offs_ref[tile * (N_EXPERTS + 1) + e + 1]
    flat0 = tile * seg

    def block(i, carry):
        base = start + i * rows
        cnt = jnp.minimum(rows, end - base)

        def gather(r, c):
            t = tok_ref[flat0 + base + r]
            xg[pl.ds(r, 1), :] = h_ref[0, pl.ds(t, 1), :]
            return c

        lax.fori_loop(0, cnt, gather, 0)
        xb = xg[...].astype(BF16)
        a = _dot(xb, w1_ref[0])
        b = _dot(xb, w3_ref[0])
        hmid = (a * _sigmoid(a) * b).astype(BF16)
        yb[...] = _dot(hmid, w2_ref[0])

        def scatter(r, c):
            t = tok_ref[flat0 + base + r]
            w = wt_ref[flat0 + base + r]
            out_ref[0, pl.ds(t, 1), :] = out_ref[0, pl.ds(t, 1), :] + w * yb[pl.ds(r, 1), :]
            return c

        lax.fori_loop(0, cnt, scatter, 0)
        return carry

    lax.fori_loop(0, (end - start + rows - 1) // rows, block, 0)


def _moe(offs, tok, wt, h3_tiles, w1, w3, w2):
    ntiles, tt, d = h3_tiles.shape
    de = w1.shape[-1]
    body = functools.partial(_moe_body, seg=tok.shape[0] // ntiles)
    grid_spec = pltpu.PrefetchScalarGridSpec(
        num_scalar_prefetch=3,
        grid=(ntiles, N_EXPERTS),
        in_specs=[
            pl.BlockSpec((1, tt, d), lambda t, e, *_: (t, 0, 0)),
            pl.BlockSpec((1, d, de), lambda t, e, *_: (e, 0, 0)),
            pl.BlockSpec((1, d, de), lambda t, e, *_: (e, 0, 0)),
            pl.BlockSpec((1, de, d), lambda t, e, *_: (e, 0, 0)),
        ],
        out_specs=pl.BlockSpec((1, tt, d), lambda t, e, *_: (t, 0, 0)),
        scratch_shapes=[pltpu.VMEM((MOE_ROWS, d), F32), pltpu.VMEM((MOE_ROWS, d), F32)],
    )
    return pl.pallas_call(
        body,
        grid_spec=grid_spec,
        out_shape=jax.ShapeDtypeStruct((ntiles, tt, d), F32),
        compiler_params=_params("arbitrary", "arbitrary"),
        name="moe",
    )(offs, tok, wt, h3_tiles, w1, w3, w2)


def _final_body(x_ref, y_ref, g_ref, o_ref):
    o_ref[...] = _rms(x_ref[...] + y_ref[...], g_ref[...])


def _final(x2, moe, g):
    n, d = x2.shape
    tm = min(2 * ROW_TILE, n)
    blk = pl.BlockSpec((tm, d), lambda i: (i, 0))
    return pl.pallas_call(
        _final_body,
        grid=(n // tm,),
        in_specs=[blk, blk, pl.BlockSpec((1, d), lambda i: (0, 0))],
        out_specs=blk,
        out_shape=jax.ShapeDtypeStruct((n, d), F32),
        compiler_params=_params("parallel"),
        name="final_norm",
    )(x2, moe, g)


def _rope_tables(seq):
    half = DA_HEAD_DIM // 2
    inv_freq = ROPE_THETA ** (-jnp.arange(0, DA_HEAD_DIM, 2, dtype=F32) / DA_HEAD_DIM)
    ang = jnp.arange(seq, dtype=F32)[:, None] * inv_freq[None, :]
    cos, sin = jnp.cos(ang), jnp.sin(ang)
    reps = LANE // half
    cos_t = jnp.tile(cos, (1, reps))
    sin_t = jnp.tile(jnp.concatenate([-sin, sin], axis=1), (1, reps // 2))
    return cos_t, sin_t


def _sort_assignments(route, tt):
    bsz, _, seq = route.shape
    per = seq // tt
    ntiles = bsz * per
    tile = lambda a: a.reshape(bsz, TOP_K_ROWS, per, tt).transpose(0, 2, 1, 3).reshape(ntiles, TOP_K_ROWS * tt)
    keys = tile(route[:, 0:TOP_K_ROWS, :]).astype(jnp.int32)
    wts = tile(route[:, TOP_K_ROWS:2 * TOP_K_ROWS, :])
    order = jnp.argsort(keys, axis=-1).astype(jnp.int32)
    tok = order % tt
    wt = jnp.take_along_axis(wts, order, axis=-1)
    counts = jnp.sum(keys[:, :, None] == jnp.arange(N_EXPERTS, dtype=jnp.int32)[None, None, :], axis=1,
                     dtype=jnp.int32)
    offs = jnp.concatenate([jnp.zeros((ntiles, 1), jnp.int32), jnp.cumsum(counts, axis=-1, dtype=jnp.int32)],
                           axis=-1)
    return offs.reshape(-1), tok.reshape(-1), wt.reshape(-1)


TOP_K_ROWS = 2


def kernel(x, mem, norm_mix_g, w_in, w_out, da_lambda, da_subln_g, ml_conv_w, ml_conv_b, ml_gate_b, ml_norm_g, norm_x_g, norm_mem_g, w_xq, w_xk, w_xv, w_xo, norm_ffn_g, w_router_group, b_router_group, w_router_expert, b_router_expert, w1, w3, w2, norm_final_g):
    depth = w_in.shape[0]
    bsz, seq, d = x.shape
    da_qk = DA_HEADS * 2 * DA_HEAD_DIM
    da_width = DA_HEADS * DA_VAL_DIM
    ml_width = ml_norm_g.shape[-1]
    n_main = 2 * da_qk + da_width + 4 * ml_width
    n_in = w_in.shape[-1]
    cos_t, sin_t = _rope_tables(seq)
    row = lambda v: v.reshape(1, -1).astype(F32)

    for l in range(depth):
        lam_init = 0.8 - 0.6 * math.exp(-0.3 * l)
        w_pad = jnp.pad(w_in[l], ((0, 0), (0, n_main + LANE - n_in))).astype(BF16)
        proj, gcol, grow = _inproj(x, row(norm_mix_g[l]), w_pad, cos_t, sin_t, n_main, 2 * da_qk, da_qk)

        yda = _diffattn(proj, da_lambda[l].astype(F32), row(da_subln_g[l]), lam_init)

        gb = ml_gate_b[l].astype(F32).reshape(1, 2 * ML_HEADS)
        gb_col = jnp.pad(gb, ((0, 0), (0, LANE - 2 * ML_HEADS)))
        gb_row = gb.reshape(2 * ML_HEADS, 1)
        yml = _mlstm(proj, gcol, grow, ml_conv_w[l].reshape(ML_CONV, 2 * ml_width).astype(F32),
                     row(ml_conv_b[l]), gb_col, gb_row, row(ml_norm_g[l]), 2 * da_qk + da_width)

        kmem, vmem = _memkv(mem, row(norm_mem_g[l]), w_xk[l].astype(BF16), w_xv[l].astype(BF16))

        wr = jnp.concatenate([w_router_group[l], w_router_expert[l]], axis=1).astype(F32).T
        wr = jnp.pad(wr, ((0, ROUTER_ROWS - wr.shape[0]), (0, 0)))
        wr_hi = wr.astype(BF16)
        wr_lo = (wr - wr_hi.astype(F32)).astype(BF16)
        br = jnp.pad(jnp.concatenate([b_router_group[l], b_router_expert[l]]).astype(F32),
                     (0, ROUTER_ROWS - N_GROUPS - N_EXPERTS)).reshape(ROUTER_ROWS, 1)
        wo = w_out[l].astype(BF16)
        x2, h3, route = _mix(x, yda, yml, wo[:da_width], wo[da_width:], row(norm_x_g[l]), w_xq[l].astype(BF16),
                             kmem, vmem, w_xo[l].astype(BF16), row(norm_ffn_g[l]), wr_hi, wr_lo, br)

        tt = min(MOE_TILE, seq)
        offs, tok, wt = _sort_assignments(route, tt)
        moe = _moe(offs, tok, wt, h3.reshape(bsz * seq // tt, tt, d), w1[l].astype(BF16), w3[l].astype(BF16),
                   w2[l].astype(BF16))
        x = (x2 + moe.reshape(bsz, seq, d)) if l + 1 < depth else None

    out = _final(x2.reshape(bsz * seq, d), moe.reshape(bsz * seq, d), row(norm_final_g))
    return out.reshape(bsz, seq, d)
```

```python
import functools
import math

import jax
import jax.numpy as jnp
from jax import lax
from jax.experimental import pallas as pl
from jax.experimental.pallas import tpu as pltpu

F32 = jnp.float32
BF16 = jnp.bfloat16

LANE = 128
SUBLANE = 8
VMEM_LIMIT_BYTES = 56 * 1024 * 1024

EPS = 1e-6
ROPE_THETA = 10000.0
DA_HEADS = 4
DA_HEAD_DIM = 64
DA_VAL_DIM = 2 * DA_HEAD_DIM
ML_HEADS = 4
ML_CONV = 4
X_HEADS = 4
N_GROUPS = 4
EXPERTS_PER_GROUP = 8
N_EXPERTS = N_GROUPS * EXPERTS_PER_GROUP
ROUTER_ROWS = 40

ROW_TILE = 512
ML_CHUNK = 256
MOE_TILE = 2048
MOE_ROWS = 128

NEG_INF = float("-inf")


def _rms(x, g):
    return x * lax.rsqrt(jnp.mean(x * x, axis=-1, keepdims=True) + EPS) * g


def _dot(a, b):
    return jnp.dot(a, b, preferred_element_type=F32)


def _dot_nt(a, b):
    return lax.dot_general(a, b, (((1,), (1,)), ((), ())), preferred_element_type=F32)


def _dot_tn(a, b):
    return lax.dot_general(a, b, (((0,), (0,)), ((), ())), preferred_element_type=F32)


def _split_bf16(x):
    hi = x.astype(BF16)
    lo = (x - hi.astype(F32)).astype(BF16)
    return hi, lo


def _params(*semantics):
    return pltpu.CompilerParams(dimension_semantics=semantics, vmem_limit_bytes=VMEM_LIMIT_BYTES)


def _inproj_body(x_ref, g_ref, w_ref, cos_ref, sin_ref, qt_ref, vt_ref, proj_ref, gcol_ref, grow_ref):
    hb = _rms(x_ref[0], g_ref[...]).astype(BF16)
    cos = cos_ref[...]
    sin = sin_ref[...]
    lane = lax.broadcasted_iota(jnp.int32, cos.shape, 1)
    first_half = (lane % DA_HEAD_DIM) < (DA_HEAD_DIM // 2)
    n_main = w_ref.shape[1] - LANE
    for c in range(n_main // LANE):
        t = _dot(hb, w_ref[:, c * LANE:(c + 1) * LANE])
        if c < 2 * DA_HEADS:
            rot = jnp.where(first_half, pltpu.roll(t, LANE - DA_HEAD_DIM // 2, 1),
                            pltpu.roll(t, DA_HEAD_DIM // 2, 1))
            t = t * cos + rot * sin
        if c < DA_HEADS:
            qt_ref[0, 0, c * LANE:(c + 1) * LANE, :] = (t * (DA_HEAD_DIM ** -0.5)).T.astype(BF16)
        elif c < 2 * DA_HEADS:
            proj_ref[0, :, (c - DA_HEADS) * LANE:(c - DA_HEADS + 1) * LANE] = t.astype(BF16)
        elif c < 3 * DA_HEADS:
            vt_ref[0, 0, (c - 2 * DA_HEADS) * LANE:(c - 2 * DA_HEADS + 1) * LANE, :] = t.T.astype(BF16)
        else:
            proj_ref[0, :, (c - 2 * DA_HEADS) * LANE:(c - 2 * DA_HEADS + 1) * LANE] = t.astype(BF16)
    gates = _dot(hb, w_ref[:, n_main:n_main + LANE])
    gcol_ref[0] = gates
    grow_ref[0] = gates.T[:SUBLANE, :]


def _inproj(x, g, w_pad, cos_t, sin_t):
    bsz, seq, d = x.shape
    tm = min(ROW_TILE, seq)
    nt = seq // tm
    da_w = DA_HEADS * DA_VAL_DIM
    n_proj = w_pad.shape[1] - LANE - 2 * da_w
    return pl.pallas_call(
        _inproj_body,
        grid=(bsz, nt),
        in_specs=[
            pl.BlockSpec((1, tm, d), lambda b, i: (b, i, 0)),
            pl.BlockSpec((1, d), lambda b, i: (0, 0)),
            pl.BlockSpec(w_pad.shape, lambda b, i: (0, 0)),
            pl.BlockSpec((tm, LANE), lambda b, i: (i, 0)),
            pl.BlockSpec((tm, LANE), lambda b, i: (i, 0)),
        ],
        out_specs=[
            pl.BlockSpec((1, 1, da_w, tm), lambda b, i: (b, i, 0, 0)),
            pl.BlockSpec((1, 1, da_w, tm), lambda b, i: (b, i, 0, 0)),
            pl.BlockSpec((1, tm, n_proj), lambda b, i: (b, i, 0)),
            pl.BlockSpec((1, tm, LANE), lambda b, i: (b, i, 0)),
            pl.BlockSpec((1, SUBLANE, tm), lambda b, i: (b, 0, i)),
        ],
        out_shape=[
            jax.ShapeDtypeStruct((bsz, nt, da_w, tm), BF16),
            jax.ShapeDtypeStruct((bsz, nt, da_w, tm), BF16),
            jax.ShapeDtypeStruct((bsz, seq, n_proj), BF16),
            jax.ShapeDtypeStruct((bsz, seq, LANE), F32),
            jax.ShapeDtypeStruct((bsz, SUBLANE, seq), F32),
        ],
        compiler_params=_params("parallel", "parallel"),
        name="inproj",
    )(x, g, w_pad, cos_t, sin_t)


def _diffattn_body(qt_ref, k_ref, vt_ref, lam_ref, g_ref, o_ref, m_scr, acc_scr, *, lam_init):
    qi = pl.program_id(2)
    tq = qt_ref.shape[3]
    qt = qt_ref[0, 0]
    row = lax.broadcasted_iota(jnp.int32, qt.shape, 0)
    zero = jnp.zeros_like(qt)
    q2t = jnp.concatenate([jnp.where(row < DA_HEAD_DIM, qt, zero), jnp.where(row >= DA_HEAD_DIM, qt, zero)], axis=1)
    pad_rows = acc_scr.shape[0] - DA_VAL_DIM
    ones_rows = jnp.where(lax.broadcasted_iota(jnp.int32, (pad_rows, tq), 0) == 0, 1.0, 0.0).astype(BF16)

    m_scr[...] = jnp.full(m_scr.shape, NEG_INF, F32)
    acc_scr[...] = jnp.zeros(acc_scr.shape, F32)

    def step(j, masked):
        k = k_ref[0, pl.ds(pl.multiple_of(j * tq, tq), tq), :]
        v_aug = jnp.concatenate([vt_ref[0, j], ones_rows], axis=0)
        s = _dot(k, q2t)
        if masked:
            key = lax.broadcasted_iota(jnp.int32, s.shape, 0)
            qry = lax.broadcasted_iota(jnp.int32, s.shape, 1) % tq
            s = jnp.where(key <= qry, s, NEG_INF)
        m_prev = m_scr[...]
        m_new = jnp.maximum(m_prev, jnp.max(s, axis=0, keepdims=True))
        alpha = jnp.exp(m_prev - m_new)
        p = jnp.exp(s - m_new).astype(BF16)
        acc_scr[...] = alpha * acc_scr[...] + _dot(v_aug, p)
        m_scr[...] = m_new

    def full_step(j, carry):
        step(j, False)
        return carry

    lax.fori_loop(0, qi, full_step, 0)
    step(qi, True)

    lv = lam_ref[...]
    lam = (jnp.exp(jnp.sum(lv[0:1, :] * lv[1:2, :], axis=-1, keepdims=True))
           - jnp.exp(jnp.sum(lv[2:3, :] * lv[3:4, :], axis=-1, keepdims=True)) + lam_init)
    acc = acc_scr[...]
    o = acc[0:DA_VAL_DIM, :] / acc[DA_VAL_DIM:DA_VAL_DIM + 1, :]
    o = o[:, :tq] - lam * o[:, tq:]
    y = o * lax.rsqrt(jnp.mean(o * o, axis=0, keepdims=True) + EPS) * g_ref[...] * (1.0 - lam_init)
    o_ref[0] = y.T.astype(o_ref.dtype)


def _diffattn(qt, proj, vt, da_lambda, subln_g_col, lam_init):
    bsz, nt, _, tq = qt.shape
    seq = nt * tq
    body = functools.partial(_diffattn_body, lam_init=lam_init)
    return pl.pallas_call(
        body,
        grid=(bsz, DA_HEADS, nt),
        in_specs=[
            pl.BlockSpec((1, 1, DA_VAL_DIM, tq), lambda b, h, i: (b, i, h, 0)),
            pl.BlockSpec((1, seq, LANE), lambda b, h, i: (b, 0, h)),
            pl.BlockSpec((1, nt, DA_VAL_DIM, tq), lambda b, h, i: (b, 0, h, 0)),
            pl.BlockSpec(da_lambda.shape, lambda b, h, i: (0, 0)),
            pl.BlockSpec((DA_VAL_DIM, 1), lambda b, h, i: (0, 0)),
        ],
        out_specs=pl.BlockSpec((1, tq, LANE), lambda b, h, i: (b, i, h)),
        out_shape=jax.ShapeDtypeStruct((bsz, seq, DA_HEADS * DA_VAL_DIM), BF16),
        scratch_shapes=[
            pltpu.VMEM((1, 2 * tq), F32),
            pltpu.VMEM((DA_VAL_DIM + 2 * SUBLANE, 2 * tq), F32),
        ],
        compiler_params=_params("parallel", "parallel", "arbitrary"),
        name="diffattn",
    )(qt, proj, vt, da_lambda, subln_g_col)


def _log_sigmoid(x):
    return jnp.minimum(x, 0.0) - jnp.log(1.0 + jnp.exp(-jnp.abs(x)))


def _sigmoid(x):
    return 1.0 / (1.0 + jnp.exp(-x))


def _mlstm_body(q_ref, k_ref, v_ref, o_ref, gcol_ref, grow_ref, cw_ref, cb_ref, gbcol_ref, gbrow_ref, ng_ref,
                y_ref, xbuf, c_scr, m_scr):
    chunk = pl.program_id(1)
    L = q_ref.shape[1]
    width = q_ref.shape[2]
    dh = width // ML_HEADS
    tail = SUBLANE

    @pl.when(chunk == 0)
    def _():
        xbuf[0:tail, :] = jnp.zeros((tail, 2 * width), F32)
        c_scr[...] = jnp.zeros(c_scr.shape, F32)
        m_scr[...] = jnp.zeros(m_scr.shape, F32)

    xbuf[tail:tail + L, 0:width] = q_ref[0].astype(F32)
    xbuf[tail:tail + L, width:2 * width] = k_ref[0].astype(F32)
    conv = cb_ref[...]
    for j in range(ML_CONV):
        conv = conv + cw_ref[j:j + 1, :] * xbuf[pl.ds(tail - (ML_CONV - 1) + j, L), :]
    xbuf[0:tail, :] = xbuf[L:L + tail, :]
    qk = conv * _sigmoid(conv)

    gc = gcol_ref[0] + gbcol_ref[...]
    gr = grow_ref[0] + gbrow_ref[...]
    row_id = lax.broadcasted_iota(jnp.int32, (L, L), 0)
    col_id = lax.broadcasted_iota(jnp.int32, (L, L), 1)
    causal = col_id <= row_id
    tri = jnp.where(causal, 1.0, 0.0).astype(BF16)
    tri_t = jnp.where(row_id <= col_id, 1.0, 0.0).astype(BF16)
    lfc_hi, lfc_lo = _split_bf16(_log_sigmoid(gc))
    lfr_hi, lfr_lo = _split_bf16(_log_sigmoid(gr))
    a_c = _dot(tri, lfc_hi) + _dot(tri, lfc_lo)
    a_r = _dot(lfr_hi, tri_t) + _dot(lfr_lo, tri_t)

    lane = lax.broadcasted_iota(jnp.int32, (L, dh), 1)
    ones_col = jnp.where(lane == 0, 1.0, 0.0).astype(BF16)

    for h in range(ML_HEADS):
        fh = ML_HEADS + h
        m_prev = m_scr[h:h + 1, 0:1]
        a_col = a_c[:, fh:fh + 1]
        ig_col = gc[:, h:h + 1]
        a_row = a_r[fh:fh + 1, :]
        ig_row = gr[h:h + 1, :]
        log_d = jnp.where(causal, a_col - (a_row - ig_row), NEG_INF)
        log_inter = a_col + m_prev
        m_row = jnp.maximum(log_inter, jnp.max(log_d, axis=-1, keepdims=True))
        w_intra = jnp.exp(log_d - m_row)
        w_inter = jnp.exp(log_inter - m_row)

        qh = qk[:, h * dh:(h + 1) * dh].astype(BF16)
        kh = (qk[:, width + h * dh:width + (h + 1) * dh] * (dh ** -0.5)).astype(BF16)
        v_aug = jnp.concatenate([v_ref[0, :, h * dh:(h + 1) * dh], ones_col], axis=1)
        s = _dot_nt(qh, kh) * w_intra
        c_mat = c_scr[h]
        num = w_inter * _dot(qh, c_mat.astype(BF16)) + _dot(s.astype(BF16), v_aug)
        den = num[:, dh:dh + 1]
        hid = num[:, 0:dh] / jnp.maximum(jnp.abs(den), jnp.exp(-m_row))

        g_last = a_col[L - 1:L, :]
        log_w = g_last - a_col + ig_col
        m_new = jnp.maximum(g_last + m_prev, jnp.max(log_w, axis=0, keepdims=True))
        w_state = jnp.exp(log_w - m_new)
        decay = jnp.exp(g_last + m_prev - m_new)
        wv = (w_state * v_aug.astype(F32)).astype(BF16)
        c_scr[h] = decay * c_mat + _dot_tn(kh, wv)
        m_scr[h:h + 1, :] = jnp.broadcast_to(m_new, (1, m_scr.shape[1]))

        mu = jnp.mean(hid, axis=-1, keepdims=True)
        cen = hid - mu
        var = jnp.mean(cen * cen, axis=-1, keepdims=True)
        hn = cen * lax.rsqrt(var + EPS) * ng_ref[:, h * dh:(h + 1) * dh]
        gate = _sigmoid(o_ref[0, :, h * dh:(h + 1) * dh].astype(F32))
        y_ref[0, :, h * dh:(h + 1) * dh] = (gate * hn).astype(y_ref.dtype)


def _mlstm(proj, gcol, grow, conv_w, conv_b, gb_col, gb_row, norm_g, col0):
    bsz, seq, _ = proj.shape
    width = norm_g.shape[-1]
    dh = width // ML_HEADS
    L = min(ML_CHUNK, seq)
    blk0 = col0 // width
    spec = lambda k: pl.BlockSpec((1, L, width), lambda b, c: (b, c, blk0 + k))
    return pl.pallas_call(
        _mlstm_body,
        grid=(bsz, seq // L),
        in_specs=[
            spec(0), spec(1), spec(2), spec(3),
            pl.BlockSpec((1, L, LANE), lambda b, c: (b, c, 0)),
            pl.BlockSpec((1, SUBLANE, L), lambda b, c: (b, 0, c)),
            pl.BlockSpec(conv_w.shape, lambda b, c: (0, 0)),
            pl.BlockSpec(conv_b.shape, lambda b, c: (0, 0)),
            pl.BlockSpec(gb_col.shape, lambda b, c: (0, 0)),
            pl.BlockSpec(gb_row.shape, lambda b, c: (0, 0)),
            pl.BlockSpec(norm_g.shape, lambda b, c: (0, 0)),
        ],
        out_specs=pl.BlockSpec((1, L, width), lambda b, c: (b, c, 0)),
        out_shape=jax.ShapeDtypeStruct((bsz, seq, width), BF16),
        scratch_shapes=[
            pltpu.VMEM((L + 2 * SUBLANE, 2 * width), F32),
            pltpu.VMEM((ML_HEADS, dh, 2 * dh), F32),
            pltpu.VMEM((SUBLANE, LANE), F32),
        ],
        compiler_params=_params("parallel", "arbitrary"),
        name="mlstm",
    )(proj, proj, proj, proj, gcol, grow, conv_w, conv_b, gb_col, gb_row, norm_g)


def _memkv_body(mem_ref, g_ref, wk_ref, wv_ref, k_ref, v_ref):
    mb = _rms(mem_ref[0], g_ref[...]).astype(BF16)
    k_ref[0] = _dot(mb, wk_ref[...]).astype(BF16)
    v_ref[0] = _dot(mb, wv_ref[...]).astype(BF16)


def _memkv(mem, g, wk, wv):
    bsz, mlen, d = mem.shape
    full = lambda a: pl.BlockSpec(a.shape, lambda b: (0,) * a.ndim)
    blk = pl.BlockSpec((1, mlen, d), lambda b: (b, 0, 0))
    return pl.pallas_call(
        _memkv_body,
        grid=(bsz,),
        in_specs=[blk, full(g), full(wk), full(wv)],
        out_specs=[blk, blk],
        out_shape=[jax.ShapeDtypeStruct((bsz, mlen, d), BF16)] * 2,
        compiler_params=_params("parallel"),
        name="memkv",
    )(mem, g, wk, wv)


def _mix_body(x_ref, yda_ref, yml_ref, wo1_ref, wo2_ref, gx_ref, wq_ref, km_ref, vm_ref, wxo_ref, gf_ref,
              wrh_ref, wrl_ref, br_ref, x2_ref, h3_ref, route_ref):
    x1 = x_ref[0] + _dot(yda_ref[0], wo1_ref[...]) + _dot(yml_ref[0], wo2_ref[...])

    d = x1.shape[-1]
    dh = d // X_HEADS
    q = _dot(_rms(x1, gx_ref[...]).astype(BF16), wq_ref[...]).astype(BF16)
    heads = []
    for h in range(X_HEADS):
        sl = slice(h * dh, (h + 1) * dh)
        s = _dot_nt(q[:, sl], km_ref[0, :, sl]) * (dh ** -0.5)
        p = jnp.exp(s - jnp.max(s, axis=-1, keepdims=True))
        o = _dot(p.astype(BF16), vm_ref[0, :, sl]) / jnp.sum(p, axis=-1, keepdims=True)
        heads.append(o.astype(BF16))
    x2 = x1 + _dot(jnp.concatenate(heads, axis=1), wxo_ref[...])
    x2_ref[0] = x2

    h3 = _rms(x2, gf_ref[...])
    h3_ref[0] = h3
    h_hi, h_lo = _split_bf16(h3)
    logits = (_dot_nt(wrh_ref[...], h_hi) + _dot_nt(wrh_ref[...], h_lo) + _dot_nt(wrl_ref[...], h_hi)
              + br_ref[...])
    tm = logits.shape[1]
    gl = logits[0:N_GROUPS, :]
    g_iota = lax.broadcasted_iota(jnp.int32, gl.shape, 0)
    g_max = jnp.max(gl, axis=0, keepdims=True)
    g_idx = jnp.min(jnp.where(gl == g_max, g_iota, N_GROUPS), axis=0, keepdims=True)
    g_w = 1.0 / jnp.sum(jnp.exp(gl - g_max), axis=0, keepdims=True)
    el = jnp.zeros((EXPERTS_PER_GROUP, tm), F32)
    for g in range(N_GROUPS):
        lo = N_GROUPS + g * EXPERTS_PER_GROUP
        el = jnp.where(g_idx == g, logits[lo:lo + EXPERTS_PER_GROUP, :], el)
    e_iota = lax.broadcasted_iota(jnp.int32, el.shape, 0)
    e1 = jnp.max(el, axis=0, keepdims=True)
    i1 = jnp.min(jnp.where(el == e1, e_iota, EXPERTS_PER_GROUP), axis=0, keepdims=True)
    el2 = jnp.where(e_iota == i1, NEG_INF, el)
    e2 = jnp.max(el2, axis=0, keepdims=True)
    i2 = jnp.min(jnp.where(el2 == e2, e_iota, EXPERTS_PER_GROUP), axis=0, keepdims=True)
    p2 = jnp.exp(e2 - e1)
    w1 = g_w / (1.0 + p2)
    w2 = g_w * p2 / (1.0 + p2)
    base = g_idx * EXPERTS_PER_GROUP
    zero = jnp.zeros((1, tm), F32)
    route_ref[0] = jnp.concatenate(
        [(base + i1).astype(F32), (base + i2).astype(F32), w1, w2, zero, zero, zero, zero], axis=0)


def _mix(x, yda, yml, wo1, wo2, gx, wq, kmem, vmem, wxo, gf, wr_hi, wr_lo, br):
    bsz, seq, d = x.shape
    tm = min(ROW_TILE, seq)
    mlen = kmem.shape[1]
    full = lambda a: pl.BlockSpec(a.shape, lambda b, i: (0,) * a.ndim)
    tok = lambda w: pl.BlockSpec((1, tm, w), lambda b, i: (b, i, 0))
    memspec = pl.BlockSpec((1, mlen, d), lambda b, i: (b, 0, 0))
    return pl.pallas_call(
        _mix_body,
        grid=(bsz, seq // tm),
        in_specs=[tok(d), tok(yda.shape[-1]), tok(yml.shape[-1]), full(wo1), full(wo2), full(gx), full(wq),
                  memspec, memspec, full(wxo), full(gf), full(wr_hi), full(wr_lo), full(br)],
        out_specs=[tok(d), tok(d), pl.BlockSpec((1, SUBLANE, tm), lambda b, i: (b, 0, i))],
        out_shape=[
            jax.ShapeDtypeStruct((bsz, seq, d), F32),
            jax.ShapeDtypeStruct((bsz, seq, d), F32),
            jax.ShapeDtypeStruct((bsz, SUBLANE, seq), F32),
        ],
        compiler_params=_params("parallel", "parallel"),
        name="mix_xattn_router",
    )(x, yda, yml, wo1, wo2, gx, wq, kmem, vmem, wxo, gf, wr_hi, wr_lo, br)


def _moe_body(offs_ref, tok_ref, wt_ref, h_ref, w1_ref, w3_ref, w2_ref, out_ref, xg, yb, *, seg):
    tile = pl.program_id(0)
    e = pl.program_id(1)
    rows = xg.shape[0]

    @pl.when(jnp.logical_and(tile == 0, e == 0))
    def _():
        xg[...] = jnp.zeros(xg.shape, F32)

    @pl.when(e == 0)
    def _():
        out_ref[...] = jnp.zeros(out_ref.shape, F32)

    start = offs_ref[tile * (N_EXPERTS + 1) + e]
    end = offs_ref[tile * (N_EXPERTS + 1) + e + 1]
    flat0 = tile * seg

    def block(i, carry):
        base = start + i * rows
        cnt = jnp.minimum(rows, end - base)

        def gather(r, c):
            t = tok_ref[flat0 + base + r]
            xg[pl.ds(r, 1), :] = h_ref[0, pl.ds(t, 1), :]
            return c

        lax.fori_loop(0, cnt, gather, 0)
        xb = xg[...].astype(BF16)
        a = _dot(xb, w1_ref[0])
        b = _dot(xb, w3_ref[0])
        hmid = (a * _sigmoid(a) * b).astype(BF16)
        yb[...] = _dot(hmid, w2_ref[0])

        def scatter(r, c):
            t = tok_ref[flat0 + base + r]
            w = wt_ref[flat0 + base + r]
            out_ref[0, pl.ds(t, 1), :] = out_ref[0, pl.ds(t, 1), :] + w * yb[pl.ds(r, 1), :]
            return c

        lax.fori_loop(0, cnt, scatter, 0)
        return carry

    lax.fori_loop(0, (end - start + rows - 1) // rows, block, 0)


def _moe(offs, tok, wt, h3_tiles, w1, w3, w2):
    ntiles, tt, d = h3_tiles.shape
    de = w1.shape[-1]
    body = functools.partial(_moe_body, seg=tok.shape[0] // ntiles)
    grid_spec = pltpu.PrefetchScalarGridSpec(
        num_scalar_prefetch=3,
        grid=(ntiles, N_EXPERTS),
        in_specs=[
            pl.BlockSpec((1, tt, d), lambda t, e, *_: (t, 0, 0)),
            pl.BlockSpec((1, d, de), lambda t, e, *_: (e, 0, 0)),
            pl.BlockSpec((1, d, de), lambda t, e, *_: (e, 0, 0)),
            pl.BlockSpec((1, de, d), lambda t, e, *_: (e, 0, 0)),
        ],
        out_specs=pl.BlockSpec((1, tt, d), lambda t, e, *_: (t, 0, 0)),
        scratch_shapes=[pltpu.VMEM((MOE_ROWS, d), F32), pltpu.VMEM((MOE_ROWS, d), F32)],
    )
    return pl.pallas_call(
        body,
        grid_spec=grid_spec,
        out_shape=jax.ShapeDtypeStruct((ntiles, tt, d), F32),
        compiler_params=_params("arbitrary", "arbitrary"),
        name="moe",
    )(offs, tok, wt, h3_tiles, w1, w3, w2)


def _final_body(x_ref, y_ref, g_ref, o_ref):
    o_ref[...] = _rms(x_ref[...] + y_ref[...], g_ref[...])


def _final(x2, moe, g):
    n, d = x2.shape
    tm = min(2 * ROW_TILE, n)
    blk = pl.BlockSpec((tm, d), lambda i: (i, 0))
    return pl.pallas_call(
        _final_body,
        grid=(n // tm,),
        in_specs=[blk, blk, pl.BlockSpec((1, d), lambda i: (0, 0))],
        out_specs=blk,
        out_shape=jax.ShapeDtypeStruct((n, d), F32),
        compiler_params=_params("parallel"),
        name="final_norm",
    )(x2, moe, g)


def _rope_tables(seq):
    half = DA_HEAD_DIM // 2
    inv_freq = ROPE_THETA ** (-jnp.arange(0, DA_HEAD_DIM, 2, dtype=F32) / DA_HEAD_DIM)
    ang = jnp.arange(seq, dtype=F32)[:, None] * inv_freq[None, :]
    cos, sin = jnp.cos(ang), jnp.sin(ang)
    reps = LANE // half
    cos_t = jnp.tile(cos, (1, reps))
    sin_t = jnp.tile(jnp.concatenate([-sin, sin], axis=1), (1, reps // 2))
    return cos_t, sin_t


def _sort_assignments(route, tt):
    bsz, _, seq = route.shape
    per = seq // tt
    ntiles = bsz * per
    tile = lambda a: a.reshape(bsz, TOP_K_ROWS, per, tt).transpose(0, 2, 1, 3).reshape(ntiles, TOP_K_ROWS * tt)
    keys = tile(route[:, 0:TOP_K_ROWS, :]).astype(jnp.int32)
    wts = tile(route[:, TOP_K_ROWS:2 * TOP_K_ROWS, :])
    order = jnp.argsort(keys, axis=-1).astype(jnp.int32)
    tok = order % tt
    wt = jnp.take_along_axis(wts, order, axis=-1)
    counts = jnp.sum(keys[:, :, None] == jnp.arange(N_EXPERTS, dtype=jnp.int32)[None, None, :], axis=1,
                     dtype=jnp.int32)
    offs = jnp.concatenate([jnp.zeros((ntiles, 1), jnp.int32), jnp.cumsum(counts, axis=-1, dtype=jnp.int32)],
                           axis=-1)
    return offs.reshape(-1), tok.reshape(-1), wt.reshape(-1)


TOP_K_ROWS = 2


def kernel(x, mem, norm_mix_g, w_in, w_out, da_lambda, da_subln_g, ml_conv_w, ml_conv_b, ml_gate_b, ml_norm_g, norm_x_g, norm_mem_g, w_xq, w_xk, w_xv, w_xo, norm_ffn_g, w_router_group, b_router_group, w_router_expert, b_router_expert, w1, w3, w2, norm_final_g):
    depth = w_in.shape[0]
    bsz, seq, d = x.shape
    da_qk = DA_HEADS * 2 * DA_HEAD_DIM
    da_width = DA_HEADS * DA_VAL_DIM
    ml_width = ml_norm_g.shape[-1]
    n_main = 2 * da_qk + da_width + 4 * ml_width
    n_in = w_in.shape[-1]
    cos_t, sin_t = _rope_tables(seq)
    row = lambda v: v.reshape(1, -1).astype(F32)

    for l in range(depth):
        lam_init = 0.8 - 0.6 * math.exp(-0.3 * l)
        w_pad = jnp.pad(w_in[l], ((0, 0), (0, n_main + LANE - n_in))).astype(BF16)
        qt, vt, proj, gcol, grow = _inproj(x, row(norm_mix_g[l]), w_pad, cos_t, sin_t)

        yda = _diffattn(qt, proj, vt, da_lambda[l].astype(F32), da_subln_g[l].astype(F32).reshape(-1, 1), lam_init)

        gb = ml_gate_b[l].astype(F32).reshape(1, 2 * ML_HEADS)
        gb_col = jnp.pad(gb, ((0, 0), (0, LANE - 2 * ML_HEADS)))
        gb_row = gb.reshape(2 * ML_HEADS, 1)
        yml = _mlstm(proj, gcol, grow, ml_conv_w[l].reshape(ML_CONV, 2 * ml_width).astype(F32),
                     row(ml_conv_b[l]), gb_col, gb_row, row(ml_norm_g[l]), da_qk)

        kmem, vmem = _memkv(mem, row(norm_mem_g[l]), w_xk[l].astype(BF16), w_xv[l].astype(BF16))

        wr = jnp.concatenate([w_router_group[l], w_router_expert[l]], axis=1).astype(F32).T
        wr = jnp.pad(wr, ((0, ROUTER_ROWS - wr.shape[0]), (0, 0)))
        wr_hi = wr.astype(BF16)
        wr_lo = (wr - wr_hi.astype(F32)).astype(BF16)
        br = jnp.pad(jnp.concatenate([b_router_group[l], b_router_expert[l]]).astype(F32),
                     (0, ROUTER_ROWS - N_GROUPS - N_EXPERTS)).reshape(ROUTER_ROWS, 1)
        wo = w_out[l].astype(BF16)
        x2, h3, route = _mix(x, yda, yml, wo[:da_width], wo[da_width:], row(norm_x_g[l]), w_xq[l].astype(BF16),
                             kmem, vmem, w_xo[l].astype(BF16), row(norm_ffn_g[l]), wr_hi, wr_lo, br)

        tt = min(MOE_TILE, seq)
        offs, tok, wt = _sort_assignments(route, tt)
        moe = _moe(offs, tok, wt, h3.reshape(bsz * seq // tt, tt, d), w1[l].astype(BF16), w3[l].astype(BF16),
                   w2[l].astype(BF16))
        x = (x2 + moe.reshape(bsz, seq, d)) if l + 1 < depth else None

    out = _final(x2.reshape(bsz * seq, d), moe.reshape(bsz * seq, d), row(norm_final_g))
    return out.reshape(bsz, seq, d)
```

```python
import functools
import math

import jax
import jax.numpy as jnp
from jax import lax
from jax.experimental import pallas as pl
from jax.experimental.pallas import tpu as pltpu

F32 = jnp.float32
BF16 = jnp.bfloat16

LANE = 128
SUBLANE = 8
VMEM_LIMIT_BYTES = 56 * 1024 * 1024

EPS = 1e-6
ROPE_THETA = 10000.0
DA_HEADS = 4
DA_HEAD_DIM = 64
DA_VAL_DIM = 2 * DA_HEAD_DIM
ML_HEADS = 4
ML_CONV = 4
X_HEADS = 4
N_GROUPS = 4
EXPERTS_PER_GROUP = 8
N_EXPERTS = N_GROUPS * EXPERTS_PER_GROUP
ROUTER_ROWS = 40

ROW_TILE = 512
ML_CHUNK = 256
MOE_TILE = 4096
MOE_ROWS = 128

NEG_INF = float("-inf")


def _rms(x, g):
    return x * lax.rsqrt(jnp.mean(x * x, axis=-1, keepdims=True) + EPS) * g


def _dot(a, b):
    return jnp.dot(a, b, preferred_element_type=F32)


def _dot_nt(a, b):
    return lax.dot_general(a, b, (((1,), (1,)), ((), ())), preferred_element_type=F32)


def _dot_tn(a, b):
    return lax.dot_general(a, b, (((0,), (0,)), ((), ())), preferred_element_type=F32)


def _split_bf16(x):
    hi = x.astype(BF16)
    lo = (x - hi.astype(F32)).astype(BF16)
    return hi, lo


def _params(*semantics):
    return pltpu.CompilerParams(dimension_semantics=semantics, vmem_limit_bytes=VMEM_LIMIT_BYTES)


def _inproj_body(x_ref, g_ref, w_ref, cos_ref, sin_ref, qt_ref, vt_ref, proj_ref, gcol_ref, grow_ref):
    hb = _rms(x_ref[0], g_ref[...]).astype(BF16)
    cos = cos_ref[...]
    sin = sin_ref[...]
    lane = lax.broadcasted_iota(jnp.int32, cos.shape, 1)
    first_half = (lane % DA_HEAD_DIM) < (DA_HEAD_DIM // 2)
    n_main = w_ref.shape[1] - LANE
    for c in range(n_main // LANE):
        t = _dot(hb, w_ref[:, c * LANE:(c + 1) * LANE])
        if c < 2 * DA_HEADS:
            rot = jnp.where(first_half, pltpu.roll(t, LANE - DA_HEAD_DIM // 2, 1),
                            pltpu.roll(t, DA_HEAD_DIM // 2, 1))
            t = t * cos + rot * sin
        if c < DA_HEADS:
            qt_ref[0, 0, c * LANE:(c + 1) * LANE, :] = (t * (DA_HEAD_DIM ** -0.5)).T.astype(BF16)
        elif c < 2 * DA_HEADS:
            proj_ref[0, :, (c - DA_HEADS) * LANE:(c - DA_HEADS + 1) * LANE] = t.astype(BF16)
        elif c < 3 * DA_HEADS:
            vt_ref[0, 0, (c - 2 * DA_HEADS) * LANE:(c - 2 * DA_HEADS + 1) * LANE, :] = t.T.astype(BF16)
        else:
            proj_ref[0, :, (c - 2 * DA_HEADS) * LANE:(c - 2 * DA_HEADS + 1) * LANE] = t.astype(BF16)
    gates = _dot(hb, w_ref[:, n_main:n_main + LANE])
    gcol_ref[0] = gates
    grow_ref[0] = gates.T[:SUBLANE, :]


def _inproj(x, g, w_pad, cos_t, sin_t):
    bsz, seq, d = x.shape
    tm = min(ROW_TILE, seq)
    nt = seq // tm
    da_w = DA_HEADS * DA_VAL_DIM
    n_proj = w_pad.shape[1] - LANE - 2 * da_w
    return pl.pallas_call(
        _inproj_body,
        grid=(bsz, nt),
        in_specs=[
            pl.BlockSpec((1, tm, d), lambda b, i: (b, i, 0)),
            pl.BlockSpec((1, d), lambda b, i: (0, 0)),
            pl.BlockSpec(w_pad.shape, lambda b, i: (0, 0)),
            pl.BlockSpec((tm, LANE), lambda b, i: (i, 0)),
            pl.BlockSpec((tm, LANE), lambda b, i: (i, 0)),
        ],
        out_specs=[
            pl.BlockSpec((1, 1, da_w, tm), lambda b, i: (b, i, 0, 0)),
            pl.BlockSpec((1, 1, da_w, tm), lambda b, i: (b, i, 0, 0)),
            pl.BlockSpec((1, tm, n_proj), lambda b, i: (b, i, 0)),
            pl.BlockSpec((1, tm, LANE), lambda b, i: (b, i, 0)),
            pl.BlockSpec((1, SUBLANE, tm), lambda b, i: (b, 0, i)),
        ],
        out_shape=[
            jax.ShapeDtypeStruct((bsz, nt, da_w, tm), BF16),
            jax.ShapeDtypeStruct((bsz, nt, da_w, tm), BF16),
            jax.ShapeDtypeStruct((bsz, seq, n_proj), BF16),
            jax.ShapeDtypeStruct((bsz, seq, LANE), F32),
            jax.ShapeDtypeStruct((bsz, SUBLANE, seq), F32),
        ],
        compiler_params=_params("parallel", "parallel"),
        name="inproj",
    )(x, g, w_pad, cos_t, sin_t)


def _diffattn_body(qt_ref, k_ref, vt_ref, lam_ref, g_ref, o_ref, m_scr, acc_scr, *, lam_init):
    qi = pl.program_id(2)
    tq = qt_ref.shape[3]
    qt = qt_ref[0, 0]
    row = lax.broadcasted_iota(jnp.int32, qt.shape, 0)
    zero = jnp.zeros_like(qt)
    q2t = jnp.concatenate([jnp.where(row < DA_HEAD_DIM, qt, zero), jnp.where(row >= DA_HEAD_DIM, qt, zero)], axis=1)
    pad_rows = acc_scr.shape[0] - DA_VAL_DIM
    ones_rows = jnp.where(lax.broadcasted_iota(jnp.int32, (pad_rows, tq), 0) == 0, 1.0, 0.0).astype(BF16)

    m_scr[...] = jnp.full(m_scr.shape, NEG_INF, F32)
    acc_scr[...] = jnp.zeros(acc_scr.shape, F32)

    def step(j, masked):
        k = k_ref[0, pl.ds(pl.multiple_of(j * tq, tq), tq), :]
        v_aug = jnp.concatenate([vt_ref[0, j], ones_rows], axis=0)
        s = _dot(k, q2t)
        if masked:
            key = lax.broadcasted_iota(jnp.int32, s.shape, 0)
            qry = lax.broadcasted_iota(jnp.int32, s.shape, 1) % tq
            s = jnp.where(key <= qry, s, NEG_INF)
        m_prev = m_scr[...]
        m_new = jnp.maximum(m_prev, jnp.max(s, axis=0, keepdims=True))
        alpha = jnp.exp(m_prev - m_new)
        p = jnp.exp(s - m_new).astype(BF16)
        acc_scr[...] = alpha * acc_scr[...] + _dot(v_aug, p)
        m_scr[...] = m_new

    def full_step(j, carry):
        step(j, False)
        return carry

    lax.fori_loop(0, qi, full_step, 0)
    step(qi, True)

    lv = lam_ref[...]
    lam = (jnp.exp(jnp.sum(lv[0:1, :] * lv[1:2, :], axis=-1, keepdims=True))
           - jnp.exp(jnp.sum(lv[2:3, :] * lv[3:4, :], axis=-1, keepdims=True)) + lam_init)
    acc = acc_scr[...]
    o = acc[0:DA_VAL_DIM, :] / acc[DA_VAL_DIM:DA_VAL_DIM + 1, :]
    o = o[:, :tq] - lam * o[:, tq:]
    y = o * lax.rsqrt(jnp.mean(o * o, axis=0, keepdims=True) + EPS) * g_ref[...] * (1.0 - lam_init)
    o_ref[0] = y.T.astype(o_ref.dtype)


def _diffattn(qt, proj, vt, da_lambda, subln_g_col, lam_init):
    bsz, nt, _, tq = qt.shape
    seq = nt * tq
    body = functools.partial(_diffattn_body, lam_init=lam_init)
    return pl.pallas_call(
        body,
        grid=(bsz, DA_HEADS, nt),
        in_specs=[
            pl.BlockSpec((1, 1, DA_VAL_DIM, tq), lambda b, h, i: (b, i, h, 0)),
            pl.BlockSpec((1, seq, LANE), lambda b, h, i: (b, 0, h)),
            pl.BlockSpec((1, nt, DA_VAL_DIM, tq), lambda b, h, i: (b, 0, h, 0)),
            pl.BlockSpec(da_lambda.shape, lambda b, h, i: (0, 0)),
            pl.BlockSpec((DA_VAL_DIM, 1), lambda b, h, i: (0, 0)),
        ],
        out_specs=pl.BlockSpec((1, tq, LANE), lambda b, h, i: (b, i, h)),
        out_shape=jax.ShapeDtypeStruct((bsz, seq, DA_HEADS * DA_VAL_DIM), BF16),
        scratch_shapes=[
            pltpu.VMEM((1, 2 * tq), F32),
            pltpu.VMEM((DA_VAL_DIM + 2 * SUBLANE, 2 * tq), F32),
        ],
        compiler_params=_params("parallel", "parallel", "arbitrary"),
        name="diffattn",
    )(qt, proj, vt, da_lambda, subln_g_col)


def _log_sigmoid(x):
    return jnp.minimum(x, 0.0) - jnp.log(1.0 + jnp.exp(-jnp.abs(x)))


def _sigmoid(x):
    return 1.0 / (1.0 + jnp.exp(-x))


def _mlstm_body(q_ref, k_ref, v_ref, o_ref, gcol_ref, grow_ref, cw_ref, cb_ref, gbcol_ref, gbrow_ref, ng_ref,
                y_ref, xbuf, c_scr, m_scr):
    chunk = pl.program_id(1)
    L = q_ref.shape[1]
    width = q_ref.shape[2]
    dh = width // ML_HEADS
    tail = SUBLANE

    @pl.when(chunk == 0)
    def _():
        xbuf[0:tail, :] = jnp.zeros((tail, 2 * width), F32)
        c_scr[...] = jnp.zeros(c_scr.shape, F32)
        m_scr[...] = jnp.zeros(m_scr.shape, F32)

    xbuf[tail:tail + L, 0:width] = q_ref[0].astype(F32)
    xbuf[tail:tail + L, width:2 * width] = k_ref[0].astype(F32)
    conv = cb_ref[...]
    for j in range(ML_CONV):
        conv = conv + cw_ref[j:j + 1, :] * xbuf[pl.ds(tail - (ML_CONV - 1) + j, L), :]
    xbuf[0:tail, :] = xbuf[L:L + tail, :]
    qk = conv * _sigmoid(conv)

    gc = gcol_ref[0] + gbcol_ref[...]
    gr = grow_ref[0] + gbrow_ref[...]
    row_id = lax.broadcasted_iota(jnp.int32, (L, L), 0)
    col_id = lax.broadcasted_iota(jnp.int32, (L, L), 1)
    causal = col_id <= row_id
    tri = jnp.where(causal, 1.0, 0.0).astype(BF16)
    tri_t = jnp.where(row_id <= col_id, 1.0, 0.0).astype(BF16)
    lfc_hi, lfc_lo = _split_bf16(_log_sigmoid(gc))
    lfr_hi, lfr_lo = _split_bf16(_log_sigmoid(gr))
    a_c = _dot(tri, lfc_hi) + _dot(tri, lfc_lo)
    a_r = _dot(lfr_hi, tri_t) + _dot(lfr_lo, tri_t)

    lane = lax.broadcasted_iota(jnp.int32, (L, dh), 1)
    ones_col = jnp.where(lane == 0, 1.0, 0.0).astype(BF16)

    for h in range(ML_HEADS):
        fh = ML_HEADS + h
        m_prev = m_scr[h:h + 1, 0:1]
        a_col = a_c[:, fh:fh + 1]
        ig_col = gc[:, h:h + 1]
        a_row = a_r[fh:fh + 1, :]
        ig_row = gr[h:h + 1, :]
        log_d = jnp.where(causal, a_col - (a_row - ig_row), NEG_INF)
        log_inter = a_col + m_prev
        m_row = jnp.maximum(log_inter, jnp.max(log_d, axis=-1, keepdims=True))
        w_intra = jnp.exp(log_d - m_row)
        w_inter = jnp.exp(log_inter - m_row)

        qh = qk[:, h * dh:(h + 1) * dh].astype(BF16)
        kh = (qk[:, width + h * dh:width + (h + 1) * dh] * (dh ** -0.5)).astype(BF16)
        v_aug = jnp.concatenate([v_ref[0, :, h * dh:(h + 1) * dh], ones_col], axis=1)
        s = _dot_nt(qh, kh) * w_intra
        c_mat = c_scr[h]
        num = w_inter * _dot(qh, c_mat.astype(BF16)) + _dot(s.astype(BF16), v_aug)
        den = num[:, dh:dh + 1]
        hid = num[:, 0:dh] / jnp.maximum(jnp.abs(den), jnp.exp(-m_row))

        g_last = a_col[L - 1:L, :]
        log_w = g_last - a_col + ig_col
        m_new = jnp.maximum(g_last + m_prev, jnp.max(log_w, axis=0, keepdims=True))
        w_state = jnp.exp(log_w - m_new)
        decay = jnp.exp(g_last + m_prev - m_new)
        wv = (w_state * v_aug.astype(F32)).astype(BF16)
        c_scr[h] = decay * c_mat + _dot_tn(kh, wv)
        m_scr[h:h + 1, :] = jnp.broadcast_to(m_new, (1, m_scr.shape[1]))

        mu = jnp.mean(hid, axis=-1, keepdims=True)
        cen = hid - mu
        var = jnp.mean(cen * cen, axis=-1, keepdims=True)
        hn = cen * lax.rsqrt(var + EPS) * ng_ref[:, h * dh:(h + 1) * dh]
        gate = _sigmoid(o_ref[0, :, h * dh:(h + 1) * dh].astype(F32))
        y_ref[0, :, h * dh:(h + 1) * dh] = (gate * hn).astype(y_ref.dtype)


def _mlstm(proj, gcol, grow, conv_w, conv_b, gb_col, gb_row, norm_g, col0):
    bsz, seq, _ = proj.shape
    width = norm_g.shape[-1]
    dh = width // ML_HEADS
    L = min(ML_CHUNK, seq)
    blk0 = col0 // width
    spec = lambda k: pl.BlockSpec((1, L, width), lambda b, c: (b, c, blk0 + k))
    return pl.pallas_call(
        _mlstm_body,
        grid=(bsz, seq // L),
        in_specs=[
            spec(0), spec(1), spec(2), spec(3),
            pl.BlockSpec((1, L, LANE), lambda b, c: (b, c, 0)),
            pl.BlockSpec((1, SUBLANE, L), lambda b, c: (b, 0, c)),
            pl.BlockSpec(conv_w.shape, lambda b, c: (0, 0)),
            pl.BlockSpec(conv_b.shape, lambda b, c: (0, 0)),
            pl.BlockSpec(gb_col.shape, lambda b, c: (0, 0)),
            pl.BlockSpec(gb_row.shape, lambda b, c: (0, 0)),
            pl.BlockSpec(norm_g.shape, lambda b, c: (0, 0)),
        ],
        out_specs=pl.BlockSpec((1, L, width), lambda b, c: (b, c, 0)),
        out_shape=jax.ShapeDtypeStruct((bsz, seq, width), BF16),
        scratch_shapes=[
            pltpu.VMEM((L + 2 * SUBLANE, 2 * width), F32),
            pltpu.VMEM((ML_HEADS, dh, 2 * dh), F32),
            pltpu.VMEM((SUBLANE, LANE), F32),
        ],
        compiler_params=_params("parallel", "arbitrary"),
        name="mlstm",
    )(proj, proj, proj, proj, gcol, grow, conv_w, conv_b, gb_col, gb_row, norm_g)


def _memkv_body(mem_ref, g_ref, wk_ref, wv_ref, k_ref, v_ref):
    mb = _rms(mem_ref[0], g_ref[...]).astype(BF16)
    k_ref[0] = _dot(mb, wk_ref[...]).astype(BF16)
    v_ref[0] = _dot(mb, wv_ref[...]).astype(BF16)


def _memkv(mem, g, wk, wv):
    bsz, mlen, d = mem.shape
    full = lambda a: pl.BlockSpec(a.shape, lambda b: (0,) * a.ndim)
    blk = pl.BlockSpec((1, mlen, d), lambda b: (b, 0, 0))
    return pl.pallas_call(
        _memkv_body,
        grid=(bsz,),
        in_specs=[blk, full(g), full(wk), full(wv)],
        out_specs=[blk, blk],
        out_shape=[jax.ShapeDtypeStruct((bsz, mlen, d), BF16)] * 2,
        compiler_params=_params("parallel"),
        name="memkv",
    )(mem, g, wk, wv)


def _mix_body(x_ref, yda_ref, yml_ref, wo1_ref, wo2_ref, gx_ref, wq_ref, km_ref, vm_ref, wxo_ref, gf_ref,
              wrh_ref, wrl_ref, br_ref, x2_ref, route_ref):
    x1 = x_ref[0] + _dot(yda_ref[0], wo1_ref[...]) + _dot(yml_ref[0], wo2_ref[...])

    d = x1.shape[-1]
    dh = d // X_HEADS
    q = _dot(_rms(x1, gx_ref[...]).astype(BF16), wq_ref[...]).astype(BF16)
    heads = []
    for h in range(X_HEADS):
        sl = slice(h * dh, (h + 1) * dh)
        s = _dot_nt(q[:, sl], km_ref[0, :, sl]) * (dh ** -0.5)
        p = jnp.exp(s - jnp.max(s, axis=-1, keepdims=True))
        o = _dot(p.astype(BF16), vm_ref[0, :, sl]) / jnp.sum(p, axis=-1, keepdims=True)
        heads.append(o.astype(BF16))
    x2 = x1 + _dot(jnp.concatenate(heads, axis=1), wxo_ref[...])
    x2_ref[0] = x2

    h3 = _rms(x2, gf_ref[...])
    h_hi, h_lo = _split_bf16(h3)
    logits = (_dot_nt(wrh_ref[...], h_hi) + _dot_nt(wrh_ref[...], h_lo) + _dot_nt(wrl_ref[...], h_hi)
              + br_ref[...])
    tm = logits.shape[1]
    gl = logits[0:N_GROUPS, :]
    g_iota = lax.broadcasted_iota(jnp.int32, gl.shape, 0)
    g_max = jnp.max(gl, axis=0, keepdims=True)
    g_idx = jnp.min(jnp.where(gl == g_max, g_iota, N_GROUPS), axis=0, keepdims=True)
    g_w = 1.0 / jnp.sum(jnp.exp(gl - g_max), axis=0, keepdims=True)
    el = jnp.zeros((EXPERTS_PER_GROUP, tm), F32)
    for g in range(N_GROUPS):
        lo = N_GROUPS + g * EXPERTS_PER_GROUP
        el = jnp.where(g_idx == g, logits[lo:lo + EXPERTS_PER_GROUP, :], el)
    e_iota = lax.broadcasted_iota(jnp.int32, el.shape, 0)
    e1 = jnp.max(el, axis=0, keepdims=True)
    i1 = jnp.min(jnp.where(el == e1, e_iota, EXPERTS_PER_GROUP), axis=0, keepdims=True)
    el2 = jnp.where(e_iota == i1, NEG_INF, el)
    e2 = jnp.max(el2, axis=0, keepdims=True)
    i2 = jnp.min(jnp.where(el2 == e2, e_iota, EXPERTS_PER_GROUP), axis=0, keepdims=True)
    p2 = jnp.exp(e2 - e1)
    w1 = g_w / (1.0 + p2)
    w2 = g_w * p2 / (1.0 + p2)
    base = g_idx * EXPERTS_PER_GROUP
    zero = jnp.zeros((1, tm), F32)
    route_ref[0] = jnp.concatenate(
        [(base + i1).astype(F32), (base + i2).astype(F32), w1, w2, zero, zero, zero, zero], axis=0)


def _mix(x, yda, yml, wo1, wo2, gx, wq, kmem, vmem, wxo, gf, wr_hi, wr_lo, br):
    bsz, seq, d = x.shape
    tm = min(ROW_TILE, seq)
    mlen = kmem.shape[1]
    full = lambda a: pl.BlockSpec(a.shape, lambda b, i: (0,) * a.ndim)
    tok = lambda w: pl.BlockSpec((1, tm, w), lambda b, i: (b, i, 0))
    memspec = pl.BlockSpec((1, mlen, d), lambda b, i: (b, 0, 0))
    return pl.pallas_call(
        _mix_body,
        grid=(bsz, seq // tm),
        in_specs=[tok(d), tok(yda.shape[-1]), tok(yml.shape[-1]), full(wo1), full(wo2), full(gx), full(wq),
                  memspec, memspec, full(wxo), full(gf), full(wr_hi), full(wr_lo), full(br)],
        out_specs=[tok(d), pl.BlockSpec((1, SUBLANE, tm), lambda b, i: (b, 0, i))],
        out_shape=[
            jax.ShapeDtypeStruct((bsz, seq, d), F32),
            jax.ShapeDtypeStruct((bsz, SUBLANE, seq), F32),
        ],
        compiler_params=_params("parallel", "parallel"),
        name="mix_xattn_router",
    )(x, yda, yml, wo1, wo2, gx, wq, kmem, vmem, wxo, gf, wr_hi, wr_lo, br)


def _moe_body(offs_ref, tok_ref, wt_ref, x_hbm, gf_ref, gfin_ref, w1_ref, w3_ref, w2_ref, out_hbm,
              acc, h3, xg, yb, sems, *, seg, final_norm):
    tile = pl.program_id(0)
    e = pl.program_id(1)
    tt = acc.shape[0]
    norm_rows = MOE_ROWS

    def for_row_chunks(fn):
        def chunk(i, c):
            fn(pl.ds(pl.multiple_of(i * norm_rows, norm_rows), norm_rows))
            return c
        lax.fori_loop(0, tt // norm_rows, chunk, 0)

    @pl.when(jnp.logical_and(tile == 0, e == 0))
    def _():
        xg[...] = jnp.zeros(xg.shape, F32)

    @pl.when(e == 0)
    def _():
        load = pltpu.make_async_copy(x_hbm.at[tile], acc, sems.at[0])
        load.start()
        load.wait()

        def ffn_norm(rows):
            h3[rows, :] = _rms(acc[rows, :], gf_ref[...])
        for_row_chunks(ffn_norm)

    start = offs_ref[tile * (N_EXPERTS + 1) + e]
    end = offs_ref[tile * (N_EXPERTS + 1) + e + 1]
    flat0 = tile * seg
    group = SUBLANE

    def run_block(base, cnt, rows):
        idx0 = flat0 + base
        n_groups = cnt // group

        def gather_group(i, c):
            for u in range(group):
                r = i * group + u
                xg[pl.ds(r, 1), :] = h3[pl.ds(tok_ref[idx0 + r], 1), :]
            return c

        def gather_row(r, c):
            xg[pl.ds(r, 1), :] = h3[pl.ds(tok_ref[idx0 + r], 1), :]
            return c

        lax.fori_loop(0, n_groups, gather_group, 0)
        lax.fori_loop(n_groups * group, cnt, gather_row, 0)

        xb = xg[0:rows, :].astype(BF16)
        a = _dot(xb, w1_ref[0])
        b = _dot(xb, w3_ref[0])
        hmid = (a * _sigmoid(a) * b).astype(BF16)
        yb[0:rows, :] = _dot(hmid, w2_ref[0])

        def scatter_group(i, c):
            toks = [tok_ref[idx0 + i * group + u] for u in range(group)]
            new = [acc[pl.ds(toks[u], 1), :] + wt_ref[idx0 + i * group + u] * yb[pl.ds(i * group + u, 1), :]
                   for u in range(group)]
            for u in range(group):
                acc[pl.ds(toks[u], 1), :] = new[u]
            return c

        def scatter_row(r, c):
            t = tok_ref[idx0 + r]
            acc[pl.ds(t, 1), :] = acc[pl.ds(t, 1), :] + wt_ref[idx0 + r] * yb[pl.ds(r, 1), :]
            return c

        lax.fori_loop(0, n_groups, scatter_group, 0)
        lax.fori_loop(n_groups * group, cnt, scatter_row, 0)

    big = xg.shape[0]
    small = big // 2
    n_small = (end - start + small - 1) // small
    n_big = n_small // 2

    def big_block(i, c):
        base = start + i * big
        run_block(base, jnp.minimum(big, end - base), big)
        return c

    lax.fori_loop(0, n_big, big_block, 0)

    @pl.when(n_small % 2 == 1)
    def _():
        base = start + n_big * big
        run_block(base, end - base, small)

    @pl.when(e == pl.num_programs(1) - 1)
    def _():
        if final_norm:
            def apply_final_norm(rows):
                acc[rows, :] = _rms(acc[rows, :], gfin_ref[...])
            for_row_chunks(apply_final_norm)
        store = pltpu.make_async_copy(acc, out_hbm.at[tile], sems.at[1])
        store.start()
        store.wait()


def _moe(offs, tok, wt, x_tiles, gf, gfin, w1, w3, w2, final_norm):
    ntiles, tt, d = x_tiles.shape
    de = w1.shape[-1]
    body = functools.partial(_moe_body, seg=tok.shape[0] // ntiles, final_norm=final_norm)
    grid_spec = pltpu.PrefetchScalarGridSpec(
        num_scalar_prefetch=3,
        grid=(ntiles, N_EXPERTS),
        in_specs=[
            pl.BlockSpec(memory_space=pl.ANY),
            pl.BlockSpec((1, d), lambda t, e, *_: (0, 0)),
            pl.BlockSpec((1, d), lambda t, e, *_: (0, 0)),
            pl.BlockSpec((1, d, de), lambda t, e, *_: (e, 0, 0)),
            pl.BlockSpec((1, d, de), lambda t, e, *_: (e, 0, 0)),
            pl.BlockSpec((1, de, d), lambda t, e, *_: (e, 0, 0)),
        ],
        out_specs=pl.BlockSpec(memory_space=pl.ANY),
        scratch_shapes=[
            pltpu.VMEM((tt, d), F32),
            pltpu.VMEM((tt, d), F32),
            pltpu.VMEM((2 * MOE_ROWS, d), F32),
            pltpu.VMEM((2 * MOE_ROWS, d), F32),
            pltpu.SemaphoreType.DMA((2,)),
        ],
    )
    return pl.pallas_call(
        body,
        grid_spec=grid_spec,
        out_shape=jax.ShapeDtypeStruct((ntiles, tt, d), F32),
        compiler_params=_params("arbitrary", "arbitrary"),
        name="moe",
    )(offs, tok, wt, x_tiles, gf, gfin, w1, w3, w2)


def _rope_tables(seq):
    half = DA_HEAD_DIM // 2
    inv_freq = ROPE_THETA ** (-jnp.arange(0, DA_HEAD_DIM, 2, dtype=F32) / DA_HEAD_DIM)
    ang = jnp.arange(seq, dtype=F32)[:, None] * inv_freq[None, :]
    cos, sin = jnp.cos(ang), jnp.sin(ang)
    reps = LANE // half
    cos_t = jnp.tile(cos, (1, reps))
    sin_t = jnp.tile(jnp.concatenate([-sin, sin], axis=1), (1, reps // 2))
    return cos_t, sin_t


def _sort_assignments(route, tt):
    bsz, _, seq = route.shape
    per = seq // tt
    ntiles = bsz * per
    tile = lambda a: a.reshape(bsz, TOP_K_ROWS, per, tt).transpose(0, 2, 1, 3).reshape(ntiles, TOP_K_ROWS * tt)
    keys = tile(route[:, 0:TOP_K_ROWS, :]).astype(jnp.int32)
    wts = tile(route[:, TOP_K_ROWS:2 * TOP_K_ROWS, :])
    order = jnp.argsort(keys, axis=-1).astype(jnp.int32)
    tok = order % tt
    wt = jnp.take_along_axis(wts, order, axis=-1)
    counts = jnp.sum(keys[:, :, None] == jnp.arange(N_EXPERTS, dtype=jnp.int32)[None, None, :], axis=1,
                     dtype=jnp.int32)
    offs = jnp.concatenate([jnp.zeros((ntiles, 1), jnp.int32), jnp.cumsum(counts, axis=-1, dtype=jnp.int32)],
                           axis=-1)
    return offs.reshape(-1), tok.reshape(-1), wt.reshape(-1)


TOP_K_ROWS = 2


def kernel(x, mem, norm_mix_g, w_in, w_out, da_lambda, da_subln_g, ml_conv_w, ml_conv_b, ml_gate_b, ml_norm_g, norm_x_g, norm_mem_g, w_xq, w_xk, w_xv, w_xo, norm_ffn_g, w_router_group, b_router_group, w_router_expert, b_router_expert, w1, w3, w2, norm_final_g):
    depth = w_in.shape[0]
    bsz, seq, d = x.shape
    da_qk = DA_HEADS * 2 * DA_HEAD_DIM
    da_width = DA_HEADS * DA_VAL_DIM
    ml_width = ml_norm_g.shape[-1]
    n_main = 2 * da_qk + da_width + 4 * ml_width
    n_in = w_in.shape[-1]
    cos_t, sin_t = _rope_tables(seq)
    row = lambda v: v.reshape(1, -1).astype(F32)

    for l in range(depth):
        lam_init = 0.8 - 0.6 * math.exp(-0.3 * l)
        w_pad = jnp.pad(w_in[l], ((0, 0), (0, n_main + LANE - n_in))).astype(BF16)
        qt, vt, proj, gcol, grow = _inproj(x, row(norm_mix_g[l]), w_pad, cos_t, sin_t)

        yda = _diffattn(qt, proj, vt, da_lambda[l].astype(F32), da_subln_g[l].astype(F32).reshape(-1, 1), lam_init)

        gb = ml_gate_b[l].astype(F32).reshape(1, 2 * ML_HEADS)
        gb_col = jnp.pad(gb, ((0, 0), (0, LANE - 2 * ML_HEADS)))
        gb_row = gb.reshape(2 * ML_HEADS, 1)
        yml = _mlstm(proj, gcol, grow, ml_conv_w[l].reshape(ML_CONV, 2 * ml_width).astype(F32),
                     row(ml_conv_b[l]), gb_col, gb_row, row(ml_norm_g[l]), da_qk)

        kmem, vmem = _memkv(mem, row(norm_mem_g[l]), w_xk[l].astype(BF16), w_xv[l].astype(BF16))

        wr = jnp.concatenate([w_router_group[l], w_router_expert[l]], axis=1).astype(F32).T
        wr = jnp.pad(wr, ((0, ROUTER_ROWS - wr.shape[0]), (0, 0)))
        wr_hi = wr.astype(BF16)
        wr_lo = (wr - wr_hi.astype(F32)).astype(BF16)
        br = jnp.pad(jnp.concatenate([b_router_group[l], b_router_expert[l]]).astype(F32),
                     (0, ROUTER_ROWS - N_GROUPS - N_EXPERTS)).reshape(ROUTER_ROWS, 1)
        wo = w_out[l].astype(BF16)
        x2, route = _mix(x, yda, yml, wo[:da_width], wo[da_width:], row(norm_x_g[l]), w_xq[l].astype(BF16),
                         kmem, vmem, w_xo[l].astype(BF16), row(norm_ffn_g[l]), wr_hi, wr_lo, br)

        tt = min(MOE_TILE, seq)
        offs, tok, wt = _sort_assignments(route, tt)
        x = _moe(offs, tok, wt, x2.reshape(bsz * seq // tt, tt, d), row(norm_ffn_g[l]), row(norm_final_g),
                 w1[l].astype(BF16), w3[l].astype(BF16), w2[l].astype(BF16),
                 final_norm=(l + 1 == depth)).reshape(bsz, seq, d)
    return x
```

```python
import functools
import math

import jax
import jax.numpy as jnp
from jax import lax
from jax.experimental import pallas as pl
from jax.experimental.pallas import tpu as pltpu

F32 = jnp.float32
BF16 = jnp.bfloat16

LANE = 128
SUBLANE = 8
VMEM_LIMIT_BYTES = 56 * 1024 * 1024

EPS = 1e-6
ROPE_THETA = 10000.0
DA_HEADS = 4
DA_HEAD_DIM = 64
DA_VAL_DIM = 2 * DA_HEAD_DIM
ML_HEADS = 4
ML_CONV = 4
X_HEADS = 4
N_GROUPS = 4
EXPERTS_PER_GROUP = 8
N_EXPERTS = N_GROUPS * EXPERTS_PER_GROUP
ROUTER_ROWS = 40

ROW_TILE = 512
ML_CHUNK = 256
MOE_TILE = 4096
MOE_ROWS = 128

NEG_INF = float("-inf")
LOG2_E = math.log2(math.e)


def _rms(x, g):
    return x * lax.rsqrt(jnp.mean(x * x, axis=-1, keepdims=True) + EPS) * g


def _dot(a, b):
    return jnp.dot(a, b, preferred_element_type=F32)


def _dot_nt(a, b):
    return lax.dot_general(a, b, (((1,), (1,)), ((), ())), preferred_element_type=F32)


def _dot_tn(a, b):
    return lax.dot_general(a, b, (((0,), (0,)), ((), ())), preferred_element_type=F32)


def _split_bf16(x):
    hi = x.astype(BF16)
    lo = (x - hi.astype(F32)).astype(BF16)
    return hi, lo


def _params(*semantics):
    return pltpu.CompilerParams(dimension_semantics=semantics, vmem_limit_bytes=VMEM_LIMIT_BYTES)


def _inproj_body(x_ref, g_ref, w_ref, cos_ref, sin_ref, qt_ref, vt_ref, proj_ref, gcol_ref, grow_ref):
    hb = _rms(x_ref[0], g_ref[...]).astype(BF16)
    cos = cos_ref[...]
    sin = sin_ref[...]
    lane = lax.broadcasted_iota(jnp.int32, cos.shape, 1)
    first_half = (lane % DA_HEAD_DIM) < (DA_HEAD_DIM // 2)
    n_main = w_ref.shape[1] - LANE
    wide = 4 * LANE
    for c in range(n_main // LANE):
        if c % (wide // LANE) == 0:
            acc = _dot(hb, w_ref[:, c * LANE:c * LANE + wide])
        t = acc[:, (c * LANE) % wide:(c * LANE) % wide + LANE]
        if c < 2 * DA_HEADS:
            rot = jnp.where(first_half, pltpu.roll(t, LANE - DA_HEAD_DIM // 2, 1),
                            pltpu.roll(t, DA_HEAD_DIM // 2, 1))
            t = t * cos + rot * sin
        if c < DA_HEADS:
            qt_ref[0, 0, c * LANE:(c + 1) * LANE, :] = (t * (DA_HEAD_DIM ** -0.5 * LOG2_E)).T.astype(BF16)
        elif c < 2 * DA_HEADS:
            proj_ref[0, :, (c - DA_HEADS) * LANE:(c - DA_HEADS + 1) * LANE] = t.astype(BF16)
        elif c < 3 * DA_HEADS:
            vt_ref[0, 0, (c - 2 * DA_HEADS) * LANE:(c - 2 * DA_HEADS + 1) * LANE, :] = t.T.astype(BF16)
        else:
            proj_ref[0, :, (c - 2 * DA_HEADS) * LANE:(c - 2 * DA_HEADS + 1) * LANE] = t.astype(BF16)
    gates = _dot(hb, w_ref[:, n_main:n_main + LANE])
    gcol_ref[0] = gates
    grow_ref[0] = gates.T[:SUBLANE, :]


def _inproj(x, g, w_pad, cos_t, sin_t):
    bsz, seq, d = x.shape
    tm = min(ROW_TILE, seq)
    nt = seq // tm
    da_w = DA_HEADS * DA_VAL_DIM
    n_proj = w_pad.shape[1] - LANE - 2 * da_w
    return pl.pallas_call(
        _inproj_body,
        grid=(bsz, nt),
        in_specs=[
            pl.BlockSpec((1, tm, d), lambda b, i: (b, i, 0)),
            pl.BlockSpec((1, d), lambda b, i: (0, 0)),
            pl.BlockSpec(w_pad.shape, lambda b, i: (0, 0)),
            pl.BlockSpec((tm, LANE), lambda b, i: (i, 0)),
            pl.BlockSpec((tm, LANE), lambda b, i: (i, 0)),
        ],
        out_specs=[
            pl.BlockSpec((1, 1, da_w, tm), lambda b, i: (b, i, 0, 0)),
            pl.BlockSpec((1, 1, da_w, tm), lambda b, i: (b, i, 0, 0)),
            pl.BlockSpec((1, tm, n_proj), lambda b, i: (b, i, 0)),
            pl.BlockSpec((1, tm, LANE), lambda b, i: (b, i, 0)),
            pl.BlockSpec((1, SUBLANE, tm), lambda b, i: (b, 0, i)),
        ],
        out_shape=[
            jax.ShapeDtypeStruct((bsz, nt, da_w, tm), BF16),
            jax.ShapeDtypeStruct((bsz, nt, da_w, tm), BF16),
            jax.ShapeDtypeStruct((bsz, seq, n_proj), BF16),
            jax.ShapeDtypeStruct((bsz, seq, LANE), F32),
            jax.ShapeDtypeStruct((bsz, SUBLANE, seq), F32),
        ],
        compiler_params=_params("parallel", "parallel"),
        name="inproj",
    )(x, g, w_pad, cos_t, sin_t)


def _diffattn_body(qt_ref, k_ref, vt_ref, lam_ref, g_ref, o_ref, m_scr, acc_scr, p_scr, *, lam_init):
    qi = pl.program_id(2)
    tq = qt_ref.shape[3]
    qt = qt_ref[0, 0]
    row = lax.broadcasted_iota(jnp.int32, qt.shape, 0)
    zero = jnp.zeros_like(qt)
    q2t = jnp.concatenate([jnp.where(row < DA_HEAD_DIM, qt, zero), jnp.where(row >= DA_HEAD_DIM, qt, zero)], axis=1)
    pad_rows = acc_scr.shape[0] - DA_VAL_DIM
    ones_rows = jnp.where(lax.broadcasted_iota(jnp.int32, (pad_rows, tq), 0) == 0, 1.0, 0.0).astype(BF16)

    m_scr[...] = jnp.full(m_scr.shape, NEG_INF, F32)
    acc_scr[...] = jnp.zeros(acc_scr.shape, F32)

    n_chains = 4
    width = 2 * tq // n_chains
    halves = tuple(slice(c * width, (c + 1) * width) for c in range(n_chains))

    def scores(j, masked, prev_values):
        k = k_ref[0, pl.ds(pl.multiple_of(j * tq, tq), tq), :]
        out = []
        for cols in halves:
            s = _dot(k, q2t[:, cols])
            pv = values(j - 1, cols) if prev_values else None
            if masked:
                key = lax.broadcasted_iota(jnp.int32, s.shape, 0)
                qry = lax.broadcasted_iota(jnp.int32, s.shape, 1) + cols.start % tq
                s = jnp.where(key <= qry, s, NEG_INF)
            m_prev = m_scr[:, cols]
            m_new = jnp.maximum(m_prev, jnp.max(s, axis=0, keepdims=True))
            p_scr[:, cols] = jnp.exp2(s - m_new).astype(BF16)
            m_scr[:, cols] = m_new
            out.append((jnp.exp2(m_prev - m_new), pv))
        return out

    def values(j, cols):
        v_aug = jnp.concatenate([vt_ref[0, j], ones_rows], axis=0)
        return _dot(v_aug, p_scr[:, cols])

    def step(j, masked):
        for cols, (alpha, pv) in zip(halves, scores(j, masked, True)):
            acc_scr[:, cols] = alpha * (acc_scr[:, cols] + pv)

    def full_step_pair(i, carry):
        step(1 + 2 * i, False)
        step(2 + 2 * i, False)
        return carry

    @pl.when(qi == 0)
    def _():
        scores(0, True, False)

    @pl.when(qi > 0)
    def _():
        scores(0, False, False)
        lax.fori_loop(0, (qi - 1) // 2, full_step_pair, 0)

        @pl.when((qi - 1) % 2 == 1)
        def _():
            step(qi - 1, False)

        step(qi, True)

    for cols in halves:
        acc_scr[:, cols] = acc_scr[:, cols] + values(qi, cols)

    lv = lam_ref[...]
    lam = (jnp.exp(jnp.sum(lv[0:1, :] * lv[1:2, :], axis=-1, keepdims=True))
           - jnp.exp(jnp.sum(lv[2:3, :] * lv[3:4, :], axis=-1, keepdims=True)) + lam_init)
    acc = acc_scr[...]
    o = acc[0:DA_VAL_DIM, :] / acc[DA_VAL_DIM:DA_VAL_DIM + 1, :]
    o = o[:, :tq] - lam * o[:, tq:]
    y = o * lax.rsqrt(jnp.mean(o * o, axis=0, keepdims=True) + EPS) * g_ref[...] * (1.0 - lam_init)
    o_ref[0] = y.T.astype(o_ref.dtype)


def _diffattn(qt, proj, vt, da_lambda, subln_g_col, lam_init):
    bsz, nt, _, tq = qt.shape
    seq = nt * tq
    body = functools.partial(_diffattn_body, lam_init=lam_init)
    return pl.pallas_call(
        body,
        grid=(bsz, DA_HEADS, nt),
        in_specs=[
            pl.BlockSpec((1, 1, DA_VAL_DIM, tq), lambda b, h, i: (b, i, h, 0)),
            pl.BlockSpec((1, seq, LANE), lambda b, h, i: (b, 0, h)),
            pl.BlockSpec((1, nt, DA_VAL_DIM, tq), lambda b, h, i: (b, 0, h, 0)),
            pl.BlockSpec(da_lambda.shape, lambda b, h, i: (0, 0)),
            pl.BlockSpec((DA_VAL_DIM, 1), lambda b, h, i: (0, 0)),
        ],
        out_specs=pl.BlockSpec((1, tq, LANE), lambda b, h, i: (b, i, h)),
        out_shape=jax.ShapeDtypeStruct((bsz, seq, DA_HEADS * DA_VAL_DIM), BF16),
        scratch_shapes=[
            pltpu.VMEM((1, 2 * tq), F32),
            pltpu.VMEM((DA_VAL_DIM + 2 * SUBLANE, 2 * tq), F32),
            pltpu.VMEM((tq, 2 * tq), BF16),
        ],
        compiler_params=_params("parallel", "parallel", "arbitrary"),
        name="diffattn",
    )(qt, proj, vt, da_lambda, subln_g_col)


def _log_sigmoid(x):
    return jnp.minimum(x, 0.0) - jnp.log(1.0 + jnp.exp(-jnp.abs(x)))


def _sigmoid(x):
    return 1.0 / (1.0 + jnp.exp(-x))


def _mlstm_body(q_ref, k_ref, v_ref, o_ref, gcol_ref, grow_ref, cw_ref, cb_ref, gbcol_ref, gbrow_ref, ng_ref,
                y_ref, xbuf, c_scr, m_scr):
    chunk = pl.program_id(1)
    L = q_ref.shape[1]
    width = q_ref.shape[2]
    dh = width // ML_HEADS
    tail = SUBLANE

    @pl.when(chunk == 0)
    def _():
        xbuf[0:tail, :] = jnp.zeros((tail, 2 * width), F32)
        c_scr[...] = jnp.zeros(c_scr.shape, F32)
        m_scr[...] = jnp.zeros(m_scr.shape, F32)

    xbuf[tail:tail + L, 0:width] = q_ref[0].astype(F32)
    xbuf[tail:tail + L, width:2 * width] = k_ref[0].astype(F32)
    conv = cb_ref[...]
    for j in range(ML_CONV):
        conv = conv + cw_ref[j:j + 1, :] * xbuf[pl.ds(tail - (ML_CONV - 1) + j, L), :]
    xbuf[0:tail, :] = xbuf[L:L + tail, :]
    qk = conv * _sigmoid(conv)

    gc = gcol_ref[0] + gbcol_ref[...]
    gr = grow_ref[0] + gbrow_ref[...]
    row_id = lax.broadcasted_iota(jnp.int32, (L, L), 0)
    col_id = lax.broadcasted_iota(jnp.int32, (L, L), 1)
    causal = col_id <= row_id
    tri = jnp.where(causal, 1.0, 0.0).astype(BF16)
    tri_t = jnp.where(row_id <= col_id, 1.0, 0.0).astype(BF16)
    lfc_hi, lfc_lo = _split_bf16(_log_sigmoid(gc))
    lfr_hi, lfr_lo = _split_bf16(_log_sigmoid(gr))
    a_c = _dot(tri, lfc_hi) + _dot(tri, lfc_lo)
    a_r = _dot(lfr_hi, tri_t) + _dot(lfr_lo, tri_t)

    lane = lax.broadcasted_iota(jnp.int32, (L, dh), 1)
    ones_col = jnp.where(lane == 0, 1.0, 0.0).astype(BF16)

    for h in range(ML_HEADS):
        fh = ML_HEADS + h
        m_prev = m_scr[h:h + 1, 0:1]
        a_col = a_c[:, fh:fh + 1]
        ig_col = gc[:, h:h + 1]
        a_row = a_r[fh:fh + 1, :]
        ig_row = gr[h:h + 1, :]
        log_d = jnp.where(causal, a_col - (a_row - ig_row), NEG_INF)
        log_inter = a_col + m_prev
        m_row = jnp.maximum(log_inter, jnp.max(log_d, axis=-1, keepdims=True))
        w_intra = jnp.exp(log_d - m_row)
        w_inter = jnp.exp(log_inter - m_row)

        qh = qk[:, h * dh:(h + 1) * dh].astype(BF16)
        kh = (qk[:, width + h * dh:width + (h + 1) * dh] * (dh ** -0.5)).astype(BF16)
        v_aug = jnp.concatenate([v_ref[0, :, h * dh:(h + 1) * dh], ones_col], axis=1)
        s = _dot_nt(qh, kh) * w_intra
        c_mat = c_scr[h]
        num = w_inter * _dot(qh, c_mat.astype(BF16)) + _dot(s.astype(BF16), v_aug)
        den = num[:, dh:dh + 1]
        hid = num[:, 0:dh] / jnp.maximum(jnp.abs(den), jnp.exp(-m_row))

        g_last = a_col[L - 1:L, :]
        log_w = g_last - a_col + ig_col
        m_new = jnp.maximum(g_last + m_prev, jnp.max(log_w, axis=0, keepdims=True))
        w_state = jnp.exp(log_w - m_new)
        decay = jnp.exp(g_last + m_prev - m_new)
        wv = (w_state * v_aug.astype(F32)).astype(BF16)
        c_scr[h] = decay * c_mat + _dot_tn(kh, wv)
        m_scr[h:h + 1, :] = jnp.broadcast_to(m_new, (1, m_scr.shape[1]))

        mu = jnp.mean(hid, axis=-1, keepdims=True)
        cen = hid - mu
        var = jnp.mean(cen * cen, axis=-1, keepdims=True)
        hn = cen * lax.rsqrt(var + EPS) * ng_ref[:, h * dh:(h + 1) * dh]
        gate = _sigmoid(o_ref[0, :, h * dh:(h + 1) * dh].astype(F32))
        y_ref[0, :, h * dh:(h + 1) * dh] = (gate * hn).astype(y_ref.dtype)


def _mlstm(proj, gcol, grow, conv_w, conv_b, gb_col, gb_row, norm_g, col0):
    bsz, seq, _ = proj.shape
    width = norm_g.shape[-1]
    dh = width // ML_HEADS
    L = min(ML_CHUNK, seq)
    blk0 = col0 // width
    spec = lambda k: pl.BlockSpec((1, L, width), lambda b, c: (b, c, blk0 + k))
    return pl.pallas_call(
        _mlstm_body,
        grid=(bsz, seq // L),
        in_specs=[
            spec(0), spec(1), spec(2), spec(3),
            pl.BlockSpec((1, L, LANE), lambda b, c: (b, c, 0)),
            pl.BlockSpec((1, SUBLANE, L), lambda b, c: (b, 0, c)),
            pl.BlockSpec(conv_w.shape, lambda b, c: (0, 0)),
            pl.BlockSpec(conv_b.shape, lambda b, c: (0, 0)),
            pl.BlockSpec(gb_col.shape, lambda b, c: (0, 0)),
            pl.BlockSpec(gb_row.shape, lambda b, c: (0, 0)),
            pl.BlockSpec(norm_g.shape, lambda b, c: (0, 0)),
        ],
        out_specs=pl.BlockSpec((1, L, width), lambda b, c: (b, c, 0)),
        out_shape=jax.ShapeDtypeStruct((bsz, seq, width), BF16),
        scratch_shapes=[
            pltpu.VMEM((L + 2 * SUBLANE, 2 * width), F32),
            pltpu.VMEM((ML_HEADS, dh, 2 * dh), F32),
            pltpu.VMEM((SUBLANE, LANE), F32),
        ],
        compiler_params=_params("parallel", "arbitrary"),
        name="mlstm",
    )(proj, proj, proj, proj, gcol, grow, conv_w, conv_b, gb_col, gb_row, norm_g)


def _memkv_body(mem_ref, g_ref, wk_ref, wv_ref, k_ref, v_ref):
    mb = _rms(mem_ref[0], g_ref[...]).astype(BF16)
    k_ref[0] = _dot(mb, wk_ref[...]).astype(BF16)
    v_ref[0] = _dot(mb, wv_ref[...]).astype(BF16)


def _memkv(mem, g, wk, wv):
    bsz, mlen, d = mem.shape
    full = lambda a: pl.BlockSpec(a.shape, lambda b: (0,) * a.ndim)
    blk = pl.BlockSpec((1, mlen, d), lambda b: (b, 0, 0))
    return pl.pallas_call(
        _memkv_body,
        grid=(bsz,),
        in_specs=[blk, full(g), full(wk), full(wv)],
        out_specs=[blk, blk],
        out_shape=[jax.ShapeDtypeStruct((bsz, mlen, d), BF16)] * 2,
        compiler_params=_params("parallel"),
        name="memkv",
    )(mem, g, wk, wv)


def _mix_body(x_ref, yda_ref, yml_ref, wo1_ref, wo2_ref, gx_ref, wq_ref, km_ref, vm_ref, wxo_ref, gf_ref,
              wrh_ref, wrl_ref, br_ref, x2_ref, route_ref):
    x1 = x_ref[0] + _dot(yda_ref[0], wo1_ref[...]) + _dot(yml_ref[0], wo2_ref[...])

    d = x1.shape[-1]
    dh = d // X_HEADS
    q = _dot(_rms(x1, gx_ref[...]).astype(BF16), wq_ref[...]).astype(BF16)
    heads = []
    for h in range(X_HEADS):
        sl = slice(h * dh, (h + 1) * dh)
        s = _dot_nt(q[:, sl], km_ref[0, :, sl]) * (dh ** -0.5)
        p = jnp.exp(s - jnp.max(s, axis=-1, keepdims=True))
        o = _dot(p.astype(BF16), vm_ref[0, :, sl]) / jnp.sum(p, axis=-1, keepdims=True)
        heads.append(o.astype(BF16))
    x2 = x1 + _dot(jnp.concatenate(heads, axis=1), wxo_ref[...])
    x2_ref[0] = x2

    h3 = _rms(x2, gf_ref[...])
    h_hi, h_lo = _split_bf16(h3)
    logits = (_dot_nt(wrh_ref[...], h_hi) + _dot_nt(wrh_ref[...], h_lo) + _dot_nt(wrl_ref[...], h_hi)
              + br_ref[...])
    tm = logits.shape[1]
    gl = logits[0:N_GROUPS, :]
    g_iota = lax.broadcasted_iota(jnp.int32, gl.shape, 0)
    g_max = jnp.max(gl, axis=0, keepdims=True)
    g_idx = jnp.min(jnp.where(gl == g_max, g_iota, N_GROUPS), axis=0, keepdims=True)
    g_w = 1.0 / jnp.sum(jnp.exp(gl - g_max), axis=0, keepdims=True)
    el = jnp.zeros((EXPERTS_PER_GROUP, tm), F32)
    for g in range(N_GROUPS):
        lo = N_GROUPS + g * EXPERTS_PER_GROUP
        el = jnp.where(g_idx == g, logits[lo:lo + EXPERTS_PER_GROUP, :], el)
    e_iota = lax.broadcasted_iota(jnp.int32, el.shape, 0)
    e1 = jnp.max(el, axis=0, keepdims=True)
    i1 = jnp.min(jnp.where(el == e1, e_iota, EXPERTS_PER_GROUP), axis=0, keepdims=True)
    el2 = jnp.where(e_iota == i1, NEG_INF, el)
    e2 = jnp.max(el2, axis=0, keepdims=True)
    i2 = jnp.min(jnp.where(el2 == e2, e_iota, EXPERTS_PER_GROUP), axis=0, keepdims=True)
    p2 = jnp.exp(e2 - e1)
    w1 = g_w / (1.0 + p2)
    w2 = g_w * p2 / (1.0 + p2)
    base = g_idx * EXPERTS_PER_GROUP
    zero = jnp.zeros((1, tm), F32)
    route_ref[0] = jnp.concatenate(
        [(base + i1).astype(F32), (base + i2).astype(F32), w1, w2, zero, zero, zero, zero], axis=0)


def _mix(x, yda, yml, wo1, wo2, gx, wq, kmem, vmem, wxo, gf, wr_hi, wr_lo, br):
    bsz, seq, d = x.shape
    tm = min(ROW_TILE, seq)
    mlen = kmem.shape[1]
    full = lambda a: pl.BlockSpec(a.shape, lambda b, i: (0,) * a.ndim)
    tok = lambda w: pl.BlockSpec((1, tm, w), lambda b, i: (b, i, 0))
    memspec = pl.BlockSpec((1, mlen, d), lambda b, i: (b, 0, 0))
    return pl.pallas_call(
        _mix_body,
        grid=(bsz, seq // tm),
        in_specs=[tok(d), tok(yda.shape[-1]), tok(yml.shape[-1]), full(wo1), full(wo2), full(gx), full(wq),
                  memspec, memspec, full(wxo), full(gf), full(wr_hi), full(wr_lo), full(br)],
        out_specs=[tok(d), pl.BlockSpec((1, SUBLANE, tm), lambda b, i: (b, 0, i))],
        out_shape=[
            jax.ShapeDtypeStruct((bsz, seq, d), F32),
            jax.ShapeDtypeStruct((bsz, SUBLANE, seq), F32),
        ],
        compiler_params=_params("parallel", "parallel"),
        name="mix_xattn_router",
    )(x, yda, yml, wo1, wo2, gx, wq, kmem, vmem, wxo, gf, wr_hi, wr_lo, br)


def _moe_body(offs_ref, tok_ref, wt_ref, x_hbm, gf_ref, gfin_ref, w1_ref, w3_ref, w2_ref, out_hbm,
              acc, h3, xg, yb, sems, *, seg, final_norm):
    tile = pl.program_id(0)
    e = pl.program_id(1)
    tt = acc.shape[0]
    norm_rows = MOE_ROWS

    def for_row_chunks(fn):
        def chunk(i, c):
            fn(pl.ds(pl.multiple_of(i * norm_rows, norm_rows), norm_rows))
            return c
        lax.fori_loop(0, tt // norm_rows, chunk, 0)

    @pl.when(jnp.logical_and(tile == 0, e == 0))
    def _():
        xg[...] = jnp.zeros(xg.shape, F32)

    @pl.when(e == 0)
    def _():
        load = pltpu.make_async_copy(x_hbm.at[tile], acc, sems.at[0])
        load.start()
        load.wait()

        def ffn_norm(rows):
            h3[rows, :] = _rms(acc[rows, :], gf_ref[...])
        for_row_chunks(ffn_norm)

    start = offs_ref[tile * (N_EXPERTS + 1) + e]
    end = offs_ref[tile * (N_EXPERTS + 1) + e + 1]
    flat0 = tile * seg
    group = SUBLANE

    def run_block(base, cnt, rows):
        idx0 = flat0 + base
        n_groups = cnt // group

        def gather_group(i, c):
            for u in range(group):
                r = i * group + u
                xg[pl.ds(r, 1), :] = h3[pl.ds(tok_ref[idx0 + r], 1), :]
            return c

        def gather_row(r, c):
            xg[pl.ds(r, 1), :] = h3[pl.ds(tok_ref[idx0 + r], 1), :]
            return c

        lax.fori_loop(0, n_groups, gather_group, 0)
        lax.fori_loop(n_groups * group, cnt, gather_row, 0)

        xb = xg[0:rows, :].astype(BF16)
        a = _dot(xb, w1_ref[0])
        b = _dot(xb, w3_ref[0])
        hmid = (a * _sigmoid(a) * b).astype(BF16)
        yb[0:rows, :] = _dot(hmid, w2_ref[0])

        def scatter_group(i, c):
            toks = [tok_ref[idx0 + i * group + u] for u in range(group)]
            new = [acc[pl.ds(toks[u], 1), :] + wt_ref[idx0 + i * group + u] * yb[pl.ds(i * group + u, 1), :]
                   for u in range(group)]
            for u in range(group):
                acc[pl.ds(toks[u], 1), :] = new[u]
            return c

        def scatter_row(r, c):
            t = tok_ref[idx0 + r]
            acc[pl.ds(t, 1), :] = acc[pl.ds(t, 1), :] + wt_ref[idx0 + r] * yb[pl.ds(r, 1), :]
            return c

        lax.fori_loop(0, n_groups, scatter_group, 0)
        lax.fori_loop(n_groups * group, cnt, scatter_row, 0)

    big = xg.shape[0]
    small = big // 2
    n_small = (end - start + small - 1) // small
    n_big = n_small // 2

    def big_block(i, c):
        base = start + i * big
        run_block(base, jnp.minimum(big, end - base), big)
        return c

    lax.fori_loop(0, n_big, big_block, 0)

    @pl.when(n_small % 2 == 1)
    def _():
        base = start + n_big * big
        run_block(base, end - base, small)

    @pl.when(e == pl.num_programs(1) - 1)
    def _():
        if final_norm:
            def apply_final_norm(rows):
                acc[rows, :] = _rms(acc[rows, :], gfin_ref[...])
            for_row_chunks(apply_final_norm)
        store = pltpu.make_async_copy(acc, out_hbm.at[tile], sems.at[1])
        store.start()
        store.wait()


def _moe(offs, tok, wt, x_tiles, gf, gfin, w1, w3, w2, final_norm):
    ntiles, tt, d = x_tiles.shape
    de = w1.shape[-1]
    body = functools.partial(_moe_body, seg=tok.shape[0] // ntiles, final_norm=final_norm)
    grid_spec = pltpu.PrefetchScalarGridSpec(
        num_scalar_prefetch=3,
        grid=(ntiles, N_EXPERTS),
        in_specs=[
            pl.BlockSpec(memory_space=pl.ANY),
            pl.BlockSpec((1, d), lambda t, e, *_: (0, 0)),
            pl.BlockSpec((1, d), lambda t, e, *_: (0, 0)),
            pl.BlockSpec((1, d, de), lambda t, e, *_: (e, 0, 0)),
            pl.BlockSpec((1, d, de), lambda t, e, *_: (e, 0, 0)),
            pl.BlockSpec((1, de, d), lambda t, e, *_: (e, 0, 0)),
        ],
        out_specs=pl.BlockSpec(memory_space=pl.ANY),
        scratch_shapes=[
            pltpu.VMEM((tt, d), F32),
            pltpu.VMEM((tt, d), F32),
            pltpu.VMEM((2 * MOE_ROWS, d), F32),
            pltpu.VMEM((2 * MOE_ROWS, d), F32),
            pltpu.SemaphoreType.DMA((2,)),
        ],
    )
    return pl.pallas_call(
        body,
        grid_spec=grid_spec,
        out_shape=jax.ShapeDtypeStruct((ntiles, tt, d), F32),
        compiler_params=_params("arbitrary", "arbitrary"),
        name="moe",
    )(offs, tok, wt, x_tiles, gf, gfin, w1, w3, w2)


def _rope_tables(seq):
    half = DA_HEAD_DIM // 2
    inv_freq = ROPE_THETA ** (-jnp.arange(0, DA_HEAD_DIM, 2, dtype=F32) / DA_HEAD_DIM)
    ang = jnp.arange(seq, dtype=F32)[:, None] * inv_freq[None, :]
    cos, sin = jnp.cos(ang), jnp.sin(ang)
    reps = LANE // half
    cos_t = jnp.tile(cos, (1, reps))
    sin_t = jnp.tile(jnp.concatenate([-sin, sin], axis=1), (1, reps // 2))
    return cos_t, sin_t


def _sort_assignments(route, tt):
    bsz, _, seq = route.shape
    per = seq // tt
    ntiles = bsz * per
    tile = lambda a: a.reshape(bsz, TOP_K_ROWS, per, tt).transpose(0, 2, 1, 3).reshape(ntiles, TOP_K_ROWS * tt)
    keys = tile(route[:, 0:TOP_K_ROWS, :]).astype(jnp.int32)
    wts = tile(route[:, TOP_K_ROWS:2 * TOP_K_ROWS, :])
    order = jnp.argsort(keys, axis=-1).astype(jnp.int32)
    tok = order % tt
    wt = jnp.take_along_axis(wts, order, axis=-1)
    counts = jnp.sum(keys[:, :, None] == jnp.arange(N_EXPERTS, dtype=jnp.int32)[None, None, :], axis=1,
                     dtype=jnp.int32)
    offs = jnp.concatenate([jnp.zeros((ntiles, 1), jnp.int32), jnp.cumsum(counts, axis=-1, dtype=jnp.int32)],
                           axis=-1)
    return offs.reshape(-1), tok.reshape(-1), wt.reshape(-1)


TOP_K_ROWS = 2


def kernel(x, mem, norm_mix_g, w_in, w_out, da_lambda, da_subln_g, ml_conv_w, ml_conv_b, ml_gate_b, ml_norm_g, norm_x_g, norm_mem_g, w_xq, w_xk, w_xv, w_xo, norm_ffn_g, w_router_group, b_router_group, w_router_expert, b_router_expert, w1, w3, w2, norm_final_g):
    depth = w_in.shape[0]
    bsz, seq, d = x.shape
    da_qk = DA_HEADS * 2 * DA_HEAD_DIM
    da_width = DA_HEADS * DA_VAL_DIM
    ml_width = ml_norm_g.shape[-1]
    n_main = 2 * da_qk + da_width + 4 * ml_width
    n_in = w_in.shape[-1]
    cos_t, sin_t = _rope_tables(seq)
    row = lambda v: v.reshape(1, -1).astype(F32)

    for l in range(depth):
        lam_init = 0.8 - 0.6 * math.exp(-0.3 * l)
        w_pad = jnp.pad(w_in[l], ((0, 0), (0, n_main + LANE - n_in))).astype(BF16)
        qt, vt, proj, gcol, grow = _inproj(x, row(norm_mix_g[l]), w_pad, cos_t, sin_t)

        yda = _diffattn(qt, proj, vt, da_lambda[l].astype(F32), da_subln_g[l].astype(F32).reshape(-1, 1), lam_init)

        gb = ml_gate_b[l].astype(F32).reshape(1, 2 * ML_HEADS)
        gb_col = jnp.pad(gb, ((0, 0), (0, LANE - 2 * ML_HEADS)))
        gb_row = gb.reshape(2 * ML_HEADS, 1)
        yml = _mlstm(proj, gcol, grow, ml_conv_w[l].reshape(ML_CONV, 2 * ml_width).astype(F32),
                     row(ml_conv_b[l]), gb_col, gb_row, row(ml_norm_g[l]), da_qk)

        kmem, vmem = _memkv(mem, row(norm_mem_g[l]), w_xk[l].astype(BF16), w_xv[l].astype(BF16))

        wr = jnp.concatenate([w_router_group[l], w_router_expert[l]], axis=1).astype(F32).T
        wr = jnp.pad(wr, ((0, ROUTER_ROWS - wr.shape[0]), (0, 0)))
        wr_hi = wr.astype(BF16)
        wr_lo = (wr - wr_hi.astype(F32)).astype(BF16)
        br = jnp.pad(jnp.concatenate([b_router_group[l], b_router_expert[l]]).astype(F32),
                     (0, ROUTER_ROWS - N_GROUPS - N_EXPERTS)).reshape(ROUTER_ROWS, 1)
        wo = w_out[l].astype(BF16)
        x2, route = _mix(x, yda, yml, wo[:da_width], wo[da_width:], row(norm_x_g[l]), w_xq[l].astype(BF16),
                         kmem, vmem, w_xo[l].astype(BF16), row(norm_ffn_g[l]), wr_hi, wr_lo, br)

        tt = min(MOE_TILE, seq)
        offs, tok, wt = _sort_assignments(route, tt)
        x = _moe(offs, tok, wt, x2.reshape(bsz * seq // tt, tt, d), row(norm_ffn_g[l]), row(norm_final_g),
                 w1[l].astype(BF16), w3[l].astype(BF16), w2[l].astype(BF16),
                 final_norm=(l + 1 == depth)).reshape(bsz, seq, d)
    return x
```

```python
import functools
import math

import jax
import jax.numpy as jnp
from jax import lax
from jax.experimental import pallas as pl
from jax.experimental.pallas import tpu as pltpu

F32 = jnp.float32
BF16 = jnp.bfloat16

LANE = 128
SUBLANE = 8
VMEM_LIMIT_BYTES = 56 * 1024 * 1024

EPS = 1e-6
ROPE_THETA = 10000.0
DA_HEADS = 4
DA_HEAD_DIM = 64
DA_VAL_DIM = 2 * DA_HEAD_DIM
ML_HEADS = 4
ML_CONV = 4
X_HEADS = 4
N_GROUPS = 4
EXPERTS_PER_GROUP = 8
N_EXPERTS = N_GROUPS * EXPERTS_PER_GROUP
ROUTER_ROWS = 40

ROW_TILE = 512
ML_CHUNK = 256
MOE_TILE = 4096
MOE_ROWS = 128

NEG_INF = float("-inf")
LOG2_E = math.log2(math.e)


def _rms(x, g):
    return x * lax.rsqrt(jnp.mean(x * x, axis=-1, keepdims=True) + EPS) * g


def _dot(a, b):
    return jnp.dot(a, b, preferred_element_type=F32)


def _dot_nt(a, b):
    return lax.dot_general(a, b, (((1,), (1,)), ((), ())), preferred_element_type=F32)


def _dot_tn(a, b):
    return lax.dot_general(a, b, (((0,), (0,)), ((), ())), preferred_element_type=F32)


def _split_bf16(x):
    hi = x.astype(BF16)
    lo = (x - hi.astype(F32)).astype(BF16)
    return hi, lo


def _params(*semantics):
    return pltpu.CompilerParams(dimension_semantics=semantics, vmem_limit_bytes=VMEM_LIMIT_BYTES)


def _inproj_body(x_ref, g_ref, w_ref, cos_ref, sin_ref, qt_ref, vt_ref, proj_ref, gcol_ref, grow_ref):
    hb = _rms(x_ref[0], g_ref[...]).astype(BF16)
    cos = cos_ref[...]
    sin = sin_ref[...]
    lane = lax.broadcasted_iota(jnp.int32, cos.shape, 1)
    first_half = (lane % DA_HEAD_DIM) < (DA_HEAD_DIM // 2)
    n_main = w_ref.shape[1] - LANE
    wide = 4 * LANE
    for c in range(n_main // LANE):
        if c % (wide // LANE) == 0:
            acc = _dot(hb, w_ref[:, c * LANE:c * LANE + wide])
        t = acc[:, (c * LANE) % wide:(c * LANE) % wide + LANE]
        if c < 2 * DA_HEADS:
            rot = jnp.where(first_half, pltpu.roll(t, LANE - DA_HEAD_DIM // 2, 1),
                            pltpu.roll(t, DA_HEAD_DIM // 2, 1))
            t = t * cos + rot * sin
        if c < DA_HEADS:
            qt_ref[0, 0, c * LANE:(c + 1) * LANE, :] = (t * (DA_HEAD_DIM ** -0.5 * LOG2_E)).T.astype(BF16)
        elif c < 2 * DA_HEADS:
            proj_ref[0, :, (c - DA_HEADS) * LANE:(c - DA_HEADS + 1) * LANE] = t.astype(BF16)
        elif c < 3 * DA_HEADS:
            vt_ref[0, 0, (c - 2 * DA_HEADS) * LANE:(c - 2 * DA_HEADS + 1) * LANE, :] = t.T.astype(BF16)
        else:
            proj_ref[0, :, (c - 2 * DA_HEADS) * LANE:(c - 2 * DA_HEADS + 1) * LANE] = t.astype(BF16)
    gates = _dot(hb, w_ref[:, n_main:n_main + LANE])
    gcol_ref[0] = gates
    grow_ref[0] = gates.T[:SUBLANE, :]


def _inproj(x, g, w_pad, cos_t, sin_t):
    bsz, seq, d = x.shape
    tm = min(ROW_TILE, seq)
    nt = seq // tm
    da_w = DA_HEADS * DA_VAL_DIM
    n_proj = w_pad.shape[1] - LANE - 2 * da_w
    return pl.pallas_call(
        _inproj_body,
        grid=(bsz, nt),
        in_specs=[
            pl.BlockSpec((1, tm, d), lambda b, i: (b, i, 0)),
            pl.BlockSpec((1, d), lambda b, i: (0, 0)),
            pl.BlockSpec(w_pad.shape, lambda b, i: (0, 0)),
            pl.BlockSpec((tm, LANE), lambda b, i: (i, 0)),
            pl.BlockSpec((tm, LANE), lambda b, i: (i, 0)),
        ],
        out_specs=[
            pl.BlockSpec((1, 1, da_w, tm), lambda b, i: (b, i, 0, 0)),
            pl.BlockSpec((1, 1, da_w, tm), lambda b, i: (b, i, 0, 0)),
            pl.BlockSpec((1, tm, n_proj), lambda b, i: (b, i, 0)),
            pl.BlockSpec((1, tm, LANE), lambda b, i: (b, i, 0)),
            pl.BlockSpec((1, SUBLANE, tm), lambda b, i: (b, 0, i)),
        ],
        out_shape=[
            jax.ShapeDtypeStruct((bsz, nt, da_w, tm), BF16),
            jax.ShapeDtypeStruct((bsz, nt, da_w, tm), BF16),
            jax.ShapeDtypeStruct((bsz, seq, n_proj), BF16),
            jax.ShapeDtypeStruct((bsz, seq, LANE), F32),
            jax.ShapeDtypeStruct((bsz, SUBLANE, seq), F32),
        ],
        compiler_params=_params("parallel", "parallel"),
        name="inproj",
    )(x, g, w_pad, cos_t, sin_t)


def _diffattn_body(qt_ref, k_ref, vt_ref, lam_ref, g_ref, o_ref, m_scr, acc_scr, p_scr, *, lam_init):
    qi = pl.program_id(2)
    tq = qt_ref.shape[3]
    qt = qt_ref[0, 0]
    row = lax.broadcasted_iota(jnp.int32, qt.shape, 0)
    zero = jnp.zeros_like(qt)
    q2t = jnp.concatenate([jnp.where(row < DA_HEAD_DIM, qt, zero), jnp.where(row >= DA_HEAD_DIM, qt, zero)], axis=1)
    pad_rows = acc_scr.shape[0] - DA_VAL_DIM
    ones_rows = jnp.where(lax.broadcasted_iota(jnp.int32, (pad_rows, tq), 0) == 0, 1.0, 0.0).astype(BF16)

    m_scr[...] = jnp.full(m_scr.shape, NEG_INF, F32)
    acc_scr[...] = jnp.zeros(acc_scr.shape, F32)

    n_chains = 4
    width = 2 * tq // n_chains
    halves = tuple(slice(c * width, (c + 1) * width) for c in range(n_chains))

    def scores(j, masked, prev_values):
        k = k_ref[0, pl.ds(pl.multiple_of(j * tq, tq), tq), :]
        out = []
        for cols in halves:
            s = _dot(k, q2t[:, cols])
            pv = values(j - 1, cols) if prev_values else None
            if masked:
                key = lax.broadcasted_iota(jnp.int32, s.shape, 0)
                qry = lax.broadcasted_iota(jnp.int32, s.shape, 1) + cols.start % tq
                s = jnp.where(key <= qry, s, NEG_INF)
            m_prev = m_scr[:, cols]
            m_new = jnp.maximum(m_prev, jnp.max(s, axis=0, keepdims=True))
            p_scr[:, cols] = jnp.exp2(s - m_new).astype(BF16)
            m_scr[:, cols] = m_new
            out.append((jnp.exp2(m_prev - m_new), pv))
        return out

    def values(j, cols):
        v_aug = jnp.concatenate([vt_ref[0, j], ones_rows], axis=0)
        return _dot(v_aug, p_scr[:, cols])

    def step(j, masked):
        for cols, (alpha, pv) in zip(halves, scores(j, masked, True)):
            acc_scr[:, cols] = alpha * (acc_scr[:, cols] + pv)

    def full_step_pair(i, carry):
        step(1 + 2 * i, False)
        step(2 + 2 * i, False)
        return carry

    @pl.when(qi == 0)
    def _():
        scores(0, True, False)

    @pl.when(qi > 0)
    def _():
        scores(0, False, False)
        lax.fori_loop(0, (qi - 1) // 2, full_step_pair, 0)

        @pl.when((qi - 1) % 2 == 1)
        def _():
            step(qi - 1, False)

        step(qi, True)

    for cols in halves:
        acc_scr[:, cols] = acc_scr[:, cols] + values(qi, cols)

    lv = lam_ref[...]
    lam = (jnp.exp(jnp.sum(lv[0:1, :] * lv[1:2, :], axis=-1, keepdims=True))
           - jnp.exp(jnp.sum(lv[2:3, :] * lv[3:4, :], axis=-1, keepdims=True)) + lam_init)
    acc = acc_scr[...]
    o = acc[0:DA_VAL_DIM, :] / acc[DA_VAL_DIM:DA_VAL_DIM + 1, :]
    o = o[:, :tq] - lam * o[:, tq:]
    y = o * lax.rsqrt(jnp.mean(o * o, axis=0, keepdims=True) + EPS) * g_ref[...] * (1.0 - lam_init)
    o_ref[0] = y.T.astype(o_ref.dtype)


def _diffattn(qt, proj, vt, da_lambda, subln_g_col, lam_init):
    bsz, nt, _, tq = qt.shape
    seq = nt * tq
    body = functools.partial(_diffattn_body, lam_init=lam_init)
    return pl.pallas_call(
        body,
        grid=(bsz, DA_HEADS, nt),
        in_specs=[
            pl.BlockSpec((1, 1, DA_VAL_DIM, tq), lambda b, h, i: (b, i, h, 0)),
            pl.BlockSpec((1, seq, LANE), lambda b, h, i: (b, 0, h)),
            pl.BlockSpec((1, nt, DA_VAL_DIM, tq), lambda b, h, i: (b, 0, h, 0)),
            pl.BlockSpec(da_lambda.shape, lambda b, h, i: (0, 0)),
            pl.BlockSpec((DA_VAL_DIM, 1), lambda b, h, i: (0, 0)),
        ],
        out_specs=pl.BlockSpec((1, tq, LANE), lambda b, h, i: (b, i, h)),
        out_shape=jax.ShapeDtypeStruct((bsz, seq, DA_HEADS * DA_VAL_DIM), BF16),
        scratch_shapes=[
            pltpu.VMEM((1, 2 * tq), F32),
            pltpu.VMEM((DA_VAL_DIM + 2 * SUBLANE, 2 * tq), F32),
            pltpu.VMEM((tq, 2 * tq), BF16),
        ],
        compiler_params=_params("parallel", "parallel", "arbitrary"),
        name="diffattn",
    )(qt, proj, vt, da_lambda, subln_g_col)


def _log_sigmoid(x):
    return jnp.minimum(x, 0.0) - jnp.log(1.0 + jnp.exp(-jnp.abs(x)))


def _sigmoid(x):
    return 1.0 / (1.0 + jnp.exp(-x))


def _mlstm_body(q_ref, k_ref, v_ref, o_ref, gcol_ref, grow_ref, cw_ref, cb_ref, gbcol_ref, gbrow_ref, ng_ref,
                y_ref, xbuf, c_scr, m_scr):
    chunk = pl.program_id(1)
    L = q_ref.shape[1]
    width = q_ref.shape[2]
    dh = width // ML_HEADS
    tail = SUBLANE

    @pl.when(chunk == 0)
    def _():
        xbuf[0:tail, :] = jnp.zeros((tail, 2 * width), F32)
        c_scr[...] = jnp.zeros(c_scr.shape, F32)
        m_scr[...] = jnp.zeros(m_scr.shape, F32)

    xbuf[tail:tail + L, 0:width] = q_ref[0].astype(F32)
    xbuf[tail:tail + L, width:2 * width] = k_ref[0].astype(F32)
    conv = cb_ref[...]
    for j in range(ML_CONV):
        conv = conv + cw_ref[j:j + 1, :] * xbuf[pl.ds(tail - (ML_CONV - 1) + j, L), :]
    xbuf[0:tail, :] = xbuf[L:L + tail, :]
    qk = conv * _sigmoid(conv)

    gc = gcol_ref[0] + gbcol_ref[...]
    gr = grow_ref[0] + gbrow_ref[...]
    row_id = lax.broadcasted_iota(jnp.int32, (L, L), 0)
    col_id = lax.broadcasted_iota(jnp.int32, (L, L), 1)
    causal = col_id <= row_id
    tri = jnp.where(causal, 1.0, 0.0).astype(BF16)
    tri_t = jnp.where(row_id <= col_id, 1.0, 0.0).astype(BF16)
    lfc_hi, lfc_lo = _split_bf16(_log_sigmoid(gc))
    lfr_hi, lfr_lo = _split_bf16(_log_sigmoid(gr))
    a_c = _dot(tri, lfc_hi) + _dot(tri, lfc_lo)
    a_r = _dot(lfr_hi, tri_t) + _dot(lfr_lo, tri_t)

    lane = lax.broadcasted_iota(jnp.int32, (L, dh), 1)
    ones_col = jnp.where(lane == 0, 1.0, 0.0).astype(BF16)

    for h in range(ML_HEADS):
        fh = ML_HEADS + h
        m_prev = m_scr[h:h + 1, 0:1]
        a_col = a_c[:, fh:fh + 1]
        ig_col = gc[:, h:h + 1]
        a_row = a_r[fh:fh + 1, :]
        ig_row = gr[h:h + 1, :]
        log_d = jnp.where(causal, a_col - (a_row - ig_row), NEG_INF)
        log_inter = a_col + m_prev
        m_row = jnp.maximum(log_inter, jnp.max(log_d, axis=-1, keepdims=True))
        w_intra = jnp.exp(log_d - m_row)
        w_inter = jnp.exp(log_inter - m_row)

        qh = qk[:, h * dh:(h + 1) * dh].astype(BF16)
        kh = (qk[:, width + h * dh:width + (h + 1) * dh] * (dh ** -0.5)).astype(BF16)
        v_aug = jnp.concatenate([v_ref[0, :, h * dh:(h + 1) * dh], ones_col], axis=1)
        s = _dot_nt(qh, kh) * w_intra
        c_mat = c_scr[h]
        num = w_inter * _dot(qh, c_mat.astype(BF16)) + _dot(s.astype(BF16), v_aug)
        den = num[:, dh:dh + 1]
        hid = num[:, 0:dh] / jnp.maximum(jnp.abs(den), jnp.exp(-m_row))

        g_last = a_col[L - 1:L, :]
        log_w = g_last - a_col + ig_col
        m_new = jnp.maximum(g_last + m_prev, jnp.max(log_w, axis=0, keepdims=True))
        w_state = jnp.exp(log_w - m_new)
        decay = jnp.exp(g_last + m_prev - m_new)
        wv = (w_state * v_aug.astype(F32)).astype(BF16)
        c_scr[h] = decay * c_mat + _dot_tn(kh, wv)
        m_scr[h:h + 1, :] = jnp.broadcast_to(m_new, (1, m_scr.shape[1]))

        mu = jnp.mean(hid, axis=-1, keepdims=True)
        cen = hid - mu
        var = jnp.mean(cen * cen, axis=-1, keepdims=True)
        hn = cen * lax.rsqrt(var + EPS) * ng_ref[:, h * dh:(h + 1) * dh]
        gate = _sigmoid(o_ref[0, :, h * dh:(h + 1) * dh].astype(F32))
        y_ref[0, :, h * dh:(h + 1) * dh] = (gate * hn).astype(y_ref.dtype)


def _mlstm(proj, gcol, grow, conv_w, conv_b, gb_col, gb_row, norm_g, col0):
    bsz, seq, _ = proj.shape
    width = norm_g.shape[-1]
    dh = width // ML_HEADS
    L = min(ML_CHUNK, seq)
    blk0 = col0 // width
    spec = lambda k: pl.BlockSpec((1, L, width), lambda b, c: (b, c, blk0 + k))
    return pl.pallas_call(
        _mlstm_body,
        grid=(bsz, seq // L),
        in_specs=[
            spec(0), spec(1), spec(2), spec(3),
            pl.BlockSpec((1, L, LANE), lambda b, c: (b, c, 0)),
            pl.BlockSpec((1, SUBLANE, L), lambda b, c: (b, 0, c)),
            pl.BlockSpec(conv_w.shape, lambda b, c: (0, 0)),
            pl.BlockSpec(conv_b.shape, lambda b, c: (0, 0)),
            pl.BlockSpec(gb_col.shape, lambda b, c: (0, 0)),
            pl.BlockSpec(gb_row.shape, lambda b, c: (0, 0)),
            pl.BlockSpec(norm_g.shape, lambda b, c: (0, 0)),
        ],
        out_specs=pl.BlockSpec((1, L, width), lambda b, c: (b, c, 0)),
        out_shape=jax.ShapeDtypeStruct((bsz, seq, width), BF16),
        scratch_shapes=[
            pltpu.VMEM((L + 2 * SUBLANE, 2 * width), F32),
            pltpu.VMEM((ML_HEADS, dh, 2 * dh), F32),
            pltpu.VMEM((SUBLANE, LANE), F32),
        ],
        compiler_params=_params("parallel", "arbitrary"),
        name="mlstm",
    )(proj, proj, proj, proj, gcol, grow, conv_w, conv_b, gb_col, gb_row, norm_g)


def _memkv_body(mem_ref, g_ref, wk_ref, wv_ref, k_ref, v_ref):
    mb = _rms(mem_ref[0], g_ref[...]).astype(BF16)
    k_ref[0] = _dot(mb, wk_ref[...]).astype(BF16)
    v_ref[0] = _dot(mb, wv_ref[...]).astype(BF16)


def _memkv(mem, g, wk, wv):
    bsz, mlen, d = mem.shape
    full = lambda a: pl.BlockSpec(a.shape, lambda b: (0,) * a.ndim)
    blk = pl.BlockSpec((1, mlen, d), lambda b: (b, 0, 0))
    return pl.pallas_call(
        _memkv_body,
        grid=(bsz,),
        in_specs=[blk, full(g), full(wk), full(wv)],
        out_specs=[blk, blk],
        out_shape=[jax.ShapeDtypeStruct((bsz, mlen, d), BF16)] * 2,
        compiler_params=_params("parallel"),
        name="memkv",
    )(mem, g, wk, wv)


def _mix_body(x_ref, yda_ref, yml_ref, wo1_ref, wo2_ref, gx_ref, wq_ref, km_ref, vm_ref, wxo_ref, gf_ref,
              wrh_ref, wrl_ref, br_ref, x2_ref, route_ref):
    x1 = x_ref[0] + _dot(yda_ref[0], wo1_ref[...]) + _dot(yml_ref[0], wo2_ref[...])

    d = x1.shape[-1]
    dh = d // X_HEADS
    q = _dot(_rms(x1, gx_ref[...]).astype(BF16), wq_ref[...]).astype(BF16)
    heads = []
    for h in range(X_HEADS):
        sl = slice(h * dh, (h + 1) * dh)
        s = _dot_nt(q[:, sl], km_ref[0, :, sl]) * (dh ** -0.5)
        p = jnp.exp(s - jnp.max(s, axis=-1, keepdims=True))
        o = _dot(p.astype(BF16), vm_ref[0, :, sl]) / jnp.sum(p, axis=-1, keepdims=True)
        heads.append(o.astype(BF16))
    x2 = x1 + _dot(jnp.concatenate(heads, axis=1), wxo_ref[...])
    _to_token_tiles(x2_ref.at[0], 0, x2)

    h3 = _rms(x2, gf_ref[...])
    h_hi, h_lo = _split_bf16(h3)
    logits = (_dot_nt(wrh_ref[...], h_hi) + _dot_nt(wrh_ref[...], h_lo) + _dot_nt(wrl_ref[...], h_hi)
              + br_ref[...])
    tm = logits.shape[1]
    gl = logits[0:N_GROUPS, :]
    g_iota = lax.broadcasted_iota(jnp.int32, gl.shape, 0)
    g_max = jnp.max(gl, axis=0, keepdims=True)
    g_idx = jnp.min(jnp.where(gl == g_max, g_iota, N_GROUPS), axis=0, keepdims=True)
    g_w = 1.0 / jnp.sum(jnp.exp(gl - g_max), axis=0, keepdims=True)
    el = jnp.zeros((EXPERTS_PER_GROUP, tm), F32)
    for g in range(N_GROUPS):
        lo = N_GROUPS + g * EXPERTS_PER_GROUP
        el = jnp.where(g_idx == g, logits[lo:lo + EXPERTS_PER_GROUP, :], el)
    e_iota = lax.broadcasted_iota(jnp.int32, el.shape, 0)
    e1 = jnp.max(el, axis=0, keepdims=True)
    i1 = jnp.min(jnp.where(el == e1, e_iota, EXPERTS_PER_GROUP), axis=0, keepdims=True)
    el2 = jnp.where(e_iota == i1, NEG_INF, el)
    e2 = jnp.max(el2, axis=0, keepdims=True)
    i2 = jnp.min(jnp.where(el2 == e2, e_iota, EXPERTS_PER_GROUP), axis=0, keepdims=True)
    p2 = jnp.exp(e2 - e1)
    w1 = g_w / (1.0 + p2)
    w2 = g_w * p2 / (1.0 + p2)
    base = g_idx * EXPERTS_PER_GROUP
    zero = jnp.zeros((1, tm), F32)
    route_ref[0] = jnp.concatenate(
        [(base + i1).astype(F32), (base + i2).astype(F32), w1, w2, zero, zero, zero, zero], axis=0)


def _mix(x, yda, yml, wo1, wo2, gx, wq, kmem, vmem, wxo, gf, wr_hi, wr_lo, br):
    bsz, seq, d = x.shape
    tm = min(ROW_TILE, seq)
    mlen = kmem.shape[1]
    full = lambda a: pl.BlockSpec(a.shape, lambda b, i: (0,) * a.ndim)
    tok = lambda w: pl.BlockSpec((1, tm, w), lambda b, i: (b, i, 0))
    memspec = pl.BlockSpec((1, mlen, d), lambda b, i: (b, 0, 0))
    return pl.pallas_call(
        _mix_body,
        grid=(bsz, seq // tm),
        in_specs=[tok(d), tok(yda.shape[-1]), tok(yml.shape[-1]), full(wo1), full(wo2), full(gx), full(wq),
                  memspec, memspec, full(wxo), full(gf), full(wr_hi), full(wr_lo), full(br)],
        out_specs=[pl.BlockSpec((1, tm * SUBLANE, LANE), lambda b, i: (b, i, 0)),
                   pl.BlockSpec((1, SUBLANE, tm), lambda b, i: (b, 0, i))],
        out_shape=[
            jax.ShapeDtypeStruct((bsz, seq * d // LANE, LANE), F32),
            jax.ShapeDtypeStruct((bsz, SUBLANE, seq), F32),
        ],
        compiler_params=_params("parallel", "parallel"),
        name="mix_xattn_router",
    )(x, yda, yml, wo1, wo2, gx, wq, kmem, vmem, wxo, gf, wr_hi, wr_lo, br)


def _to_token_tiles(ref, first_row, dense):
    n, d = dense.shape
    for c in range(d // LANE):
        ref[pl.ds(first_row * SUBLANE + c, n, stride=SUBLANE), :] = dense[:, c * LANE:(c + 1) * LANE]


def _from_token_tiles(ref, first_row, n):
    return jnp.concatenate(
        [ref[pl.ds(first_row * SUBLANE + c, n, stride=SUBLANE), :] for c in range(SUBLANE)], axis=1)


def _moe_body(start_ref, cnt_ref, x_hbm, tok_hbm, wt_hbm, gf_ref, gfin_ref, w1_ref, w3_ref, w2_ref, out_hbm,
              acc, h3, xg, yt, stage, tok_s, wt_s, sems, *, final_norm):
    tile = pl.program_id(0)
    e = pl.program_id(1)
    tt = acc.shape[0] // SUBLANE
    chunk_rows = stage.shape[0]
    group = SUBLANE

    def token_tile(ref, r):
        return ref.at[pl.ds(pl.multiple_of(r * SUBLANE, SUBLANE), SUBLANE), :]

    @pl.when(jnp.logical_and(tile == 0, e == 0))
    def _():
        xg[...] = jnp.zeros(xg.shape, F32)

    @pl.when(e == 0)
    def _():
        loads = [pltpu.make_async_copy(x_hbm.at[tile], acc, sems.at[0]),
                 pltpu.make_async_copy(tok_hbm.at[tile], tok_s, sems.at[1]),
                 pltpu.make_async_copy(wt_hbm.at[tile], wt_s, sems.at[2])]
        for cp in loads:
            cp.start()
        for cp in loads:
            cp.wait()

        def ffn_norm(i, c):
            r0 = i * chunk_rows
            _to_token_tiles(h3, r0, _rms(_from_token_tiles(acc, r0, chunk_rows), gf_ref[...]))
            return c
        lax.fori_loop(0, tt // chunk_rows, ffn_norm, 0)

    start = start_ref[tile * N_EXPERTS + e]
    count = cnt_ref[tile * N_EXPERTS + e]

    def run_block(first, cnt, rows):
        idx0 = start + first
        n_groups = cnt // group

        def gather_group(i, c):
            for u in range(group):
                r = i * group + u
                token_tile(xg, r)[...] = token_tile(h3, tok_s[idx0 + r])[...]
            return c

        def gather_row(r, c):
            token_tile(xg, r)[...] = token_tile(h3, tok_s[idx0 + r])[...]
            return c

        lax.fori_loop(0, n_groups, gather_group, 0)
        lax.fori_loop(n_groups * group, cnt, gather_row, 0)

        xb = _from_token_tiles(xg, 0, rows).astype(BF16)
        a = _dot(xb, w1_ref[0])
        b = _dot(xb, w3_ref[0])
        hmid = (a * _sigmoid(a) * b).astype(BF16)
        _to_token_tiles(yt, 0, _dot(hmid, w2_ref[0]))

        def scatter_group(i, c):
            toks = [tok_s[idx0 + i * group + u] for u in range(group)]
            new = [token_tile(acc, toks[u])[...] + wt_s[idx0 + i * group + u] * token_tile(yt, i * group + u)[...]
                   for u in range(group)]
            for u in range(group):
                token_tile(acc, toks[u])[...] = new[u]
            return c

        def scatter_row(r, c):
            t = tok_s[idx0 + r]
            token_tile(acc, t)[...] = token_tile(acc, t)[...] + wt_s[idx0 + r] * token_tile(yt, r)[...]
            return c

        lax.fori_loop(0, n_groups, scatter_group, 0)
        lax.fori_loop(n_groups * group, cnt, scatter_row, 0)

    big = xg.shape[0] // SUBLANE
    small = big // 2
    n_small = (count + small - 1) // small
    n_big = n_small // 2

    def big_block(i, c):
        run_block(i * big, jnp.minimum(big, count - i * big), big)
        return c

    lax.fori_loop(0, n_big, big_block, 0)

    @pl.when(n_small % 2 == 1)
    def _():
        run_block(n_big * big, count - n_big * big, small)

    @pl.when(e == pl.num_programs(1) - 1)
    def _():
        def write_back(i, c):
            r0 = pl.multiple_of(i * chunk_rows, chunk_rows)
            rows = _from_token_tiles(acc, r0, chunk_rows)
            stage[...] = _rms(rows, gfin_ref[...]) if final_norm else rows
            store = pltpu.make_async_copy(stage, out_hbm.at[tile, pl.ds(r0, chunk_rows), :], sems.at[3])
            store.start()
            store.wait()
            return c
        lax.fori_loop(0, tt // chunk_rows, write_back, 0)


def _moe(start, cnt, tok, wt, x_token_tiles, gf, gfin, w1, w3, w2, final_norm):
    ntiles, tile_rows, _ = x_token_tiles.shape
    tt = tile_rows // SUBLANE
    d = SUBLANE * LANE
    de = w1.shape[-1]
    seg = tok.shape[1]
    body = functools.partial(_moe_body, final_norm=final_norm)
    hbm = pl.BlockSpec(memory_space=pl.ANY)
    grid_spec = pltpu.PrefetchScalarGridSpec(
        num_scalar_prefetch=2,
        grid=(ntiles, N_EXPERTS),
        in_specs=[
            hbm, hbm, hbm,
            pl.BlockSpec((1, d), lambda t, e, *_: (0, 0)),
            pl.BlockSpec((1, d), lambda t, e, *_: (0, 0)),
            pl.BlockSpec((1, d, de), lambda t, e, *_: (e, 0, 0)),
            pl.BlockSpec((1, d, de), lambda t, e, *_: (e, 0, 0)),
            pl.BlockSpec((1, de, d), lambda t, e, *_: (e, 0, 0)),
        ],
        out_specs=hbm,
        scratch_shapes=[
            pltpu.VMEM((tile_rows, LANE), F32),
            pltpu.VMEM((tile_rows, LANE), F32),
            pltpu.VMEM((2 * MOE_ROWS * SUBLANE, LANE), F32),
            pltpu.VMEM((2 * MOE_ROWS * SUBLANE, LANE), F32),
            pltpu.VMEM((2 * MOE_ROWS, d), F32),
            pltpu.SMEM((seg,), jnp.int32),
            pltpu.SMEM((seg,), F32),
            pltpu.SemaphoreType.DMA((4,)),
        ],
    )
    return pl.pallas_call(
        body,
        grid_spec=grid_spec,
        out_shape=jax.ShapeDtypeStruct((ntiles, tt, d), F32),
        compiler_params=_params("arbitrary", "arbitrary"),
        name="moe",
    )(start, cnt, x_token_tiles, tok, wt, gf, gfin, w1, w3, w2)


def _rope_tables(seq):
    half = DA_HEAD_DIM // 2
    inv_freq = ROPE_THETA ** (-jnp.arange(0, DA_HEAD_DIM, 2, dtype=F32) / DA_HEAD_DIM)
    ang = jnp.arange(seq, dtype=F32)[:, None] * inv_freq[None, :]
    cos, sin = jnp.cos(ang), jnp.sin(ang)
    reps = LANE // half
    cos_t = jnp.tile(cos, (1, reps))
    sin_t = jnp.tile(jnp.concatenate([-sin, sin], axis=1), (1, reps // 2))
    return cos_t, sin_t


def _sort_assignments(route, tt):
    bsz, _, seq = route.shape
    per = seq // tt
    ntiles = bsz * per
    tile = lambda a: a.reshape(bsz, TOP_K_ROWS, per, tt).transpose(0, 2, 1, 3).reshape(ntiles, TOP_K_ROWS * tt)
    keys = tile(route[:, 0:TOP_K_ROWS, :]).astype(jnp.int32)
    wts = tile(route[:, TOP_K_ROWS:2 * TOP_K_ROWS, :])
    order = jnp.argsort(keys, axis=-1).astype(jnp.int32)
    tok = order % tt
    wt = jnp.take_along_axis(wts, order, axis=-1)
    counts = jnp.sum(keys[:, :, None] == jnp.arange(N_EXPERTS, dtype=jnp.int32)[None, None, :], axis=1,
                     dtype=jnp.int32)
    start = jnp.cumsum(counts, axis=-1, dtype=jnp.int32) - counts
    return start.reshape(-1), counts.reshape(-1), tok, wt


TOP_K_ROWS = 2


def kernel(x, mem, norm_mix_g, w_in, w_out, da_lambda, da_subln_g, ml_conv_w, ml_conv_b, ml_gate_b, ml_norm_g, norm_x_g, norm_mem_g, w_xq, w_xk, w_xv, w_xo, norm_ffn_g, w_router_group, b_router_group, w_router_expert, b_router_expert, w1, w3, w2, norm_final_g):
    depth = w_in.shape[0]
    bsz, seq, d = x.shape
    da_qk = DA_HEADS * 2 * DA_HEAD_DIM
    da_width = DA_HEADS * DA_VAL_DIM
    ml_width = ml_norm_g.shape[-1]
    n_main = 2 * da_qk + da_width + 4 * ml_width
    n_in = w_in.shape[-1]
    cos_t, sin_t = _rope_tables(seq)
    row = lambda v: v.reshape(1, -1).astype(F32)

    for l in range(depth):
        lam_init = 0.8 - 0.6 * math.exp(-0.3 * l)
        w_pad = jnp.pad(w_in[l], ((0, 0), (0, n_main + LANE - n_in))).astype(BF16)
        qt, vt, proj, gcol, grow = _inproj(x, row(norm_mix_g[l]), w_pad, cos_t, sin_t)

        yda = _diffattn(qt, proj, vt, da_lambda[l].astype(F32), da_subln_g[l].astype(F32).reshape(-1, 1), lam_init)

        gb = ml_gate_b[l].astype(F32).reshape(1, 2 * ML_HEADS)
        gb_col = jnp.pad(gb, ((0, 0), (0, LANE - 2 * ML_HEADS)))
        gb_row = gb.reshape(2 * ML_HEADS, 1)
        yml = _mlstm(proj, gcol, grow, ml_conv_w[l].reshape(ML_CONV, 2 * ml_width).astype(F32),
                     row(ml_conv_b[l]), gb_col, gb_row, row(ml_norm_g[l]), da_qk)

        kmem, vmem = _memkv(mem, row(norm_mem_g[l]), w_xk[l].astype(BF16), w_xv[l].astype(BF16))

        wr = jnp.concatenate([w_router_group[l], w_router_expert[l]], axis=1).astype(F32).T
        wr = jnp.pad(wr, ((0, ROUTER_ROWS - wr.shape[0]), (0, 0)))
        wr_hi = wr.astype(BF16)
        wr_lo = (wr - wr_hi.astype(F32)).astype(BF16)
        br = jnp.pad(jnp.concatenate([b_router_group[l], b_router_expert[l]]).astype(F32),
                     (0, ROUTER_ROWS - N_GROUPS - N_EXPERTS)).reshape(ROUTER_ROWS, 1)
        wo = w_out[l].astype(BF16)
        x2, route = _mix(x, yda, yml, wo[:da_width], wo[da_width:], row(norm_x_g[l]), w_xq[l].astype(BF16),
                         kmem, vmem, w_xo[l].astype(BF16), row(norm_ffn_g[l]), wr_hi, wr_lo, br)

        tt = min(MOE_TILE, seq)
        x = _moe(*_sort_assignments(route, tt), x2.reshape(bsz * seq // tt, tt * d // LANE, LANE),
                 row(norm_ffn_g[l]), row(norm_final_g),
                 w1[l].astype(BF16), w3[l].astype(BF16), w2[l].astype(BF16),
                 final_norm=(l + 1 == depth)).reshape(bsz, seq, d)
    return x
```

```python
import functools
import math

import jax
import jax.numpy as jnp
from jax import lax
from jax.experimental import pallas as pl
from jax.experimental.pallas import tpu as pltpu

F32 = jnp.float32
BF16 = jnp.bfloat16

LANE = 128
SUBLANE = 8
VMEM_LIMIT_BYTES = 56 * 1024 * 1024

EPS = 1e-6
ROPE_THETA = 10000.0
DA_HEADS = 4
DA_HEAD_DIM = 64
DA_VAL_DIM = 2 * DA_HEAD_DIM
ML_HEADS = 4
ML_CONV = 4
X_HEADS = 4
N_GROUPS = 4
EXPERTS_PER_GROUP = 8
N_EXPERTS = N_GROUPS * EXPERTS_PER_GROUP
ROUTER_ROWS = 40

ROW_TILE = 512
ML_CHUNK = 256
MOE_TILE = 4096
MOE_ROWS = 128

NEG_INF = float("-inf")
LOG2_E = math.log2(math.e)


def _rms(x, g):
    return x * lax.rsqrt(jnp.mean(x * x, axis=-1, keepdims=True) + EPS) * g


def _dot(a, b):
    return jnp.dot(a, b, preferred_element_type=F32)


def _dot_nt(a, b):
    return lax.dot_general(a, b, (((1,), (1,)), ((), ())), preferred_element_type=F32)


def _dot_tn(a, b):
    return lax.dot_general(a, b, (((0,), (0,)), ((), ())), preferred_element_type=F32)


def _split_bf16(x):
    hi = x.astype(BF16)
    lo = (x - hi.astype(F32)).astype(BF16)
    return hi, lo


def _params(*semantics):
    return pltpu.CompilerParams(dimension_semantics=semantics, vmem_limit_bytes=VMEM_LIMIT_BYTES)


def _inproj_body(x_ref, g_ref, w_ref, cos_ref, sin_ref, qt_ref, vt_ref, proj_ref, gcol_ref, grow_ref):
    hb = _rms(x_ref[0], g_ref[...]).astype(BF16)
    cos = cos_ref[...]
    sin = sin_ref[...]
    lane = lax.broadcasted_iota(jnp.int32, cos.shape, 1)
    first_half = (lane % DA_HEAD_DIM) < (DA_HEAD_DIM // 2)
    n_main = w_ref.shape[1] - LANE
    wide = 4 * LANE
    for c in range(n_main // LANE):
        if c % (wide // LANE) == 0:
            acc = _dot(hb, w_ref[:, c * LANE:c * LANE + wide])
        t = acc[:, (c * LANE) % wide:(c * LANE) % wide + LANE]
        if c < 2 * DA_HEADS:
            rot = jnp.where(first_half, pltpu.roll(t, LANE - DA_HEAD_DIM // 2, 1),
                            pltpu.roll(t, DA_HEAD_DIM // 2, 1))
            t = t * cos + rot * sin
        if c < DA_HEADS:
            qt_ref[0, 0, c * LANE:(c + 1) * LANE, :] = (t * (DA_HEAD_DIM ** -0.5 * LOG2_E)).T.astype(BF16)
        elif c < 2 * DA_HEADS:
            proj_ref[0, :, (c - DA_HEADS) * LANE:(c - DA_HEADS + 1) * LANE] = t.astype(BF16)
        elif c < 3 * DA_HEADS:
            vt_ref[0, 0, (c - 2 * DA_HEADS) * LANE:(c - 2 * DA_HEADS + 1) * LANE, :] = t.T.astype(BF16)
        else:
            proj_ref[0, :, (c - 2 * DA_HEADS) * LANE:(c - 2 * DA_HEADS + 1) * LANE] = t.astype(BF16)
    gates = _dot(hb, w_ref[:, n_main:n_main + LANE])
    gcol_ref[0] = gates
    grow_ref[0] = gates.T[:SUBLANE, :]


def _inproj(x, g, w_pad, cos_t, sin_t):
    bsz, seq, d = x.shape
    tm = min(ROW_TILE, seq)
    nt = seq // tm
    da_w = DA_HEADS * DA_VAL_DIM
    n_proj = w_pad.shape[1] - LANE - 2 * da_w
    return pl.pallas_call(
        _inproj_body,
        grid=(bsz, nt),
        in_specs=[
            pl.BlockSpec((1, tm, d), lambda b, i: (b, i, 0)),
            pl.BlockSpec((1, d), lambda b, i: (0, 0)),
            pl.BlockSpec(w_pad.shape, lambda b, i: (0, 0)),
            pl.BlockSpec((tm, LANE), lambda b, i: (i, 0)),
            pl.BlockSpec((tm, LANE), lambda b, i: (i, 0)),
        ],
        out_specs=[
            pl.BlockSpec((1, 1, da_w, tm), lambda b, i: (b, i, 0, 0)),
            pl.BlockSpec((1, 1, da_w, tm), lambda b, i: (b, i, 0, 0)),
            pl.BlockSpec((1, tm, n_proj), lambda b, i: (b, i, 0)),
            pl.BlockSpec((1, tm, LANE), lambda b, i: (b, i, 0)),
            pl.BlockSpec((1, SUBLANE, tm), lambda b, i: (b, 0, i)),
        ],
        out_shape=[
            jax.ShapeDtypeStruct((bsz, nt, da_w, tm), BF16),
            jax.ShapeDtypeStruct((bsz, nt, da_w, tm), BF16),
            jax.ShapeDtypeStruct((bsz, seq, n_proj), BF16),
            jax.ShapeDtypeStruct((bsz, seq, LANE), F32),
            jax.ShapeDtypeStruct((bsz, SUBLANE, seq), F32),
        ],
        compiler_params=_params("parallel", "parallel"),
        name="inproj",
    )(x, g, w_pad, cos_t, sin_t)


def _diffattn_body(qt_ref, k_ref, vt_ref, lam_ref, g_ref, o_ref, m_scr, acc_scr, p_scr, *, lam_init):
    qi = pl.program_id(2)
    tq = qt_ref.shape[3]
    qt = qt_ref[0, 0]
    row = lax.broadcasted_iota(jnp.int32, qt.shape, 0)
    zero = jnp.zeros_like(qt)
    q2t = jnp.concatenate([jnp.where(row < DA_HEAD_DIM, qt, zero), jnp.where(row >= DA_HEAD_DIM, qt, zero)], axis=1)
    pad_rows = acc_scr.shape[0] - DA_VAL_DIM
    ones_rows = jnp.where(lax.broadcasted_iota(jnp.int32, (pad_rows, tq), 0) == 0, 1.0, 0.0).astype(BF16)

    m_scr[...] = jnp.full(m_scr.shape, NEG_INF, F32)
    acc_scr[...] = jnp.zeros(acc_scr.shape, F32)

    n_chains = 4
    width = 2 * tq // n_chains
    halves = tuple(slice(c * width, (c + 1) * width) for c in range(n_chains))

    def scores(j, masked, prev_values):
        k = k_ref[0, pl.ds(pl.multiple_of(j * tq, tq), tq), :]
        out = []
        for cols in halves:
            s = _dot(k, q2t[:, cols])
            pv = values(j - 1, cols) if prev_values else None
            if masked:
                key = lax.broadcasted_iota(jnp.int32, s.shape, 0)
                qry = lax.broadcasted_iota(jnp.int32, s.shape, 1) + cols.start % tq
                s = jnp.where(key <= qry, s, NEG_INF)
            m_prev = m_scr[:, cols]
            m_new = jnp.maximum(m_prev, jnp.max(s, axis=0, keepdims=True))
            p_scr[:, cols] = jnp.exp2(s - m_new).astype(BF16)
            m_scr[:, cols] = m_new
            out.append((jnp.exp2(m_prev - m_new), pv))
        return out

    def values(j, cols):
        v_aug = jnp.concatenate([vt_ref[0, j], ones_rows], axis=0)
        return _dot(v_aug, p_scr[:, cols])

    def step(j, masked):
        for cols, (alpha, pv) in zip(halves, scores(j, masked, True)):
            acc_scr[:, cols] = alpha * (acc_scr[:, cols] + pv)

    def full_step_pair(i, carry):
        step(1 + 2 * i, False)
        step(2 + 2 * i, False)
        return carry

    @pl.when(qi == 0)
    def _():
        scores(0, True, False)

    @pl.when(qi > 0)
    def _():
        scores(0, False, False)
        lax.fori_loop(0, (qi - 1) // 2, full_step_pair, 0)

        @pl.when((qi - 1) % 2 == 1)
        def _():
            step(qi - 1, False)

        step(qi, True)

    for cols in halves:
        acc_scr[:, cols] = acc_scr[:, cols] + values(qi, cols)

    lv = lam_ref[...]
    lam = (jnp.exp(jnp.sum(lv[0:1, :] * lv[1:2, :], axis=-1, keepdims=True))
           - jnp.exp(jnp.sum(lv[2:3, :] * lv[3:4, :], axis=-1, keepdims=True)) + lam_init)
    acc = acc_scr[...]
    o = acc[0:DA_VAL_DIM, :] / acc[DA_VAL_DIM:DA_VAL_DIM + 1, :]
    o = o[:, :tq] - lam * o[:, tq:]
    y = o * lax.rsqrt(jnp.mean(o * o, axis=0, keepdims=True) + EPS) * g_ref[...] * (1.0 - lam_init)
    o_ref[0] = y.T.astype(o_ref.dtype)


def _diffattn(qt, proj, vt, da_lambda, subln_g_col, lam_init):
    bsz, nt, _, tq = qt.shape
    seq = nt * tq
    body = functools.partial(_diffattn_body, lam_init=lam_init)
    return pl.pallas_call(
        body,
        grid=(bsz, DA_HEADS, nt),
        in_specs=[
            pl.BlockSpec((1, 1, DA_VAL_DIM, tq), lambda b, h, i: (b, i, h, 0)),
            pl.BlockSpec((1, seq, LANE), lambda b, h, i: (b, 0, h)),
            pl.BlockSpec((1, nt, DA_VAL_DIM, tq), lambda b, h, i: (b, 0, h, 0)),
            pl.BlockSpec(da_lambda.shape, lambda b, h, i: (0, 0)),
            pl.BlockSpec((DA_VAL_DIM, 1), lambda b, h, i: (0, 0)),
        ],
        out_specs=pl.BlockSpec((1, tq, LANE), lambda b, h, i: (b, i, h)),
        out_shape=jax.ShapeDtypeStruct((bsz, seq, DA_HEADS * DA_VAL_DIM), BF16),
        scratch_shapes=[
            pltpu.VMEM((1, 2 * tq), F32),
            pltpu.VMEM((DA_VAL_DIM + 2 * SUBLANE, 2 * tq), F32),
            pltpu.VMEM((tq, 2 * tq), BF16),
        ],
        compiler_params=_params("parallel", "parallel", "arbitrary"),
        name="diffattn",
    )(qt, proj, vt, da_lambda, subln_g_col)


def _log_sigmoid(x):
    return jnp.minimum(x, 0.0) - jnp.log(1.0 + jnp.exp(-jnp.abs(x)))


def _sigmoid(x):
    return 1.0 / (1.0 + jnp.exp(-x))


def _mlstm_body(q_ref, k_ref, v_ref, o_ref, gcol_ref, grow_ref, cw_ref, cb_ref, gbcol_ref, gbrow_ref, ng_ref,
                y_ref, xbuf, c_scr, m_scr):
    chunk = pl.program_id(1)
    L = q_ref.shape[1]
    width = q_ref.shape[2]
    dh = width // ML_HEADS
    tail = xbuf.shape[0]

    @pl.when(chunk == 0)
    def _():
        xbuf[...] = jnp.zeros(xbuf.shape, xbuf.dtype)
        c_scr[...] = jnp.zeros(c_scr.shape, F32)
        m_scr[...] = jnp.zeros(m_scr.shape, F32)

    u = jnp.concatenate([q_ref[0], k_ref[0]], axis=1)
    u_ext = jnp.concatenate([xbuf[...], u], axis=0)
    src = lax.broadcasted_iota(jnp.int32, (L, L + tail), 1) - lax.broadcasted_iota(jnp.int32, (L, L + tail), 0)
    conv = cb_ref[...] + cw_ref[ML_CONV - 1:ML_CONV, :] * u.astype(F32)
    for back in range(1, ML_CONV):
        shift = jnp.where(src == tail - back, 1.0, 0.0).astype(BF16)
        conv = conv + cw_ref[ML_CONV - 1 - back:ML_CONV - back, :] * _dot(shift, u_ext)
    xbuf[...] = u[L - tail:, :]
    qk = conv * _sigmoid(conv)

    gc = gcol_ref[0] + gbcol_ref[...]
    gr = grow_ref[0] + gbrow_ref[...]
    row_id = lax.broadcasted_iota(jnp.int32, (L, L), 0)
    col_id = lax.broadcasted_iota(jnp.int32, (L, L), 1)
    causal = col_id <= row_id
    tri = jnp.where(causal, 1.0, 0.0).astype(BF16)
    tri_t = jnp.where(row_id <= col_id, 1.0, 0.0).astype(BF16)
    lfc_hi, lfc_lo = _split_bf16(_log_sigmoid(gc))
    lfr_hi, lfr_lo = _split_bf16(_log_sigmoid(gr))
    a_c = _dot(tri, lfc_hi) + _dot(tri, lfc_lo)
    a_r = _dot(lfr_hi, tri_t) + _dot(lfr_lo, tri_t)

    lane = lax.broadcasted_iota(jnp.int32, (L, dh), 1)
    ones_col = jnp.where(lane == 0, 1.0, 0.0).astype(BF16)

    for h in range(ML_HEADS):
        fh = ML_HEADS + h
        m_prev = m_scr[h:h + 1, 0:1]
        a_col = a_c[:, fh:fh + 1]
        ig_col = gc[:, h:h + 1]
        a_row = a_r[fh:fh + 1, :]
        ig_row = gr[h:h + 1, :]
        log_d = jnp.where(causal, a_col - (a_row - ig_row), NEG_INF)
        log_inter = a_col + m_prev
        m_row = jnp.maximum(log_inter, jnp.max(log_d, axis=-1, keepdims=True))
        w_intra = jnp.exp(log_d - m_row)
        w_inter = jnp.exp(log_inter - m_row)

        qh = qk[:, h * dh:(h + 1) * dh].astype(BF16)
        kh = (qk[:, width + h * dh:width + (h + 1) * dh] * (dh ** -0.5)).astype(BF16)
        v_aug = jnp.concatenate([v_ref[0, :, h * dh:(h + 1) * dh], ones_col], axis=1)
        s = _dot_nt(qh, kh) * w_intra
        c_mat = c_scr[h]
        num = w_inter * _dot(qh, c_mat.astype(BF16)) + _dot(s.astype(BF16), v_aug)
        den = num[:, dh:dh + 1]
        hid = num[:, 0:dh] / jnp.maximum(jnp.abs(den), jnp.exp(-m_row))

        g_last = a_col[L - 1:L, :]
        log_w = g_last - a_col + ig_col
        m_new = jnp.maximum(g_last + m_prev, jnp.max(log_w, axis=0, keepdims=True))
        w_state = jnp.exp(log_w - m_new)
        decay = jnp.exp(g_last + m_prev - m_new)
        wv = (w_state * v_aug.astype(F32)).astype(BF16)
        c_scr[h] = decay * c_mat + _dot_tn(kh, wv)
        m_scr[h:h + 1, :] = jnp.broadcast_to(m_new, (1, m_scr.shape[1]))

        mu = jnp.mean(hid, axis=-1, keepdims=True)
        cen = hid - mu
        var = jnp.mean(cen * cen, axis=-1, keepdims=True)
        hn = cen * lax.rsqrt(var + EPS) * ng_ref[:, h * dh:(h + 1) * dh]
        gate = _sigmoid(o_ref[0, :, h * dh:(h + 1) * dh].astype(F32))
        y_ref[0, :, h * dh:(h + 1) * dh] = (gate * hn).astype(y_ref.dtype)


def _mlstm(proj, gcol, grow, conv_w, conv_b, gb_col, gb_row, norm_g, col0):
    bsz, seq, _ = proj.shape
    width = norm_g.shape[-1]
    dh = width // ML_HEADS
    L = min(ML_CHUNK, seq)
    blk0 = col0 // width
    spec = lambda k: pl.BlockSpec((1, L, width), lambda b, c: (b, c, blk0 + k))
    return pl.pallas_call(
        _mlstm_body,
        grid=(bsz, seq // L),
        in_specs=[
            spec(0), spec(1), spec(2), spec(3),
            pl.BlockSpec((1, L, LANE), lambda b, c: (b, c, 0)),
            pl.BlockSpec((1, SUBLANE, L), lambda b, c: (b, 0, c)),
            pl.BlockSpec(conv_w.shape, lambda b, c: (0, 0)),
            pl.BlockSpec(conv_b.shape, lambda b, c: (0, 0)),
            pl.BlockSpec(gb_col.shape, lambda b, c: (0, 0)),
            pl.BlockSpec(gb_row.shape, lambda b, c: (0, 0)),
            pl.BlockSpec(norm_g.shape, lambda b, c: (0, 0)),
        ],
        out_specs=pl.BlockSpec((1, L, width), lambda b, c: (b, c, 0)),
        out_shape=jax.ShapeDtypeStruct((bsz, seq, width), BF16),
        scratch_shapes=[
            pltpu.VMEM((2 * SUBLANE, 2 * width), BF16),
            pltpu.VMEM((ML_HEADS, dh, 2 * dh), F32),
            pltpu.VMEM((SUBLANE, LANE), F32),
        ],
        compiler_params=_params("parallel", "arbitrary"),
        name="mlstm",
    )(proj, proj, proj, proj, gcol, grow, conv_w, conv_b, gb_col, gb_row, norm_g)


def _memkv_body(mem_ref, g_ref, wk_ref, wv_ref, k_ref, v_ref):
    mb = _rms(mem_ref[0], g_ref[...]).astype(BF16)
    k_ref[0] = _dot(mb, wk_ref[...]).astype(BF16)
    v_ref[0] = _dot(mb, wv_ref[...]).astype(BF16)


def _memkv(mem, g, wk, wv):
    bsz, mlen, d = mem.shape
    full = lambda a: pl.BlockSpec(a.shape, lambda b: (0,) * a.ndim)
    blk = pl.BlockSpec((1, mlen, d), lambda b: (b, 0, 0))
    return pl.pallas_call(
        _memkv_body,
        grid=(bsz,),
        in_specs=[blk, full(g), full(wk), full(wv)],
        out_specs=[blk, blk],
        out_shape=[jax.ShapeDtypeStruct((bsz, mlen, d), BF16)] * 2,
        compiler_params=_params("parallel"),
        name="memkv",
    )(mem, g, wk, wv)


def _mix_body(x_ref, yda_ref, yml_ref, wo1_ref, wo2_ref, gx_ref, wq_ref, km_ref, vm_ref, wxo_ref, gf_ref,
              wrh_ref, wrl_ref, br_ref, x2_ref, route_ref):
    x1 = x_ref[0] + _dot(yda_ref[0], wo1_ref[...]) + _dot(yml_ref[0], wo2_ref[...])

    d = x1.shape[-1]
    dh = d // X_HEADS
    q = _dot(_rms(x1, gx_ref[...]).astype(BF16), wq_ref[...]).astype(BF16)
    heads = []
    for h in range(X_HEADS):
        sl = slice(h * dh, (h + 1) * dh)
        s = _dot_nt(q[:, sl], km_ref[0, :, sl]) * (dh ** -0.5)
        p = jnp.exp(s - jnp.max(s, axis=-1, keepdims=True))
        o = _dot(p.astype(BF16), vm_ref[0, :, sl]) / jnp.sum(p, axis=-1, keepdims=True)
        heads.append(o.astype(BF16))
    x2 = x1 + _dot(jnp.concatenate(heads, axis=1), wxo_ref[...])
    _to_token_tiles(x2_ref.at[0], 0, x2)

    h3 = _rms(x2, gf_ref[...])
    h_hi, h_lo = _split_bf16(h3)
    logits = (_dot_nt(wrh_ref[...], h_hi) + _dot_nt(wrh_ref[...], h_lo) + _dot_nt(wrl_ref[...], h_hi)
              + br_ref[...])
    tm = logits.shape[1]
    gl = logits[0:N_GROUPS, :]
    g_iota = lax.broadcasted_iota(jnp.int32, gl.shape, 0)
    g_max = jnp.max(gl, axis=0, keepdims=True)
    g_idx = jnp.min(jnp.where(gl == g_max, g_iota, N_GROUPS), axis=0, keepdims=True)
    g_w = 1.0 / jnp.sum(jnp.exp(gl - g_max), axis=0, keepdims=True)
    el = jnp.zeros((EXPERTS_PER_GROUP, tm), F32)
    for g in range(N_GROUPS):
        lo = N_GROUPS + g * EXPERTS_PER_GROUP
        el = jnp.where(g_idx == g, logits[lo:lo + EXPERTS_PER_GROUP, :], el)
    e_iota = lax.broadcasted_iota(jnp.int32, el.shape, 0)
    e1 = jnp.max(el, axis=0, keepdims=True)
    i1 = jnp.min(jnp.where(el == e1, e_iota, EXPERTS_PER_GROUP), axis=0, keepdims=True)
    el2 = jnp.where(e_iota == i1, NEG_INF, el)
    e2 = jnp.max(el2, axis=0, keepdims=True)
    i2 = jnp.min(jnp.where(el2 == e2, e_iota, EXPERTS_PER_GROUP), axis=0, keepdims=True)
    p2 = jnp.exp(e2 - e1)
    w1 = g_w / (1.0 + p2)
    w2 = g_w * p2 / (1.0 + p2)
    base = g_idx * EXPERTS_PER_GROUP
    zero = jnp.zeros((1, tm), F32)
    route_ref[0] = jnp.concatenate(
        [(base + i1).astype(F32), (base + i2).astype(F32), w1, w2, zero, zero, zero, zero], axis=0)


def _mix(x, yda, yml, wo1, wo2, gx, wq, kmem, vmem, wxo, gf, wr_hi, wr_lo, br):
    bsz, seq, d = x.shape
    tm = min(ROW_TILE, seq)
    mlen = kmem.shape[1]
    full = lambda a: pl.BlockSpec(a.shape, lambda b, i: (0,) * a.ndim)
    tok = lambda w: pl.BlockSpec((1, tm, w), lambda b, i: (b, i, 0))
    memspec = pl.BlockSpec((1, mlen, d), lambda b, i: (b, 0, 0))
    return pl.pallas_call(
        _mix_body,
        grid=(bsz, seq // tm),
        in_specs=[tok(d), tok(yda.shape[-1]), tok(yml.shape[-1]), full(wo1), full(wo2), full(gx), full(wq),
                  memspec, memspec, full(wxo), full(gf), full(wr_hi), full(wr_lo), full(br)],
        out_specs=[pl.BlockSpec((1, tm * SUBLANE, LANE), lambda b, i: (b, i, 0)),
                   pl.BlockSpec((1, SUBLANE, tm), lambda b, i: (b, 0, i))],
        out_shape=[
            jax.ShapeDtypeStruct((bsz, seq * d // LANE, LANE), F32),
            jax.ShapeDtypeStruct((bsz, SUBLANE, seq), F32),
        ],
        compiler_params=_params("parallel", "parallel"),
        name="mix_xattn_router",
    )(x, yda, yml, wo1, wo2, gx, wq, kmem, vmem, wxo, gf, wr_hi, wr_lo, br)


def _to_token_tiles(ref, first_row, dense):
    n, d = dense.shape
    for c in range(d // LANE):
        ref[pl.ds(first_row * SUBLANE + c, n, stride=SUBLANE), :] = dense[:, c * LANE:(c + 1) * LANE]


def _from_token_tiles(ref, first_row, n):
    return jnp.concatenate(
        [ref[pl.ds(first_row * SUBLANE + c, n, stride=SUBLANE), :] for c in range(SUBLANE)], axis=1)


def _moe_body(start_ref, cnt_ref, x_hbm, tok_hbm, wt_hbm, gf_ref, gfin_ref, w1_ref, w3_ref, w2_ref, out_hbm,
              acc, h3, xg, yt, stage, tok_s, wt_s, sems, load_sems, store_sems, *, final_norm):
    tile = pl.program_id(0)
    e = pl.program_id(1)
    tt = acc.shape[0] // SUBLANE
    chunk_rows = stage.shape[1]
    group = SUBLANE

    def row_tile(ref, row0):
        return ref.at[pl.ds(pl.multiple_of(row0, SUBLANE), SUBLANE), :]

    def token_tile(ref, r):
        return row_tile(ref, r * SUBLANE)

    @pl.when(jnp.logical_and(tile == 0, e == 0))
    def _():
        xg[...] = jnp.zeros(xg.shape, F32)

    n_chunks = tt // chunk_rows
    chunk_tiles = chunk_rows * SUBLANE

    def load_chunk(i):
        rows = pl.ds(pl.multiple_of(i * chunk_tiles, chunk_tiles), chunk_tiles)
        return pltpu.make_async_copy(x_hbm.at[tile, rows, :], acc.at[rows, :], load_sems.at[i])

    @pl.when(e == 0)
    def _():
        lists = [pltpu.make_async_copy(tok_hbm.at[tile], tok_s, sems.at[0]),
                 pltpu.make_async_copy(wt_hbm.at[tile], wt_s, sems.at[1])]
        for cp in lists:
            cp.start()
        for i in range(n_chunks):
            load_chunk(i).start()

        def ffn_norm(i, c):
            load_chunk(i).wait()
            r0 = i * chunk_rows
            _to_token_tiles(h3, r0, _rms(_from_token_tiles(acc, r0, chunk_rows), gf_ref[...]))
            return c
        lax.fori_loop(0, n_chunks, ffn_norm, 0)
        for cp in lists:
            cp.wait()

    start = start_ref[tile * N_EXPERTS + e]
    count = cnt_ref[tile * N_EXPERTS + e]

    def run_block(first, cnt, rows):
        idx0 = start + first
        n_groups = cnt // group

        def gather_group(i, c):
            for u in range(group):
                r = i * group + u
                token_tile(xg, r)[...] = row_tile(h3, tok_s[idx0 + r])[...]
            return c

        def gather_row(r, c):
            token_tile(xg, r)[...] = row_tile(h3, tok_s[idx0 + r])[...]
            return c

        lax.fori_loop(0, n_groups, gather_group, 0)
        lax.fori_loop(n_groups * group, cnt, gather_row, 0)

        xb = _from_token_tiles(xg, 0, rows).astype(BF16)
        a = _dot(xb, w1_ref[0])
        b = _dot(xb, w3_ref[0])
        hmid = (a * _sigmoid(a) * b).astype(BF16)
        _to_token_tiles(yt, 0, _dot(hmid, w2_ref[0]))

        def scatter_group(i, c):
            toks = [tok_s[idx0 + i * group + u] for u in range(group)]
            new = [row_tile(acc, toks[u])[...] + wt_s[idx0 + i * group + u] * token_tile(yt, i * group + u)[...]
                   for u in range(group)]
            for u in range(group):
                row_tile(acc, toks[u])[...] = new[u]
            return c

        def scatter_row(r, c):
            t = tok_s[idx0 + r]
            row_tile(acc, t)[...] = row_tile(acc, t)[...] + wt_s[idx0 + r] * token_tile(yt, r)[...]
            return c

        lax.fori_loop(0, n_groups, scatter_group, 0)
        lax.fori_loop(n_groups * group, cnt, scatter_row, 0)

    big = xg.shape[0] // SUBLANE
    small = big // 2
    n_small = (count + small - 1) // small
    n_big = n_small // 2

    def big_block(i, c):
        run_block(i * big, jnp.minimum(big, count - i * big), big)
        return c

    lax.fori_loop(0, n_big, big_block, 0)

    @pl.when(n_small % 2 == 1)
    def _():
        run_block(n_big * big, count - n_big * big, small)

    @pl.when(e == pl.num_programs(1) - 1)
    def _():
        n_slots = stage.shape[0]

        def store_chunk(i, slot):
            r0 = pl.multiple_of(i * chunk_rows, chunk_rows)
            return pltpu.make_async_copy(stage.at[slot], out_hbm.at[tile, pl.ds(r0, chunk_rows), :],
                                         store_sems.at[slot])

        def write_back(p, c):
            for slot in range(n_slots):
                i = p * n_slots + slot

                @pl.when(p > 0)
                def _():
                    store_chunk(i - n_slots, slot).wait()

                rows = _from_token_tiles(acc, i * chunk_rows, chunk_rows)
                stage[slot] = _rms(rows, gfin_ref[...]) if final_norm else rows
                store_chunk(i, slot).start()
            return c
        lax.fori_loop(0, n_chunks // n_slots, write_back, 0)
        for slot in range(n_slots):
            store_chunk(n_chunks - n_slots + slot, slot).wait()


def _moe(start, cnt, tok, wt, x_token_tiles, gf, gfin, w1, w3, w2, final_norm):
    ntiles, tile_rows, _ = x_token_tiles.shape
    tt = tile_rows // SUBLANE
    d = SUBLANE * LANE
    de = w1.shape[-1]
    seg = tok.shape[1]
    body = functools.partial(_moe_body, final_norm=final_norm)
    hbm = pl.BlockSpec(memory_space=pl.ANY)
    grid_spec = pltpu.PrefetchScalarGridSpec(
        num_scalar_prefetch=2,
        grid=(ntiles, N_EXPERTS),
        in_specs=[
            hbm, hbm, hbm,
            pl.BlockSpec((1, d), lambda t, e, *_: (0, 0)),
            pl.BlockSpec((1, d), lambda t, e, *_: (0, 0)),
            pl.BlockSpec((1, d, de), lambda t, e, *_: (e, 0, 0)),
            pl.BlockSpec((1, d, de), lambda t, e, *_: (e, 0, 0)),
            pl.BlockSpec((1, de, d), lambda t, e, *_: (e, 0, 0)),
        ],
        out_specs=hbm,
        scratch_shapes=[
            pltpu.VMEM((tile_rows, LANE), F32),
            pltpu.VMEM((tile_rows, LANE), F32),
            pltpu.VMEM((2 * MOE_ROWS * SUBLANE, LANE), F32),
            pltpu.VMEM((2 * MOE_ROWS * SUBLANE, LANE), F32),
            pltpu.VMEM((2, 2 * MOE_ROWS, d), F32),
            pltpu.SMEM((seg,), jnp.int32),
            pltpu.SMEM((seg,), F32),
            pltpu.SemaphoreType.DMA((2,)),
            pltpu.SemaphoreType.DMA((tt // (2 * MOE_ROWS),)),
            pltpu.SemaphoreType.DMA((2,)),
        ],
    )
    return pl.pallas_call(
        body,
        grid_spec=grid_spec,
        out_shape=jax.ShapeDtypeStruct((ntiles, tt, d), F32),
        compiler_params=_params("arbitrary", "arbitrary"),
        name="moe",
    )(start, cnt, x_token_tiles, tok, wt, gf, gfin, w1, w3, w2)


def _rope_tables(seq):
    half = DA_HEAD_DIM // 2
    inv_freq = ROPE_THETA ** (-jnp.arange(0, DA_HEAD_DIM, 2, dtype=F32) / DA_HEAD_DIM)
    ang = jnp.arange(seq, dtype=F32)[:, None] * inv_freq[None, :]
    cos, sin = jnp.cos(ang), jnp.sin(ang)
    reps = LANE // half
    cos_t = jnp.tile(cos, (1, reps))
    sin_t = jnp.tile(jnp.concatenate([-sin, sin], axis=1), (1, reps // 2))
    return cos_t, sin_t


def _sort_assignments(route, tt):
    bsz, _, seq = route.shape
    per = seq // tt
    ntiles = bsz * per
    tile = lambda a: a.reshape(bsz, TOP_K_ROWS, per, tt).transpose(0, 2, 1, 3).reshape(ntiles, TOP_K_ROWS * tt)
    keys = tile(route[:, 0:TOP_K_ROWS, :]).astype(jnp.int32)
    wts = tile(route[:, TOP_K_ROWS:2 * TOP_K_ROWS, :])
    order = jnp.argsort(keys, axis=-1).astype(jnp.int32)
    tok = (order % tt) * SUBLANE
    wt = jnp.take_along_axis(wts, order, axis=-1)
    counts = jnp.sum(keys[:, :, None] == jnp.arange(N_EXPERTS, dtype=jnp.int32)[None, None, :], axis=1,
                     dtype=jnp.int32)
    start = jnp.cumsum(counts, axis=-1, dtype=jnp.int32) - counts
    return start.reshape(-1), counts.reshape(-1), tok, wt


TOP_K_ROWS = 2


def kernel(x, mem, norm_mix_g, w_in, w_out, da_lambda, da_subln_g, ml_conv_w, ml_conv_b, ml_gate_b, ml_norm_g, norm_x_g, norm_mem_g, w_xq, w_xk, w_xv, w_xo, norm_ffn_g, w_router_group, b_router_group, w_router_expert, b_router_expert, w1, w3, w2, norm_final_g):
    depth = w_in.shape[0]
    bsz, seq, d = x.shape
    da_qk = DA_HEADS * 2 * DA_HEAD_DIM
    da_width = DA_HEADS * DA_VAL_DIM
    ml_width = ml_norm_g.shape[-1]
    n_main = 2 * da_qk + da_width + 4 * ml_width
    n_in = w_in.shape[-1]
    cos_t, sin_t = _rope_tables(seq)
    row = lambda v: v.reshape(1, -1).astype(F32)

    for l in range(depth):
        lam_init = 0.8 - 0.6 * math.exp(-0.3 * l)
        w_pad = jnp.pad(w_in[l], ((0, 0), (0, n_main + LANE - n_in))).astype(BF16)
        qt, vt, proj, gcol, grow = _inproj(x, row(norm_mix_g[l]), w_pad, cos_t, sin_t)

        yda = _diffattn(qt, proj, vt, da_lambda[l].astype(F32), da_subln_g[l].astype(F32).reshape(-1, 1), lam_init)

        gb = ml_gate_b[l].astype(F32).reshape(1, 2 * ML_HEADS)
        gb_col = jnp.pad(gb, ((0, 0), (0, LANE - 2 * ML_HEADS)))
        gb_row = gb.reshape(2 * ML_HEADS, 1)
        yml = _mlstm(proj, gcol, grow, ml_conv_w[l].reshape(ML_CONV, 2 * ml_width).astype(F32),
                     row(ml_conv_b[l]), gb_col, gb_row, row(ml_norm_g[l]), da_qk)

        kmem, vmem = _memkv(mem, row(norm_mem_g[l]), w_xk[l].astype(BF16), w_xv[l].astype(BF16))

        wr = jnp.concatenate([w_router_group[l], w_router_expert[l]], axis=1).astype(F32).T
        wr = jnp.pad(wr, ((0, ROUTER_ROWS - wr.shape[0]), (0, 0)))
        wr_hi = wr.astype(BF16)
        wr_lo = (wr - wr_hi.astype(F32)).astype(BF16)
        br = jnp.pad(jnp.concatenate([b_router_group[l], b_router_expert[l]]).astype(F32),
                     (0, ROUTER_ROWS - N_GROUPS - N_EXPERTS)).reshape(ROUTER_ROWS, 1)
        wo = w_out[l].astype(BF16)
        x2, route = _mix(x, yda, yml, wo[:da_width], wo[da_width:], row(norm_x_g[l]), w_xq[l].astype(BF16),
                         kmem, vmem, w_xo[l].astype(BF16), row(norm_ffn_g[l]), wr_hi, wr_lo, br)

        tt = min(MOE_TILE, seq)
        x = _moe(*_sort_assignments(route, tt), x2.reshape(bsz * seq // tt, tt * d // LANE, LANE),
                 row(norm_ffn_g[l]), row(norm_final_g),
                 w1[l].astype(BF16), w3[l].astype(BF16), w2[l].astype(BF16),
                 final_norm=(l + 1 == depth)).reshape(bsz, seq, d)
    return x
```

```python
import functools
import math

import jax
import jax.numpy as jnp
from jax import lax
from jax.experimental import pallas as pl
from jax.experimental.pallas import tpu as pltpu

F32 = jnp.float32
BF16 = jnp.bfloat16

LANE = 128
SUBLANE = 8
VMEM_LIMIT_BYTES = 56 * 1024 * 1024

EPS = 1e-6
ROPE_THETA = 10000.0
DA_HEADS = 4
DA_HEAD_DIM = 64
DA_VAL_DIM = 2 * DA_HEAD_DIM
ML_HEADS = 4
ML_CONV = 4
X_HEADS = 4
N_GROUPS = 4
EXPERTS_PER_GROUP = 8
N_EXPERTS = N_GROUPS * EXPERTS_PER_GROUP
ROUTER_ROWS = 40

ROW_TILE = 512
ML_CHUNK = 256
MOE_TILE = 4096
MOE_ROWS = 128

NEG_INF = float("-inf")
LOG2_E = math.log2(math.e)


def _rms(x, g):
    return x * lax.rsqrt(jnp.mean(x * x, axis=-1, keepdims=True) + EPS) * g


def _dot(a, b):
    return jnp.dot(a, b, preferred_element_type=F32)


def _dot_nt(a, b):
    return lax.dot_general(a, b, (((1,), (1,)), ((), ())), preferred_element_type=F32)


def _dot_tn(a, b):
    return lax.dot_general(a, b, (((0,), (0,)), ((), ())), preferred_element_type=F32)


def _split_bf16(x):
    hi = x.astype(BF16)
    lo = (x - hi.astype(F32)).astype(BF16)
    return hi, lo


def _params(*semantics):
    return pltpu.CompilerParams(dimension_semantics=semantics, vmem_limit_bytes=VMEM_LIMIT_BYTES)


def _inproj_body(x_ref, g_ref, w_ref, cos_ref, sin_ref, qt_ref, vt_ref, proj_ref, gcol_ref, grow_ref):
    hb = _rms(x_ref[0], g_ref[...]).astype(BF16)
    cos = cos_ref[...]
    sin = sin_ref[...]
    lane = lax.broadcasted_iota(jnp.int32, cos.shape, 1)
    first_half = (lane % DA_HEAD_DIM) < (DA_HEAD_DIM // 2)
    n_main = w_ref.shape[1] - LANE
    wide = 4 * LANE
    for c in range(n_main // LANE):
        if c % (wide // LANE) == 0:
            acc = _dot(hb, w_ref[:, c * LANE:c * LANE + wide])
        t = acc[:, (c * LANE) % wide:(c * LANE) % wide + LANE]
        if c < 2 * DA_HEADS:
            rot = jnp.where(first_half, pltpu.roll(t, LANE - DA_HEAD_DIM // 2, 1),
                            pltpu.roll(t, DA_HEAD_DIM // 2, 1))
            t = t * cos + rot * sin
        if c < DA_HEADS:
            qt_ref[0, 0, c * LANE:(c + 1) * LANE, :] = (t * (DA_HEAD_DIM ** -0.5 * LOG2_E)).T.astype(BF16)
        elif c < 2 * DA_HEADS:
            proj_ref[0, :, (c - DA_HEADS) * LANE:(c - DA_HEADS + 1) * LANE] = t.astype(BF16)
        elif c < 3 * DA_HEADS:
            vt_ref[0, 0, (c - 2 * DA_HEADS) * LANE:(c - 2 * DA_HEADS + 1) * LANE, :] = t.T.astype(BF16)
        else:
            proj_ref[0, :, (c - 2 * DA_HEADS) * LANE:(c - 2 * DA_HEADS + 1) * LANE] = t.astype(BF16)
    gates = _dot(hb, w_ref[:, n_main:n_main + LANE])
    gcol_ref[0] = gates
    grow_ref[0] = gates.T[:SUBLANE, :]


def _inproj(x, g, w_pad, cos_t, sin_t):
    bsz, seq, d = x.shape
    tm = min(ROW_TILE, seq)
    nt = seq // tm
    da_w = DA_HEADS * DA_VAL_DIM
    n_proj = w_pad.shape[1] - LANE - 2 * da_w
    return pl.pallas_call(
        _inproj_body,
        grid=(bsz, nt),
        in_specs=[
            pl.BlockSpec((1, tm, d), lambda b, i: (b, i, 0)),
            pl.BlockSpec((1, d), lambda b, i: (0, 0)),
            pl.BlockSpec(w_pad.shape, lambda b, i: (0, 0)),
            pl.BlockSpec((tm, LANE), lambda b, i: (i, 0)),
            pl.BlockSpec((tm, LANE), lambda b, i: (i, 0)),
        ],
        out_specs=[
            pl.BlockSpec((1, 1, da_w, tm), lambda b, i: (b, i, 0, 0)),
            pl.BlockSpec((1, 1, da_w, tm), lambda b, i: (b, i, 0, 0)),
            pl.BlockSpec((1, tm, n_proj), lambda b, i: (b, i, 0)),
            pl.BlockSpec((1, tm, LANE), lambda b, i: (b, i, 0)),
            pl.BlockSpec((1, SUBLANE, tm), lambda b, i: (b, 0, i)),
        ],
        out_shape=[
            jax.ShapeDtypeStruct((bsz, nt, da_w, tm), BF16),
            jax.ShapeDtypeStruct((bsz, nt, da_w, tm), BF16),
            jax.ShapeDtypeStruct((bsz, seq, n_proj), BF16),
            jax.ShapeDtypeStruct((bsz, seq, LANE), F32),
            jax.ShapeDtypeStruct((bsz, SUBLANE, seq), F32),
        ],
        compiler_params=_params("parallel", "parallel"),
        name="inproj",
    )(x, g, w_pad, cos_t, sin_t)


def _diffattn_body(qt_ref, k_ref, vt_ref, lam_ref, g_ref, o_ref, m_scr, acc_scr, p_scr, *, lam_init):
    qi = pl.program_id(2)
    tq = qt_ref.shape[3]
    qt = qt_ref[0, 0]
    row = lax.broadcasted_iota(jnp.int32, qt.shape, 0)
    zero = jnp.zeros_like(qt)
    q2t = jnp.concatenate([jnp.where(row < DA_HEAD_DIM, qt, zero), jnp.where(row >= DA_HEAD_DIM, qt, zero)], axis=1)
    pad_rows = acc_scr.shape[0] - DA_VAL_DIM
    ones_rows = jnp.where(lax.broadcasted_iota(jnp.int32, (pad_rows, tq), 0) == 0, 1.0, 0.0).astype(BF16)

    m_scr[...] = jnp.full(m_scr.shape, NEG_INF, F32)
    acc_scr[...] = jnp.zeros(acc_scr.shape, F32)

    n_chains = 4
    width = 2 * tq // n_chains
    halves = tuple(slice(c * width, (c + 1) * width) for c in range(n_chains))

    def scores(j, masked, prev_values):
        k = k_ref[0, pl.ds(pl.multiple_of(j * tq, tq), tq), :]
        out = []
        for cols in halves:
            s = _dot(k, q2t[:, cols])
            pv = values(j - 1, cols) if prev_values else None
            if masked:
                key = lax.broadcasted_iota(jnp.int32, s.shape, 0)
                qry = lax.broadcasted_iota(jnp.int32, s.shape, 1) + cols.start % tq
                s = jnp.where(key <= qry, s, NEG_INF)
            m_prev = m_scr[:, cols]
            m_new = jnp.maximum(m_prev, jnp.max(s, axis=0, keepdims=True))
            p_scr[:, cols] = jnp.exp2(s - m_new).astype(BF16)
            m_scr[:, cols] = m_new
            out.append((jnp.exp2(m_prev - m_new), pv))
        return out

    def values(j, cols):
        v_aug = jnp.concatenate([vt_ref[0, j], ones_rows], axis=0)
        return _dot(v_aug, p_scr[:, cols])

    def step(j, masked):
        for cols, (alpha, pv) in zip(halves, scores(j, masked, True)):
            acc_scr[:, cols] = alpha * (acc_scr[:, cols] + pv)

    def full_step_pair(i, carry):
        step(1 + 2 * i, False)
        step(2 + 2 * i, False)
        return carry

    @pl.when(qi == 0)
    def _():
        scores(0, True, False)

    @pl.when(qi > 0)
    def _():
        scores(0, False, False)
        lax.fori_loop(0, (qi - 1) // 2, full_step_pair, 0)

        @pl.when((qi - 1) % 2 == 1)
        def _():
            step(qi - 1, False)

        step(qi, True)

    for cols in halves:
        acc_scr[:, cols] = acc_scr[:, cols] + values(qi, cols)

    lv = lam_ref[...]
    lam = (jnp.exp(jnp.sum(lv[0:1, :] * lv[1:2, :], axis=-1, keepdims=True))
           - jnp.exp(jnp.sum(lv[2:3, :] * lv[3:4, :], axis=-1, keepdims=True)) + lam_init)
    acc = acc_scr[...]
    o = acc[0:DA_VAL_DIM, :] / acc[DA_VAL_DIM:DA_VAL_DIM + 1, :]
    o = o[:, :tq] - lam * o[:, tq:]
    y = o * lax.rsqrt(jnp.mean(o * o, axis=0, keepdims=True) + EPS) * g_ref[...] * (1.0 - lam_init)
    o_ref[0] = y.T.astype(o_ref.dtype)


def _diffattn(qt, proj, vt, da_lambda, subln_g_col, lam_init):
    bsz, nt, _, tq = qt.shape
    seq = nt * tq
    body = functools.partial(_diffattn_body, lam_init=lam_init)
    return pl.pallas_call(
        body,
        grid=(bsz, DA_HEADS, nt),
        in_specs=[
            pl.BlockSpec((1, 1, DA_VAL_DIM, tq), lambda b, h, i: (b, i, h, 0)),
            pl.BlockSpec((1, seq, LANE), lambda b, h, i: (b, 0, h)),
            pl.BlockSpec((1, nt, DA_VAL_DIM, tq), lambda b, h, i: (b, 0, h, 0)),
            pl.BlockSpec(da_lambda.shape, lambda b, h, i: (0, 0)),
            pl.BlockSpec((DA_VAL_DIM, 1), lambda b, h, i: (0, 0)),
        ],
        out_specs=pl.BlockSpec((1, tq, LANE), lambda b, h, i: (b, i, h)),
        out_shape=jax.ShapeDtypeStruct((bsz, seq, DA_HEADS * DA_VAL_DIM), BF16),
        scratch_shapes=[
            pltpu.VMEM((1, 2 * tq), F32),
            pltpu.VMEM((DA_VAL_DIM + 2 * SUBLANE, 2 * tq), F32),
            pltpu.VMEM((tq, 2 * tq), BF16),
        ],
        compiler_params=_params("parallel", "parallel", "arbitrary"),
        name="diffattn",
    )(qt, proj, vt, da_lambda, subln_g_col)


def _log_sigmoid(x):
    return jnp.minimum(x, 0.0) - jnp.log(1.0 + jnp.exp(-jnp.abs(x)))


def _sigmoid(x):
    return 1.0 / (1.0 + jnp.exp(-x))


def _mlstm_body(q_ref, k_ref, v_ref, o_ref, gcol_ref, grow_ref, cw_ref, cb_ref, gbcol_ref, gbrow_ref, ng_ref,
                y_ref, xbuf, c_scr, m_scr):
    chunk = pl.program_id(1)
    L = q_ref.shape[1]
    width = q_ref.shape[2]
    dh = width // ML_HEADS
    tail = xbuf.shape[0]

    @pl.when(chunk == 0)
    def _():
        xbuf[...] = jnp.zeros(xbuf.shape, xbuf.dtype)
        c_scr[...] = jnp.zeros(c_scr.shape, F32)
        m_scr[...] = jnp.zeros(m_scr.shape, F32)

    u = jnp.concatenate([q_ref[0], k_ref[0]], axis=1)
    u_ext = jnp.concatenate([xbuf[...], u], axis=0)
    src = lax.broadcasted_iota(jnp.int32, (L, L + tail), 1) - lax.broadcasted_iota(jnp.int32, (L, L + tail), 0)
    conv = cb_ref[...] + cw_ref[ML_CONV - 1:ML_CONV, :] * u.astype(F32)
    for back in range(1, ML_CONV):
        shift = jnp.where(src == tail - back, 1.0, 0.0).astype(BF16)
        conv = conv + cw_ref[ML_CONV - 1 - back:ML_CONV - back, :] * _dot(shift, u_ext)
    xbuf[...] = u[L - tail:, :]
    qk = conv * _sigmoid(conv)

    gc = gcol_ref[0] + gbcol_ref[...]
    gr = grow_ref[0] + gbrow_ref[...]
    row_id = lax.broadcasted_iota(jnp.int32, (L, L), 0)
    col_id = lax.broadcasted_iota(jnp.int32, (L, L), 1)
    causal = col_id <= row_id
    tri = jnp.where(causal, 1.0, 0.0).astype(BF16)
    tri_t = jnp.where(row_id <= col_id, 1.0, 0.0).astype(BF16)
    lfc_hi, lfc_lo = _split_bf16(_log_sigmoid(gc))
    lfr_hi, lfr_lo = _split_bf16(_log_sigmoid(gr))
    a_c = _dot(tri, lfc_hi) + _dot(tri, lfc_lo)
    a_r = _dot(lfr_hi, tri_t) + _dot(lfr_lo, tri_t)

    lane = lax.broadcasted_iota(jnp.int32, (L, dh), 1)
    ones_col = jnp.where(lane == 0, 1.0, 0.0).astype(BF16)

    for h in range(ML_HEADS):
        fh = ML_HEADS + h
        m_prev = m_scr[h:h + 1, 0:1]
        a_col = a_c[:, fh:fh + 1]
        ig_col = gc[:, h:h + 1]
        a_row = a_r[fh:fh + 1, :]
        ig_row = gr[h:h + 1, :]
        log_d = jnp.where(causal, a_col - (a_row - ig_row), NEG_INF)
        log_inter = a_col + m_prev
        m_row = jnp.maximum(log_inter, jnp.max(log_d, axis=-1, keepdims=True))
        w_intra = jnp.exp(log_d - m_row)
        w_inter = jnp.exp(log_inter - m_row)

        qh = qk[:, h * dh:(h + 1) * dh].astype(BF16)
        kh = (qk[:, width + h * dh:width + (h + 1) * dh] * (dh ** -0.5)).astype(BF16)
        v_aug = jnp.concatenate([v_ref[0, :, h * dh:(h + 1) * dh], ones_col], axis=1)
        s = _dot_nt(qh, kh) * w_intra
        c_mat = c_scr[h]
        num = w_inter * _dot(qh, c_mat.astype(BF16)) + _dot(s.astype(BF16), v_aug)
        den = num[:, dh:dh + 1]
        hid = num[:, 0:dh] / jnp.maximum(jnp.abs(den), jnp.exp(-m_row))

        g_last = a_col[L - 1:L, :]
        log_w = g_last - a_col + ig_col
        m_new = jnp.maximum(g_last + m_prev, jnp.max(log_w, axis=0, keepdims=True))
        w_state = jnp.exp(log_w - m_new)
        decay = jnp.exp(g_last + m_prev - m_new)
        wv = (w_state * v_aug.astype(F32)).astype(BF16)
        c_scr[h] = decay * c_mat + _dot_tn(kh, wv)
        m_scr[h:h + 1, :] = jnp.broadcast_to(m_new, (1, m_scr.shape[1]))

        mu = jnp.mean(hid, axis=-1, keepdims=True)
        cen = hid - mu
        var = jnp.mean(cen * cen, axis=-1, keepdims=True)
        hn = cen * lax.rsqrt(var + EPS) * ng_ref[:, h * dh:(h + 1) * dh]
        gate = _sigmoid(o_ref[0, :, h * dh:(h + 1) * dh].astype(F32))
        y_ref[0, :, h * dh:(h + 1) * dh] = (gate * hn).astype(y_ref.dtype)


def _mlstm(proj, gcol, grow, conv_w, conv_b, gb_col, gb_row, norm_g, col0):
    bsz, seq, _ = proj.shape
    width = norm_g.shape[-1]
    dh = width // ML_HEADS
    L = min(ML_CHUNK, seq)
    blk0 = col0 // width
    spec = lambda k: pl.BlockSpec((1, L, width), lambda b, c: (b, c, blk0 + k))
    return pl.pallas_call(
        _mlstm_body,
        grid=(bsz, seq // L),
        in_specs=[
            spec(0), spec(1), spec(2), spec(3),
            pl.BlockSpec((1, L, LANE), lambda b, c: (b, c, 0)),
            pl.BlockSpec((1, SUBLANE, L), lambda b, c: (b, 0, c)),
            pl.BlockSpec(conv_w.shape, lambda b, c: (0, 0)),
            pl.BlockSpec(conv_b.shape, lambda b, c: (0, 0)),
            pl.BlockSpec(gb_col.shape, lambda b, c: (0, 0)),
            pl.BlockSpec(gb_row.shape, lambda b, c: (0, 0)),
            pl.BlockSpec(norm_g.shape, lambda b, c: (0, 0)),
        ],
        out_specs=pl.BlockSpec((1, L, width), lambda b, c: (b, c, 0)),
        out_shape=jax.ShapeDtypeStruct((bsz, seq, width), BF16),
        scratch_shapes=[
            pltpu.VMEM((2 * SUBLANE, 2 * width), BF16),
            pltpu.VMEM((ML_HEADS, dh, 2 * dh), F32),
            pltpu.VMEM((SUBLANE, LANE), F32),
        ],
        compiler_params=_params("parallel", "arbitrary"),
        name="mlstm",
    )(proj, proj, proj, proj, gcol, grow, conv_w, conv_b, gb_col, gb_row, norm_g)


def _memkv_body(mem_ref, g_ref, wk_ref, wv_ref, k_ref, v_ref):
    mb = _rms(mem_ref[0], g_ref[...]).astype(BF16)
    k_ref[0] = _dot(mb, wk_ref[...]).astype(BF16)
    v_ref[0] = _dot(mb, wv_ref[...]).astype(BF16)


def _memkv(mem, g, wk, wv):
    bsz, mlen, d = mem.shape
    full = lambda a: pl.BlockSpec(a.shape, lambda b: (0,) * a.ndim)
    blk = pl.BlockSpec((1, mlen, d), lambda b: (b, 0, 0))
    return pl.pallas_call(
        _memkv_body,
        grid=(bsz,),
        in_specs=[blk, full(g), full(wk), full(wv)],
        out_specs=[blk, blk],
        out_shape=[jax.ShapeDtypeStruct((bsz, mlen, d), BF16)] * 2,
        compiler_params=_params("parallel"),
        name="memkv",
    )(mem, g, wk, wv)


def _mix_body(x_ref, yda_ref, yml_ref, wo1_ref, wo2_ref, gx_ref, wq_ref, km_ref, vm_ref, wxo_ref, gf_ref,
              wrh_ref, wrl_ref, br_ref, x2_ref, route_ref):
    x1 = x_ref[0] + _dot(yda_ref[0], wo1_ref[...]) + _dot(yml_ref[0], wo2_ref[...])

    d = x1.shape[-1]
    dh = d // X_HEADS
    q = _dot(_rms(x1, gx_ref[...]).astype(BF16), wq_ref[...]).astype(BF16)
    heads = []
    for h in range(X_HEADS):
        sl = slice(h * dh, (h + 1) * dh)
        s = _dot_nt(q[:, sl], km_ref[0, :, sl]) * (dh ** -0.5)
        p = jnp.exp(s - jnp.max(s, axis=-1, keepdims=True))
        o = _dot(p.astype(BF16), vm_ref[0, :, sl]) / jnp.sum(p, axis=-1, keepdims=True)
        heads.append(o.astype(BF16))
    x2 = x1 + _dot(jnp.concatenate(heads, axis=1), wxo_ref[...])
    _to_token_tiles(x2_ref.at[0], 0, x2)

    h3 = _rms(x2, gf_ref[...])
    h_hi, h_lo = _split_bf16(h3)
    logits = (_dot_nt(wrh_ref[...], h_hi) + _dot_nt(wrh_ref[...], h_lo) + _dot_nt(wrl_ref[...], h_hi)
              + br_ref[...])
    tm = logits.shape[1]
    gl = logits[0:N_GROUPS, :]
    g_iota = lax.broadcasted_iota(jnp.int32, gl.shape, 0)
    g_max = jnp.max(gl, axis=0, keepdims=True)
    g_idx = jnp.min(jnp.where(gl == g_max, g_iota, N_GROUPS), axis=0, keepdims=True)
    g_w = 1.0 / jnp.sum(jnp.exp(gl - g_max), axis=0, keepdims=True)
    el = jnp.zeros((EXPERTS_PER_GROUP, tm), F32)
    for g in range(N_GROUPS):
        lo = N_GROUPS + g * EXPERTS_PER_GROUP
        el = jnp.where(g_idx == g, logits[lo:lo + EXPERTS_PER_GROUP, :], el)
    e_iota = lax.broadcasted_iota(jnp.int32, el.shape, 0)
    e1 = jnp.max(el, axis=0, keepdims=True)
    i1 = jnp.min(jnp.where(el == e1, e_iota, EXPERTS_PER_GROUP), axis=0, keepdims=True)
    el2 = jnp.where(e_iota == i1, NEG_INF, el)
    e2 = jnp.max(el2, axis=0, keepdims=True)
    i2 = jnp.min(jnp.where(el2 == e2, e_iota, EXPERTS_PER_GROUP), axis=0, keepdims=True)
    p2 = jnp.exp(e2 - e1)
    w1 = g_w / (1.0 + p2)
    w2 = g_w * p2 / (1.0 + p2)
    base = g_idx * EXPERTS_PER_GROUP
    zero = jnp.zeros((1, tm), F32)
    route_ref[0] = jnp.concatenate(
        [(base + i1).astype(F32), (base + i2).astype(F32), w1, w2, zero, zero, zero, zero], axis=0)


def _mix(x, yda, yml, wo1, wo2, gx, wq, kmem, vmem, wxo, gf, wr_hi, wr_lo, br):
    bsz, seq, d = x.shape
    tm = min(ROW_TILE, seq)
    mlen = kmem.shape[1]
    full = lambda a: pl.BlockSpec(a.shape, lambda b, i: (0,) * a.ndim)
    tok = lambda w: pl.BlockSpec((1, tm, w), lambda b, i: (b, i, 0))
    memspec = pl.BlockSpec((1, mlen, d), lambda b, i: (b, 0, 0))
    return pl.pallas_call(
        _mix_body,
        grid=(bsz, seq // tm),
        in_specs=[tok(d), tok(yda.shape[-1]), tok(yml.shape[-1]), full(wo1), full(wo2), full(gx), full(wq),
                  memspec, memspec, full(wxo), full(gf), full(wr_hi), full(wr_lo), full(br)],
        out_specs=[pl.BlockSpec((1, tm * SUBLANE, LANE), lambda b, i: (b, i, 0)),
                   pl.BlockSpec((1, SUBLANE, tm), lambda b, i: (b, 0, i))],
        out_shape=[
            jax.ShapeDtypeStruct((bsz, seq * d // LANE, LANE), F32),
            jax.ShapeDtypeStruct((bsz, SUBLANE, seq), F32),
        ],
        compiler_params=_params("parallel", "parallel"),
        name="mix_xattn_router",
    )(x, yda, yml, wo1, wo2, gx, wq, kmem, vmem, wxo, gf, wr_hi, wr_lo, br)


def _to_token_tiles(ref, first_row, dense):
    n, d = dense.shape
    for c in range(d // LANE):
        ref[pl.ds(first_row * SUBLANE + c, n, stride=SUBLANE), :] = dense[:, c * LANE:(c + 1) * LANE]


def _from_token_tiles(ref, first_row, n):
    return jnp.concatenate(
        [ref[pl.ds(first_row * SUBLANE + c, n, stride=SUBLANE), :] for c in range(SUBLANE)], axis=1)


def _moe_body(start_ref, cnt_ref, x_hbm, tok_hbm, wt_hbm, gf_ref, gfin_ref, w1_ref, w3_ref, w2_ref, out_hbm,
              acc, h3, xg, yt, stage, tok_s, wt_s, sems, load_sems, store_sems, *, final_norm):
    tile = pl.program_id(0)
    e = pl.program_id(1)
    tt = h3.shape[0] // SUBLANE
    chunk_rows = stage.shape[1]
    group = SUBLANE

    def row_tile(ref, row0):
        return ref.at[pl.ds(pl.multiple_of(row0, SUBLANE), SUBLANE), :]

    def token_tile(ref, r):
        return ref.at[pl.ds(r * SUBLANE, SUBLANE), :]

    spare_row = tt * SUBLANE

    n_chunks = tt // chunk_rows
    chunk_tiles = chunk_rows * SUBLANE

    def load_chunk(i):
        rows = pl.ds(pl.multiple_of(i * chunk_tiles, chunk_tiles), chunk_tiles)
        return pltpu.make_async_copy(x_hbm.at[tile, rows, :], acc.at[rows, :], load_sems.at[i])

    @pl.when(e == 0)
    def _():
        lists = [pltpu.make_async_copy(tok_hbm.at[tile], tok_s, sems.at[0]),
                 pltpu.make_async_copy(wt_hbm.at[tile], wt_s, sems.at[1])]
        for cp in lists:
            cp.start()
        for i in range(n_chunks):
            load_chunk(i).start()
        token_tile(acc, tt)[...] = jnp.zeros((SUBLANE, LANE), F32)

        def ffn_norm(i, c):
            load_chunk(i).wait()
            r0 = i * chunk_rows
            _to_token_tiles(h3, r0, _rms(_from_token_tiles(acc, r0, chunk_rows), gf_ref[...]))
            return c
        lax.fori_loop(0, n_chunks, ffn_norm, 0)
        for cp in lists:
            cp.wait()

    start = start_ref[tile * N_EXPERTS + e]
    count = cnt_ref[tile * N_EXPERTS + e]

    def run_block(first, cnt, rows):
        idx0 = start + first
        for r in range(rows):
            token_tile(xg, r)[...] = row_tile(h3, tok_s[idx0 + r])[...]

        xb = _from_token_tiles(xg, 0, rows).astype(BF16)
        a = _dot(xb, w1_ref[0])
        b = _dot(xb, w3_ref[0])
        hmid = (a * _sigmoid(a) * b).astype(BF16)
        _to_token_tiles(yt, 0, _dot(hmid, w2_ref[0]))

        for g0 in range(0, rows, group):
            dst = [jnp.where(g0 + u < cnt, tok_s[idx0 + g0 + u], spare_row) for u in range(group)]
            new = [row_tile(acc, dst[u])[...] + wt_s[idx0 + g0 + u] * token_tile(yt, g0 + u)[...]
                   for u in range(group)]
            for u in range(group):
                row_tile(acc, dst[u])[...] = new[u]

    big = xg.shape[0] // SUBLANE
    small = big // 2
    n_small = (count + small - 1) // small
    n_big = n_small // 2

    def big_block(i, c):
        run_block(i * big, jnp.minimum(big, count - i * big), big)
        return c

    lax.fori_loop(0, n_big, big_block, 0)

    @pl.when(n_small % 2 == 1)
    def _():
        run_block(n_big * big, count - n_big * big, small)

    @pl.when(e == pl.num_programs(1) - 1)
    def _():
        n_slots = stage.shape[0]

        def store_chunk(i, slot):
            r0 = pl.multiple_of(i * chunk_rows, chunk_rows)
            return pltpu.make_async_copy(stage.at[slot], out_hbm.at[tile, pl.ds(r0, chunk_rows), :],
                                         store_sems.at[slot])

        def write_back(p, c):
            for slot in range(n_slots):
                i = p * n_slots + slot

                @pl.when(p > 0)
                def _():
                    store_chunk(i - n_slots, slot).wait()

                rows = _from_token_tiles(acc, i * chunk_rows, chunk_rows)
                stage[slot] = _rms(rows, gfin_ref[...]) if final_norm else rows
                store_chunk(i, slot).start()
            return c
        lax.fori_loop(0, n_chunks // n_slots, write_back, 0)
        for slot in range(n_slots):
            store_chunk(n_chunks - n_slots + slot, slot).wait()


def _moe(start, cnt, tok, wt, x_token_tiles, gf, gfin, w1, w3, w2, final_norm):
    ntiles, tile_rows, _ = x_token_tiles.shape
    tt = tile_rows // SUBLANE
    d = SUBLANE * LANE
    de = w1.shape[-1]
    seg = tok.shape[1]
    body = functools.partial(_moe_body, final_norm=final_norm)
    hbm = pl.BlockSpec(memory_space=pl.ANY)
    grid_spec = pltpu.PrefetchScalarGridSpec(
        num_scalar_prefetch=2,
        grid=(ntiles, N_EXPERTS),
        in_specs=[
            hbm, hbm, hbm,
            pl.BlockSpec((1, d), lambda t, e, *_: (0, 0)),
            pl.BlockSpec((1, d), lambda t, e, *_: (0, 0)),
            pl.BlockSpec((1, d, de), lambda t, e, *_: (e, 0, 0)),
            pl.BlockSpec((1, d, de), lambda t, e, *_: (e, 0, 0)),
            pl.BlockSpec((1, de, d), lambda t, e, *_: (e, 0, 0)),
        ],
        out_specs=hbm,
        scratch_shapes=[
            pltpu.VMEM((tile_rows + SUBLANE, LANE), F32),
            pltpu.VMEM((tile_rows, LANE), F32),
            pltpu.VMEM((2 * MOE_ROWS * SUBLANE, LANE), F32),
            pltpu.VMEM((2 * MOE_ROWS * SUBLANE, LANE), F32),
            pltpu.VMEM((2, 2 * MOE_ROWS, d), F32),
            pltpu.SMEM((seg,), jnp.int32),
            pltpu.SMEM((seg,), F32),
            pltpu.SemaphoreType.DMA((2,)),
            pltpu.SemaphoreType.DMA((tt // (2 * MOE_ROWS),)),
            pltpu.SemaphoreType.DMA((2,)),
        ],
    )
    return pl.pallas_call(
        body,
        grid_spec=grid_spec,
        out_shape=jax.ShapeDtypeStruct((ntiles, tt, d), F32),
        compiler_params=_params("arbitrary", "arbitrary"),
        name="moe",
    )(start, cnt, x_token_tiles, tok, wt, gf, gfin, w1, w3, w2)


def _rope_tables(seq):
    half = DA_HEAD_DIM // 2
    inv_freq = ROPE_THETA ** (-jnp.arange(0, DA_HEAD_DIM, 2, dtype=F32) / DA_HEAD_DIM)
    ang = jnp.arange(seq, dtype=F32)[:, None] * inv_freq[None, :]
    cos, sin = jnp.cos(ang), jnp.sin(ang)
    reps = LANE // half
    cos_t = jnp.tile(cos, (1, reps))
    sin_t = jnp.tile(jnp.concatenate([-sin, sin], axis=1), (1, reps // 2))
    return cos_t, sin_t


def _sort_assignments(route, tt):
    bsz, _, seq = route.shape
    per = seq // tt
    ntiles = bsz * per
    tile = lambda a: a.reshape(bsz, TOP_K_ROWS, per, tt).transpose(0, 2, 1, 3).reshape(ntiles, TOP_K_ROWS * tt)
    keys = tile(route[:, 0:TOP_K_ROWS, :]).astype(jnp.int32)
    wts = tile(route[:, TOP_K_ROWS:2 * TOP_K_ROWS, :])
    order = jnp.argsort(keys, axis=-1).astype(jnp.int32)
    tok = (order % tt) * SUBLANE
    wt = jnp.take_along_axis(wts, order, axis=-1)
    counts = jnp.sum(keys[:, :, None] == jnp.arange(N_EXPERTS, dtype=jnp.int32)[None, None, :], axis=1,
                     dtype=jnp.int32)
    start = jnp.cumsum(counts, axis=-1, dtype=jnp.int32) - counts
    over = ((0, 0), (0, 2 * MOE_ROWS))
    return start.reshape(-1), counts.reshape(-1), jnp.pad(tok, over), jnp.pad(wt, over)


TOP_K_ROWS = 2


def kernel(x, mem, norm_mix_g, w_in, w_out, da_lambda, da_subln_g, ml_conv_w, ml_conv_b, ml_gate_b, ml_norm_g, norm_x_g, norm_mem_g, w_xq, w_xk, w_xv, w_xo, norm_ffn_g, w_router_group, b_router_group, w_router_expert, b_router_expert, w1, w3, w2, norm_final_g):
    depth = w_in.shape[0]
    bsz, seq, d = x.shape
    da_qk = DA_HEADS * 2 * DA_HEAD_DIM
    da_width = DA_HEADS * DA_VAL_DIM
    ml_width = ml_norm_g.shape[-1]
    n_main = 2 * da_qk + da_width + 4 * ml_width
    n_in = w_in.shape[-1]
    cos_t, sin_t = _rope_tables(seq)
    row = lambda v: v.reshape(1, -1).astype(F32)

    for l in range(depth):
        lam_init = 0.8 - 0.6 * math.exp(-0.3 * l)
        w_pad = jnp.pad(w_in[l], ((0, 0), (0, n_main + LANE - n_in))).astype(BF16)
        qt, vt, proj, gcol, grow = _inproj(x, row(norm_mix_g[l]), w_pad, cos_t, sin_t)

        yda = _diffattn(qt, proj, vt, da_lambda[l].astype(F32), da_subln_g[l].astype(F32).reshape(-1, 1), lam_init)

        gb = ml_gate_b[l].astype(F32).reshape(1, 2 * ML_HEADS)
        gb_col = jnp.pad(gb, ((0, 0), (0, LANE - 2 * ML_HEADS)))
        gb_row = gb.reshape(2 * ML_HEADS, 1)
        yml = _mlstm(proj, gcol, grow, ml_conv_w[l].reshape(ML_CONV, 2 * ml_width).astype(F32),
                     row(ml_conv_b[l]), gb_col, gb_row, row(ml_norm_g[l]), da_qk)

        kmem, vmem = _memkv(mem, row(norm_mem_g[l]), w_xk[l].astype(BF16), w_xv[l].astype(BF16))

        wr = jnp.concatenate([w_router_group[l], w_router_expert[l]], axis=1).astype(F32).T
        wr = jnp.pad(wr, ((0, ROUTER_ROWS - wr.shape[0]), (0, 0)))
        wr_hi = wr.astype(BF16)
        wr_lo = (wr - wr_hi.astype(F32)).astype(BF16)
        br = jnp.pad(jnp.concatenate([b_router_group[l], b_router_expert[l]]).astype(F32),
                     (0, ROUTER_ROWS - N_GROUPS - N_EXPERTS)).reshape(ROUTER_ROWS, 1)
        wo = w_out[l].astype(BF16)
        x2, route = _mix(x, yda, yml, wo[:da_width], wo[da_width:], row(norm_x_g[l]), w_xq[l].astype(BF16),
                         kmem, vmem, w_xo[l].astype(BF16), row(norm_ffn_g[l]), wr_hi, wr_lo, br)

        tt = min(MOE_TILE, seq)
        x = _moe(*_sort_assignments(route, tt), x2.reshape(bsz * seq // tt, tt * d // LANE, LANE),
                 row(norm_ffn_g[l]), row(norm_final_g),
                 w1[l].astype(BF16), w3[l].astype(BF16), w2[l].astype(BF16),
                 final_norm=(l + 1 == depth)).reshape(bsz, seq, d)
    return x
```

```python
import functools
import math

import jax
import jax.numpy as jnp
from jax import lax
from jax.experimental import pallas as pl
from jax.experimental.pallas import tpu as pltpu

F32 = jnp.float32
BF16 = jnp.bfloat16

LANE = 128
SUBLANE = 8
VMEM_LIMIT_BYTES = 56 * 1024 * 1024

EPS = 1e-6
ROPE_THETA = 10000.0
DA_HEADS = 4
DA_HEAD_DIM = 64
DA_VAL_DIM = 2 * DA_HEAD_DIM
ML_HEADS = 4
ML_CONV = 4
X_HEADS = 4
N_GROUPS = 4
EXPERTS_PER_GROUP = 8
N_EXPERTS = N_GROUPS * EXPERTS_PER_GROUP
ROUTER_ROWS = 40

ROW_TILE = 512
ML_CHUNK = 256
MOE_TILE = 4096
MOE_ROWS = 128

NEG_INF = float("-inf")
LOG2_E = math.log2(math.e)


def _rms(x, g):
    return x * lax.rsqrt(jnp.mean(x * x, axis=-1, keepdims=True) + EPS) * g


def _dot(a, b):
    return jnp.dot(a, b, preferred_element_type=F32)


def _dot_nt(a, b):
    return lax.dot_general(a, b, (((1,), (1,)), ((), ())), preferred_element_type=F32)


def _dot_tn(a, b):
    return lax.dot_general(a, b, (((0,), (0,)), ((), ())), preferred_element_type=F32)


def _split_bf16(x):
    hi = x.astype(BF16)
    lo = (x - hi.astype(F32)).astype(BF16)
    return hi, lo


def _params(*semantics):
    return pltpu.CompilerParams(dimension_semantics=semantics, vmem_limit_bytes=VMEM_LIMIT_BYTES)


def _inproj_body(x_ref, g_ref, w_ref, cos_ref, sin_ref, qt_ref, vt_ref, mvt_ref, mot_ref, proj_ref, gcol_ref,
                 grow_ref):
    hb = _rms(x_ref[0], g_ref[...]).astype(BF16)
    cos = cos_ref[...]
    sin = sin_ref[...]
    lane = lax.broadcasted_iota(jnp.int32, cos.shape, 1)
    first_half = (lane % DA_HEAD_DIM) < (DA_HEAD_DIM // 2)
    n_main = w_ref.shape[1] - LANE
    wide = 4 * LANE
    for c in range(n_main // LANE):
        if c % (wide // LANE) == 0:
            acc = _dot(hb, w_ref[:, c * LANE:c * LANE + wide])
        t = acc[:, (c * LANE) % wide:(c * LANE) % wide + LANE]
        if c < 2 * DA_HEADS:
            rot = jnp.where(first_half, pltpu.roll(t, LANE - DA_HEAD_DIM // 2, 1),
                            pltpu.roll(t, DA_HEAD_DIM // 2, 1))
            t = t * cos + rot * sin
        if c < DA_HEADS:
            qt_ref[0, 0, c * LANE:(c + 1) * LANE, :] = (t * (DA_HEAD_DIM ** -0.5 * LOG2_E)).T.astype(BF16)
        elif c < 2 * DA_HEADS:
            proj_ref[0, :, (c - DA_HEADS) * LANE:(c - DA_HEADS + 1) * LANE] = t.astype(BF16)
        elif c < 3 * DA_HEADS:
            vt_ref[0, 0, (c - 2 * DA_HEADS) * LANE:(c - 2 * DA_HEADS + 1) * LANE, :] = t.T.astype(BF16)
        elif c < 3 * DA_HEADS + 2 * ML_HEADS:
            proj_ref[0, :, (c - 2 * DA_HEADS) * LANE:(c - 2 * DA_HEADS + 1) * LANE] = t.astype(BF16)
        elif c < 3 * DA_HEADS + 3 * ML_HEADS:
            m = c - 3 * DA_HEADS - 2 * ML_HEADS
            mvt_ref[0, 0, m * LANE:(m + 1) * LANE, :] = t.T.astype(BF16)
        else:
            m = c - 3 * DA_HEADS - 3 * ML_HEADS
            mot_ref[0, 0, m * LANE:(m + 1) * LANE, :] = t.T.astype(BF16)
    gates = _dot(hb, w_ref[:, n_main:n_main + LANE])
    gcol_ref[0] = gates
    grow_ref[0] = gates.T[:SUBLANE, :]


def _inproj(x, g, w_pad, cos_t, sin_t):
    bsz, seq, d = x.shape
    tm = min(ROW_TILE, seq)
    nt = seq // tm
    da_w = DA_HEADS * DA_VAL_DIM
    ml_w = ML_HEADS * LANE
    n_proj = w_pad.shape[1] - LANE - 2 * da_w - 2 * ml_w
    return pl.pallas_call(
        _inproj_body,
        grid=(bsz, nt),
        in_specs=[
            pl.BlockSpec((1, tm, d), lambda b, i: (b, i, 0)),
            pl.BlockSpec((1, d), lambda b, i: (0, 0)),
            pl.BlockSpec(w_pad.shape, lambda b, i: (0, 0)),
            pl.BlockSpec((tm, LANE), lambda b, i: (i, 0)),
            pl.BlockSpec((tm, LANE), lambda b, i: (i, 0)),
        ],
        out_specs=[
            pl.BlockSpec((1, 1, da_w, tm), lambda b, i: (b, i, 0, 0)),
            pl.BlockSpec((1, 1, da_w, tm), lambda b, i: (b, i, 0, 0)),
            pl.BlockSpec((1, 1, ml_w, tm), lambda b, i: (b, i, 0, 0)),
            pl.BlockSpec((1, 1, ml_w, tm), lambda b, i: (b, i, 0, 0)),
            pl.BlockSpec((1, tm, n_proj), lambda b, i: (b, i, 0)),
            pl.BlockSpec((1, tm, LANE), lambda b, i: (b, i, 0)),
            pl.BlockSpec((1, SUBLANE, tm), lambda b, i: (b, 0, i)),
        ],
        out_shape=[
            jax.ShapeDtypeStruct((bsz, nt, da_w, tm), BF16),
            jax.ShapeDtypeStruct((bsz, nt, da_w, tm), BF16),
            jax.ShapeDtypeStruct((bsz, nt, ml_w, tm), BF16),
            jax.ShapeDtypeStruct((bsz, nt, ml_w, tm), BF16),
            jax.ShapeDtypeStruct((bsz, seq, n_proj), BF16),
            jax.ShapeDtypeStruct((bsz, seq, LANE), F32),
            jax.ShapeDtypeStruct((bsz, SUBLANE, seq), F32),
        ],
        compiler_params=_params("parallel", "parallel"),
        name="inproj",
    )(x, g, w_pad, cos_t, sin_t)


def _diffattn_body(qt_ref, k_ref, vt_ref, lam_ref, g_ref, o_ref, m_scr, acc_scr, p_scr, *, lam_init):
    qi = pl.program_id(2)
    tq = qt_ref.shape[3]
    qt = qt_ref[0, 0]
    row = lax.broadcasted_iota(jnp.int32, qt.shape, 0)
    zero = jnp.zeros_like(qt)
    q2t = jnp.concatenate([jnp.where(row < DA_HEAD_DIM, qt, zero), jnp.where(row >= DA_HEAD_DIM, qt, zero)], axis=1)
    pad_rows = acc_scr.shape[0] - DA_VAL_DIM
    ones_rows = jnp.where(lax.broadcasted_iota(jnp.int32, (pad_rows, tq), 0) == 0, 1.0, 0.0).astype(BF16)

    m_scr[...] = jnp.full(m_scr.shape, NEG_INF, F32)
    acc_scr[...] = jnp.zeros(acc_scr.shape, F32)

    n_chains = 4
    width = 2 * tq // n_chains
    halves = tuple(slice(c * width, (c + 1) * width) for c in range(n_chains))

    def scores(j, masked, prev_values):
        k = k_ref[0, pl.ds(pl.multiple_of(j * tq, tq), tq), :]
        out = []
        for cols in halves:
            s = _dot(k, q2t[:, cols])
            pv = values(j - 1, cols) if prev_values else None
            if masked:
                key = lax.broadcasted_iota(jnp.int32, s.shape, 0)
                qry = lax.broadcasted_iota(jnp.int32, s.shape, 1) + cols.start % tq
                s = jnp.where(key <= qry, s, NEG_INF)
            m_prev = m_scr[:, cols]
            m_new = jnp.maximum(m_prev, jnp.max(s, axis=0, keepdims=True))
            p_scr[:, cols] = jnp.exp2(s - m_new).astype(BF16)
            m_scr[:, cols] = m_new
            out.append((jnp.exp2(m_prev - m_new), pv))
        return out

    def values(j, cols):
        v_aug = jnp.concatenate([vt_ref[0, j], ones_rows], axis=0)
        return _dot(v_aug, p_scr[:, cols])

    def step(j, masked):
        for cols, (alpha, pv) in zip(halves, scores(j, masked, True)):
            acc_scr[:, cols] = alpha * (acc_scr[:, cols] + pv)

    def full_step_pair(i, carry):
        step(1 + 2 * i, False)
        step(2 + 2 * i, False)
        return carry

    @pl.when(qi == 0)
    def _():
        scores(0, True, False)

    @pl.when(qi > 0)
    def _():
        scores(0, False, False)
        lax.fori_loop(0, (qi - 1) // 2, full_step_pair, 0)

        @pl.when((qi - 1) % 2 == 1)
        def _():
            step(qi - 1, False)

        step(qi, True)

    for cols in halves:
        acc_scr[:, cols] = acc_scr[:, cols] + values(qi, cols)

    lv = lam_ref[...]
    lam = (jnp.exp(jnp.sum(lv[0:1, :] * lv[1:2, :], axis=-1, keepdims=True))
           - jnp.exp(jnp.sum(lv[2:3, :] * lv[3:4, :], axis=-1, keepdims=True)) + lam_init)
    acc = acc_scr[...]
    o = acc[0:DA_VAL_DIM, :] / acc[DA_VAL_DIM:DA_VAL_DIM + 1, :]
    o = o[:, :tq] - lam * o[:, tq:]
    y = o * lax.rsqrt(jnp.mean(o * o, axis=0, keepdims=True) + EPS) * g_ref[...] * (1.0 - lam_init)
    o_ref[0] = y.T.astype(o_ref.dtype)


def _diffattn(qt, proj, vt, da_lambda, subln_g_col, lam_init):
    bsz, nt, _, tq = qt.shape
    seq = nt * tq
    body = functools.partial(_diffattn_body, lam_init=lam_init)
    return pl.pallas_call(
        body,
        grid=(bsz, DA_HEADS, nt),
        in_specs=[
            pl.BlockSpec((1, 1, DA_VAL_DIM, tq), lambda b, h, i: (b, i, h, 0)),
            pl.BlockSpec((1, seq, LANE), lambda b, h, i: (b, 0, h)),
            pl.BlockSpec((1, nt, DA_VAL_DIM, tq), lambda b, h, i: (b, 0, h, 0)),
            pl.BlockSpec(da_lambda.shape, lambda b, h, i: (0, 0)),
            pl.BlockSpec((DA_VAL_DIM, 1), lambda b, h, i: (0, 0)),
        ],
        out_specs=pl.BlockSpec((1, tq, LANE), lambda b, h, i: (b, i, h)),
        out_shape=jax.ShapeDtypeStruct((bsz, seq, DA_HEADS * DA_VAL_DIM), BF16),
        scratch_shapes=[
            pltpu.VMEM((1, 2 * tq), F32),
            pltpu.VMEM((DA_VAL_DIM + 2 * SUBLANE, 2 * tq), F32),
            pltpu.VMEM((tq, 2 * tq), BF16),
        ],
        compiler_params=_params("parallel", "parallel", "arbitrary"),
        name="diffattn",
    )(qt, proj, vt, da_lambda, subln_g_col)


def _log_sigmoid(x):
    return jnp.minimum(x, 0.0) - jnp.log(1.0 + jnp.exp(-jnp.abs(x)))


def _sigmoid(x):
    return 1.0 / (1.0 + jnp.exp(-x))


def _mlstm_body(q_ref, k_ref, vt_ref, ot_ref, gcol_ref, grow_ref, cw_ref, cb_ref, gbcol_ref, gbrow_ref, ng_ref,
                y_ref, xbuf, c_scr, m_scr):
    chunk = pl.program_id(1)
    L = q_ref.shape[1]
    width = q_ref.shape[2]
    dh = width // ML_HEADS
    tail = xbuf.shape[0]
    aug = c_scr.shape[1]

    @pl.when(chunk == 0)
    def _():
        xbuf[...] = jnp.zeros(xbuf.shape, xbuf.dtype)
        c_scr[...] = jnp.zeros(c_scr.shape, F32)
        m_scr[...] = jnp.zeros(m_scr.shape, F32)

    u = jnp.concatenate([q_ref[0], k_ref[0]], axis=1)
    u_ext = jnp.concatenate([xbuf[...], u], axis=0)
    src = lax.broadcasted_iota(jnp.int32, (L, L + tail), 1) - lax.broadcasted_iota(jnp.int32, (L, L + tail), 0)
    conv = cb_ref[...] + cw_ref[ML_CONV - 1:ML_CONV, :] * u.astype(F32)
    for back in range(1, ML_CONV):
        shift = jnp.where(src == tail - back, 1.0, 0.0).astype(BF16)
        conv = conv + cw_ref[ML_CONV - 1 - back:ML_CONV - back, :] * _dot(shift, u_ext)
    xbuf[...] = u[L - tail:, :]
    qk = conv * _sigmoid(conv)

    gc = gcol_ref[0] + gbcol_ref[...]
    gr = grow_ref[0] + gbrow_ref[...]
    row_id = lax.broadcasted_iota(jnp.int32, (L, L), 0)
    col_id = lax.broadcasted_iota(jnp.int32, (L, L), 1)
    causal = row_id <= col_id
    tri = jnp.where(col_id <= row_id, 1.0, 0.0).astype(BF16)
    tri_t = jnp.where(causal, 1.0, 0.0).astype(BF16)
    lfc_hi, lfc_lo = _split_bf16(_log_sigmoid(gc))
    lfr_hi, lfr_lo = _split_bf16(_log_sigmoid(gr))
    a_c = _dot(tri, lfc_hi) + _dot(tri, lfc_lo)
    a_r = _dot(lfr_hi, tri_t) + _dot(lfr_lo, tri_t)

    ones_rows = jnp.where(lax.broadcasted_iota(jnp.int32, (aug - dh, L), 0) == 0, 1.0, 0.0).astype(BF16)

    for h in range(ML_HEADS):
        fh = ML_HEADS + h
        m_prev = m_scr[h:h + 1, 0:1]
        a_row = a_r[fh:fh + 1, :]
        ig_row = gr[h:h + 1, :]
        nb = jnp.where(causal, gc[:, h:h + 1] - a_c[:, fh:fh + 1], NEG_INF)
        m_row = a_row + jnp.maximum(m_prev, jnp.max(nb, axis=0, keepdims=True))
        w_intra = jnp.exp(nb + (a_row - m_row))
        w_inter = jnp.exp(a_row + m_prev - m_row)

        qh = qk[:, h * dh:(h + 1) * dh].astype(BF16)
        kh = (qk[:, width + h * dh:width + (h + 1) * dh] * (dh ** -0.5)).astype(BF16)
        p_t = (_dot_nt(kh, qh) * w_intra).astype(BF16)
        v_aug = jnp.concatenate([vt_ref[0, 0, h * dh:(h + 1) * dh, :], ones_rows], axis=0)
        c_t = c_scr[h]
        num = w_inter * _dot_nt(c_t.astype(BF16), qh) + _dot(v_aug, p_t)
        scale = 1.0 / jnp.maximum(jnp.abs(num[dh:dh + 1, :]), jnp.exp(-m_row))
        hid = num[0:dh, :] * scale

        g_last = a_row[:, L - 1:L]
        log_w = g_last - a_row + ig_row
        m_new = jnp.maximum(g_last + m_prev, jnp.max(log_w, axis=1, keepdims=True))
        w_state = jnp.exp(log_w - m_new)
        decay = jnp.exp(g_last + m_prev - m_new)
        wv = (w_state * v_aug.astype(F32)).astype(BF16)
        c_scr[h] = decay * c_t + _dot(wv, kh)
        m_scr[h:h + 1, :] = jnp.broadcast_to(m_new, (1, m_scr.shape[1]))

        mu = jnp.mean(hid, axis=0, keepdims=True)
        cen = hid - mu
        var = jnp.mean(cen * cen, axis=0, keepdims=True)
        g_b = ng_ref[h * dh:(h + 1) * dh, :]
        hn = cen * lax.rsqrt(var + EPS) * jnp.concatenate([g_b] * (L // LANE), axis=1)
        gate = _sigmoid(ot_ref[0, 0, h * dh:(h + 1) * dh, :].astype(F32))
        y_ref[0, :, h * dh:(h + 1) * dh] = (gate * hn).T.astype(y_ref.dtype)


def _mlstm(proj, mvt, mot, gcol, grow, conv_w, conv_b, gb_col, gb_row, norm_g_b, col0):
    bsz, seq, _ = proj.shape
    width = norm_g_b.shape[0]
    dh = width // ML_HEADS
    L = min(ML_CHUNK, seq)
    tm = mvt.shape[-1]
    per = tm // L
    blk0 = col0 // width
    spec = lambda k: pl.BlockSpec((1, L, width), lambda b, c: (b, c, blk0 + k))
    tspec = pl.BlockSpec((1, 1, width, L), lambda b, c: (b, c // per, 0, c % per))
    return pl.pallas_call(
        _mlstm_body,
        grid=(bsz, seq // L),
        in_specs=[
            spec(0), spec(1), tspec, tspec,
            pl.BlockSpec((1, L, LANE), lambda b, c: (b, c, 0)),
            pl.BlockSpec((1, SUBLANE, L), lambda b, c: (b, 0, c)),
            pl.BlockSpec(conv_w.shape, lambda b, c: (0, 0)),
            pl.BlockSpec(conv_b.shape, lambda b, c: (0, 0)),
            pl.BlockSpec(gb_col.shape, lambda b, c: (0, 0)),
            pl.BlockSpec(gb_row.shape, lambda b, c: (0, 0)),
            pl.BlockSpec(norm_g_b.shape, lambda b, c: (0, 0)),
        ],
        out_specs=pl.BlockSpec((1, L, width), lambda b, c: (b, c, 0)),
        out_shape=jax.ShapeDtypeStruct((bsz, seq, width), BF16),
        scratch_shapes=[
            pltpu.VMEM((2 * SUBLANE, 2 * width), BF16),
            pltpu.VMEM((ML_HEADS, dh + 2 * SUBLANE, dh), F32),
            pltpu.VMEM((SUBLANE, LANE), F32),
        ],
        compiler_params=_params("parallel", "arbitrary"),
        name="mlstm",
    )(proj, proj, mvt, mot, gcol, grow, conv_w, conv_b, gb_col, gb_row, norm_g_b)


def _memkv_body(mem_ref, g_ref, wk_ref, wv_ref, k_ref, v_ref):
    mb = _rms(mem_ref[0], g_ref[...]).astype(BF16)
    k_ref[0] = _dot(mb, wk_ref[...]).astype(BF16)
    v_ref[0] = _dot(mb, wv_ref[...]).astype(BF16)


def _memkv(mem, g, wk, wv):
    bsz, mlen, d = mem.shape
    full = lambda a: pl.BlockSpec(a.shape, lambda b: (0,) * a.ndim)
    blk = pl.BlockSpec((1, mlen, d), lambda b: (b, 0, 0))
    return pl.pallas_call(
        _memkv_body,
        grid=(bsz,),
        in_specs=[blk, full(g), full(wk), full(wv)],
        out_specs=[blk, blk],
        out_shape=[jax.ShapeDtypeStruct((bsz, mlen, d), BF16)] * 2,
        compiler_params=_params("parallel"),
        name="memkv",
    )(mem, g, wk, wv)


def _mix_body(x_ref, yda_ref, yml_ref, wo1_ref, wo2_ref, gx_ref, wq_ref, km_ref, vm_ref, wxo_ref, gf_ref,
              wrh_ref, wrl_ref, br_ref, x2_ref, route_ref):
    x1 = x_ref[0] + _dot(yda_ref[0], wo1_ref[...]) + _dot(yml_ref[0], wo2_ref[...])

    d = x1.shape[-1]
    dh = d // X_HEADS
    q = _dot(_rms(x1, gx_ref[...]).astype(BF16), wq_ref[...]).astype(BF16)
    heads = []
    for h in range(X_HEADS):
        sl = slice(h * dh, (h + 1) * dh)
        s = _dot_nt(q[:, sl], km_ref[0, :, sl]) * (dh ** -0.5)
        p = jnp.exp(s - jnp.max(s, axis=-1, keepdims=True))
        o = _dot(p.astype(BF16), vm_ref[0, :, sl]) / jnp.sum(p, axis=-1, keepdims=True)
        heads.append(o.astype(BF16))
    x2 = x1 + _dot(jnp.concatenate(heads, axis=1), wxo_ref[...])
    _to_token_tiles(x2_ref.at[0], 0, x2)

    h3 = _rms(x2, gf_ref[...])
    h_hi, h_lo = _split_bf16(h3)
    logits = (_dot_nt(wrh_ref[...], h_hi) + _dot_nt(wrh_ref[...], h_lo) + _dot_nt(wrl_ref[...], h_hi)
              + br_ref[...])
    tm = logits.shape[1]
    gl = logits[0:N_GROUPS, :]
    g_iota = lax.broadcasted_iota(jnp.int32, gl.shape, 0)
    g_max = jnp.max(gl, axis=0, keepdims=True)
    g_idx = jnp.min(jnp.where(gl == g_max, g_iota, N_GROUPS), axis=0, keepdims=True)
    g_w = 1.0 / jnp.sum(jnp.exp(gl - g_max), axis=0, keepdims=True)
    el = jnp.zeros((EXPERTS_PER_GROUP, tm), F32)
    for g in range(N_GROUPS):
        lo = N_GROUPS + g * EXPERTS_PER_GROUP
        el = jnp.where(g_idx == g, logits[lo:lo + EXPERTS_PER_GROUP, :], el)
    e_iota = lax.broadcasted_iota(jnp.int32, el.shape, 0)
    e1 = jnp.max(el, axis=0, keepdims=True)
    i1 = jnp.min(jnp.where(el == e1, e_iota, EXPERTS_PER_GROUP), axis=0, keepdims=True)
    el2 = jnp.where(e_iota == i1, NEG_INF, el)
    e2 = jnp.max(el2, axis=0, keepdims=True)
    i2 = jnp.min(jnp.where(el2 == e2, e_iota, EXPERTS_PER_GROUP), axis=0, keepdims=True)
    p2 = jnp.exp(e2 - e1)
    w1 = g_w / (1.0 + p2)
    w2 = g_w * p2 / (1.0 + p2)
    base = g_idx * EXPERTS_PER_GROUP
    zero = jnp.zeros((1, tm), F32)
    route_ref[0] = jnp.concatenate(
        [(base + i1).astype(F32), (base + i2).astype(F32), w1, w2, zero, zero, zero, zero], axis=0)


def _mix(x, yda, yml, wo1, wo2, gx, wq, kmem, vmem, wxo, gf, wr_hi, wr_lo, br):
    bsz, seq, d = x.shape
    tm = min(ROW_TILE, seq)
    mlen = kmem.shape[1]
    full = lambda a: pl.BlockSpec(a.shape, lambda b, i: (0,) * a.ndim)
    tok = lambda w: pl.BlockSpec((1, tm, w), lambda b, i: (b, i, 0))
    memspec = pl.BlockSpec((1, mlen, d), lambda b, i: (b, 0, 0))
    return pl.pallas_call(
        _mix_body,
        grid=(bsz, seq // tm),
        in_specs=[tok(d), tok(yda.shape[-1]), tok(yml.shape[-1]), full(wo1), full(wo2), full(gx), full(wq),
                  memspec, memspec, full(wxo), full(gf), full(wr_hi), full(wr_lo), full(br)],
        out_specs=[pl.BlockSpec((1, tm * SUBLANE, LANE), lambda b, i: (b, i, 0)),
                   pl.BlockSpec((1, SUBLANE, tm), lambda b, i: (b, 0, i))],
        out_shape=[
            jax.ShapeDtypeStruct((bsz, seq * d // LANE, LANE), F32),
            jax.ShapeDtypeStruct((bsz, SUBLANE, seq), F32),
        ],
        compiler_params=_params("parallel", "parallel"),
        name="mix_xattn_router",
    )(x, yda, yml, wo1, wo2, gx, wq, kmem, vmem, wxo, gf, wr_hi, wr_lo, br)


def _to_token_tiles(ref, first_row, dense):
    n, d = dense.shape
    for c in range(d // LANE):
        ref[pl.ds(first_row * SUBLANE + c, n, stride=SUBLANE), :] = dense[:, c * LANE:(c + 1) * LANE]


def _from_token_tiles(ref, first_row, n):
    return jnp.concatenate(
        [ref[pl.ds(first_row * SUBLANE + c, n, stride=SUBLANE), :] for c in range(SUBLANE)], axis=1)


def _moe_body(start_ref, cnt_ref, x_hbm, tok_hbm, wt_hbm, gf_ref, gfin_ref, w1_ref, w3_ref, w2_ref, out_hbm,
              acc, h3, xg, yt, stage, tok_s, wt_s, sems, load_sems, store_sems, *, final_norm):
    tile = pl.program_id(0)
    e = pl.program_id(1)
    tt = h3.shape[0] // SUBLANE
    chunk_rows = stage.shape[1]
    group = SUBLANE

    def row_tile(ref, row0):
        return ref.at[pl.ds(pl.multiple_of(row0, SUBLANE), SUBLANE), :]

    def token_tile(ref, r):
        return ref.at[pl.ds(r * SUBLANE, SUBLANE), :]

    spare_row = tt * SUBLANE

    n_chunks = tt // chunk_rows
    chunk_tiles = chunk_rows * SUBLANE

    def load_chunk(i):
        rows = pl.ds(pl.multiple_of(i * chunk_tiles, chunk_tiles), chunk_tiles)
        return pltpu.make_async_copy(x_hbm.at[tile, rows, :], acc.at[rows, :], load_sems.at[i])

    @pl.when(e == 0)
    def _():
        lists = [pltpu.make_async_copy(tok_hbm.at[tile], tok_s, sems.at[0]),
                 pltpu.make_async_copy(wt_hbm.at[tile], wt_s, sems.at[1])]
        for cp in lists:
            cp.start()
        for i in range(n_chunks):
            load_chunk(i).start()
        token_tile(acc, tt)[...] = jnp.zeros((SUBLANE, LANE), F32)

        def ffn_norm(i, c):
            load_chunk(i).wait()
            r0 = i * chunk_rows
            _to_token_tiles(h3, r0, _rms(_from_token_tiles(acc, r0, chunk_rows), gf_ref[...]))
            return c
        lax.fori_loop(0, n_chunks, ffn_norm, 0)
        for cp in lists:
            cp.wait()

    start = start_ref[tile * N_EXPERTS + e]
    count = cnt_ref[tile * N_EXPERTS + e]

    def run_block(first, cnt, rows):
        idx0 = start + first
        for r in range(rows):
            token_tile(xg, r)[...] = row_tile(h3, tok_s[idx0 + r])[...]

        xb = _from_token_tiles(xg, 0, rows).astype(BF16)
        a = _dot(xb, w1_ref[0])
        b = _dot(xb, w3_ref[0])
        hmid = (a * _sigmoid(a) * b).astype(BF16)
        _to_token_tiles(yt, 0, _dot(hmid, w2_ref[0]))

        for g0 in range(0, rows, group):
            dst = [jnp.where(g0 + u < cnt, tok_s[idx0 + g0 + u], spare_row) for u in range(group)]
            new = [row_tile(acc, dst[u])[...] + wt_s[idx0 + g0 + u] * token_tile(yt, g0 + u)[...]
                   for u in range(group)]
            for u in range(group):
                row_tile(acc, dst[u])[...] = new[u]

    big = xg.shape[0] // SUBLANE
    small = big // 2
    n_small = (count + small - 1) // small
    n_big = n_small // 2

    def big_block(i, c):
        run_block(i * big, jnp.minimum(big, count - i * big), big)
        return c

    lax.fori_loop(0, n_big, big_block, 0)

    @pl.when(n_small % 2 == 1)
    def _():
        run_block(n_big * big, count - n_big * big, small)

    @pl.when(e == pl.num_programs(1) - 1)
    def _():
        n_slots = stage.shape[0]

        def store_chunk(i, slot):
            r0 = pl.multiple_of(i * chunk_rows, chunk_rows)
            return pltpu.make_async_copy(stage.at[slot], out_hbm.at[tile, pl.ds(r0, chunk_rows), :],
                                         store_sems.at[slot])

        def write_back(p, c):
            for slot in range(n_slots):
                i = p * n_slots + slot

                @pl.when(p > 0)
                def _():
                    store_chunk(i - n_slots, slot).wait()

                rows = _from_token_tiles(acc, i * chunk_rows, chunk_rows)
                stage[slot] = _rms(rows, gfin_ref[...]) if final_norm else rows
                store_chunk(i, slot).start()
            return c
        lax.fori_loop(0, n_chunks // n_slots, write_back, 0)
        for slot in range(n_slots):
            store_chunk(n_chunks - n_slots + slot, slot).wait()


def _moe(start, cnt, tok, wt, x_token_tiles, gf, gfin, w1, w3, w2, final_norm):
    ntiles, tile_rows, _ = x_token_tiles.shape
    tt = tile_rows // SUBLANE
    d = SUBLANE * LANE
    de = w1.shape[-1]
    seg = tok.shape[1]
    body = functools.partial(_moe_body, final_norm=final_norm)
    hbm = pl.BlockSpec(memory_space=pl.ANY)
    grid_spec = pltpu.PrefetchScalarGridSpec(
        num_scalar_prefetch=2,
        grid=(ntiles, N_EXPERTS),
        in_specs=[
            hbm, hbm, hbm,
            pl.BlockSpec((1, d), lambda t, e, *_: (0, 0)),
            pl.BlockSpec((1, d), lambda t, e, *_: (0, 0)),
            pl.BlockSpec((1, d, de), lambda t, e, *_: (e, 0, 0)),
            pl.BlockSpec((1, d, de), lambda t, e, *_: (e, 0, 0)),
            pl.BlockSpec((1, de, d), lambda t, e, *_: (e, 0, 0)),
        ],
        out_specs=hbm,
        scratch_shapes=[
            pltpu.VMEM((tile_rows + SUBLANE, LANE), F32),
            pltpu.VMEM((tile_rows, LANE), F32),
            pltpu.VMEM((2 * MOE_ROWS * SUBLANE, LANE), F32),
            pltpu.VMEM((2 * MOE_ROWS * SUBLANE, LANE), F32),
            pltpu.VMEM((2, 2 * MOE_ROWS, d), F32),
            pltpu.SMEM((seg,), jnp.int32),
            pltpu.SMEM((seg,), F32),
            pltpu.SemaphoreType.DMA((2,)),
            pltpu.SemaphoreType.DMA((tt // (2 * MOE_ROWS),)),
            pltpu.SemaphoreType.DMA((2,)),
        ],
    )
    return pl.pallas_call(
        body,
        grid_spec=grid_spec,
        out_shape=jax.ShapeDtypeStruct((ntiles, tt, d), F32),
        compiler_params=_params("arbitrary", "arbitrary"),
        name="moe",
    )(start, cnt, x_token_tiles, tok, wt, gf, gfin, w1, w3, w2)


def _rope_tables(seq):
    half = DA_HEAD_DIM // 2
    inv_freq = ROPE_THETA ** (-jnp.arange(0, DA_HEAD_DIM, 2, dtype=F32) / DA_HEAD_DIM)
    ang = jnp.arange(seq, dtype=F32)[:, None] * inv_freq[None, :]
    cos, sin = jnp.cos(ang), jnp.sin(ang)
    reps = LANE // half
    cos_t = jnp.tile(cos, (1, reps))
    sin_t = jnp.tile(jnp.concatenate([-sin, sin], axis=1), (1, reps // 2))
    return cos_t, sin_t


def _sort_assignments(route, tt):
    bsz, _, seq = route.shape
    per = seq // tt
    ntiles = bsz * per
    tile = lambda a: a.reshape(bsz, TOP_K_ROWS, per, tt).transpose(0, 2, 1, 3).reshape(ntiles, TOP_K_ROWS * tt)
    keys = tile(route[:, 0:TOP_K_ROWS, :]).astype(jnp.int32)
    wts = tile(route[:, TOP_K_ROWS:2 * TOP_K_ROWS, :])
    order = jnp.argsort(keys, axis=-1).astype(jnp.int32)
    tok = (order % tt) * SUBLANE
    wt = jnp.take_along_axis(wts, order, axis=-1)
    counts = jnp.sum(keys[:, :, None] == jnp.arange(N_EXPERTS, dtype=jnp.int32)[None, None, :], axis=1,
                     dtype=jnp.int32)
    start = jnp.cumsum(counts, axis=-1, dtype=jnp.int32) - counts
    over = ((0, 0), (0, 2 * MOE_ROWS))
    return start.reshape(-1), counts.reshape(-1), jnp.pad(tok, over), jnp.pad(wt, over)


TOP_K_ROWS = 2


def kernel(x, mem, norm_mix_g, w_in, w_out, da_lambda, da_subln_g, ml_conv_w, ml_conv_b, ml_gate_b, ml_norm_g, norm_x_g, norm_mem_g, w_xq, w_xk, w_xv, w_xo, norm_ffn_g, w_router_group, b_router_group, w_router_expert, b_router_expert, w1, w3, w2, norm_final_g):
    depth = w_in.shape[0]
    bsz, seq, d = x.shape
    da_qk = DA_HEADS * 2 * DA_HEAD_DIM
    da_width = DA_HEADS * DA_VAL_DIM
    ml_width = ml_norm_g.shape[-1]
    n_main = 2 * da_qk + da_width + 4 * ml_width
    n_in = w_in.shape[-1]
    cos_t, sin_t = _rope_tables(seq)
    row = lambda v: v.reshape(1, -1).astype(F32)

    for l in range(depth):
        lam_init = 0.8 - 0.6 * math.exp(-0.3 * l)
        w_pad = jnp.pad(w_in[l], ((0, 0), (0, n_main + LANE - n_in))).astype(BF16)
        qt, vt, mvt, mot, proj, gcol, grow = _inproj(x, row(norm_mix_g[l]), w_pad, cos_t, sin_t)

        yda = _diffattn(qt, proj, vt, da_lambda[l].astype(F32), da_subln_g[l].astype(F32).reshape(-1, 1), lam_init)

        gb = ml_gate_b[l].astype(F32).reshape(1, 2 * ML_HEADS)
        gb_col = jnp.pad(gb, ((0, 0), (0, LANE - 2 * ML_HEADS)))
        gb_row = gb.reshape(2 * ML_HEADS, 1)
        ng_b = jnp.broadcast_to(ml_norm_g[l].astype(F32)[:, None], (ml_width, LANE))
        yml = _mlstm(proj, mvt, mot, gcol, grow, ml_conv_w[l].reshape(ML_CONV, 2 * ml_width).astype(F32),
                     row(ml_conv_b[l]), gb_col, gb_row, ng_b, da_qk)

        kmem, vmem = _memkv(mem, row(norm_mem_g[l]), w_xk[l].astype(BF16), w_xv[l].astype(BF16))

        wr = jnp.concatenate([w_router_group[l], w_router_expert[l]], axis=1).astype(F32).T
        wr = jnp.pad(wr, ((0, ROUTER_ROWS - wr.shape[0]), (0, 0)))
        wr_hi = wr.astype(BF16)
        wr_lo = (wr - wr_hi.astype(F32)).astype(BF16)
        br = jnp.pad(jnp.concatenate([b_router_group[l], b_router_expert[l]]).astype(F32),
                     (0, ROUTER_ROWS - N_GROUPS - N_EXPERTS)).reshape(ROUTER_ROWS, 1)
        wo = w_out[l].astype(BF16)
        x2, route = _mix(x, yda, yml, wo[:da_width], wo[da_width:], row(norm_x_g[l]), w_xq[l].astype(BF16),
                         kmem, vmem, w_xo[l].astype(BF16), row(norm_ffn_g[l]), wr_hi, wr_lo, br)

        tt = min(MOE_TILE, seq)
        x = _moe(*_sort_assignments(route, tt), x2.reshape(bsz * seq // tt, tt * d // LANE, LANE),
                 row(norm_ffn_g[l]), row(norm_final_g),
                 w1[l].astype(BF16), w3[l].astype(BF16), w2[l].astype(BF16),
                 final_norm=(l + 1 == depth)).reshape(bsz, seq, d)
    return x
```

```python
import functools
import math

import jax
import jax.numpy as jnp
from jax import lax
from jax.experimental import pallas as pl
from jax.experimental.pallas import tpu as pltpu

F32 = jnp.float32
BF16 = jnp.bfloat16

LANE = 128
SUBLANE = 8
VMEM_LIMIT_BYTES = 56 * 1024 * 1024

EPS = 1e-6
ROPE_THETA = 10000.0
DA_HEADS = 4
DA_HEAD_DIM = 64
DA_VAL_DIM = 2 * DA_HEAD_DIM
ML_HEADS = 4
ML_CONV = 4
X_HEADS = 4
N_GROUPS = 4
EXPERTS_PER_GROUP = 8
N_EXPERTS = N_GROUPS * EXPERTS_PER_GROUP
ROUTER_ROWS = 40

ROW_TILE = 512
DA_HEADS_PER_STEP = 4
ML_CHUNK = 256
MOE_TILE = 4096
MOE_ROWS = 128

NEG_INF = float("-inf")
LOG2_E = math.log2(math.e)


def _rms(x, g):
    return x * lax.rsqrt(jnp.mean(x * x, axis=-1, keepdims=True) + EPS) * g


def _dot(a, b):
    return jnp.dot(a, b, preferred_element_type=F32)


def _dot_nt(a, b):
    return lax.dot_general(a, b, (((1,), (1,)), ((), ())), preferred_element_type=F32)


def _dot_tn(a, b):
    return lax.dot_general(a, b, (((0,), (0,)), ((), ())), preferred_element_type=F32)


def _split_bf16(x):
    hi = x.astype(BF16)
    lo = (x - hi.astype(F32)).astype(BF16)
    return hi, lo


def _params(*semantics):
    return pltpu.CompilerParams(dimension_semantics=semantics, vmem_limit_bytes=VMEM_LIMIT_BYTES)


def _inproj_body(x_ref, g_ref, w_ref, cos_ref, sin_ref, qt_ref, vt_ref, mvt_ref, mot_ref, proj_ref, gcol_ref,
                 grow_ref):
    hb = _rms(x_ref[0], g_ref[...]).astype(BF16)
    cos = cos_ref[...]
    sin = sin_ref[...]
    lane = lax.broadcasted_iota(jnp.int32, cos.shape, 1)
    first_half = (lane % DA_HEAD_DIM) < (DA_HEAD_DIM // 2)
    n_main = w_ref.shape[1] - LANE
    wide = 4 * LANE
    for c in range(n_main // LANE):
        if c % (wide // LANE) == 0:
            acc = _dot(hb, w_ref[:, c * LANE:c * LANE + wide])
        t = acc[:, (c * LANE) % wide:(c * LANE) % wide + LANE]
        if c < 2 * DA_HEADS:
            rot = jnp.where(first_half, pltpu.roll(t, LANE - DA_HEAD_DIM // 2, 1),
                            pltpu.roll(t, DA_HEAD_DIM // 2, 1))
            t = t * cos + rot * sin
        if c < DA_HEADS:
            qt_ref[0, 0, c * LANE:(c + 1) * LANE, :] = (t * (DA_HEAD_DIM ** -0.5 * LOG2_E)).T.astype(BF16)
        elif c < 2 * DA_HEADS:
            proj_ref[0, :, (c - DA_HEADS) * LANE:(c - DA_HEADS + 1) * LANE] = t.astype(BF16)
        elif c < 3 * DA_HEADS:
            vt_ref[0, 0, (c - 2 * DA_HEADS) * LANE:(c - 2 * DA_HEADS + 1) * LANE, :] = t.T.astype(BF16)
        elif c < 3 * DA_HEADS + 2 * ML_HEADS:
            proj_ref[0, :, (c - 2 * DA_HEADS) * LANE:(c - 2 * DA_HEADS + 1) * LANE] = t.astype(BF16)
        elif c < 3 * DA_HEADS + 3 * ML_HEADS:
            m = c - 3 * DA_HEADS - 2 * ML_HEADS
            mvt_ref[0, 0, m * LANE:(m + 1) * LANE, :] = t.T.astype(BF16)
        else:
            m = c - 3 * DA_HEADS - 3 * ML_HEADS
            mot_ref[0, 0, m * LANE:(m + 1) * LANE, :] = t.T.astype(BF16)
    gates = _dot(hb, w_ref[:, n_main:n_main + LANE])
    gcol_ref[0] = gates
    grow_ref[0] = gates.T[:SUBLANE, :]


def _inproj(x, g, w_pad, cos_t, sin_t):
    bsz, seq, d = x.shape
    tm = min(ROW_TILE, seq)
    nt = seq // tm
    da_w = DA_HEADS * DA_VAL_DIM
    ml_w = ML_HEADS * LANE
    n_proj = w_pad.shape[1] - LANE - 2 * da_w - 2 * ml_w
    return pl.pallas_call(
        _inproj_body,
        grid=(bsz, nt),
        in_specs=[
            pl.BlockSpec((1, tm, d), lambda b, i: (b, i, 0)),
            pl.BlockSpec((1, d), lambda b, i: (0, 0)),
            pl.BlockSpec(w_pad.shape, lambda b, i: (0, 0)),
            pl.BlockSpec((tm, LANE), lambda b, i: (i, 0)),
            pl.BlockSpec((tm, LANE), lambda b, i: (i, 0)),
        ],
        out_specs=[
            pl.BlockSpec((1, 1, da_w, tm), lambda b, i: (b, i, 0, 0)),
            pl.BlockSpec((1, 1, da_w, tm), lambda b, i: (b, i, 0, 0)),
            pl.BlockSpec((1, 1, ml_w, tm), lambda b, i: (b, i, 0, 0)),
            pl.BlockSpec((1, 1, ml_w, tm), lambda b, i: (b, i, 0, 0)),
            pl.BlockSpec((1, tm, n_proj), lambda b, i: (b, i, 0)),
            pl.BlockSpec((1, tm, LANE), lambda b, i: (b, i, 0)),
            pl.BlockSpec((1, SUBLANE, tm), lambda b, i: (b, 0, i)),
        ],
        out_shape=[
            jax.ShapeDtypeStruct((bsz, nt, da_w, tm), BF16),
            jax.ShapeDtypeStruct((bsz, nt, da_w, tm), BF16),
            jax.ShapeDtypeStruct((bsz, nt, ml_w, tm), BF16),
            jax.ShapeDtypeStruct((bsz, nt, ml_w, tm), BF16),
            jax.ShapeDtypeStruct((bsz, seq, n_proj), BF16),
            jax.ShapeDtypeStruct((bsz, seq, LANE), F32),
            jax.ShapeDtypeStruct((bsz, SUBLANE, seq), F32),
        ],
        compiler_params=_params("parallel", "parallel"),
        name="inproj",
    )(x, g, w_pad, cos_t, sin_t)


def _diffattn_body(qt_ref, k_ref, vt_ref, lam_ref, g_ref, o_ref, m_scr, acc_scr, p_scr, *, lam_init):
    qi = pl.program_id(2)
    tq = qt_ref.shape[3]
    heads = qt_ref.shape[2] // DA_VAL_DIM
    q2t = []
    for g in range(heads):
        qt = qt_ref[0, 0, g * DA_VAL_DIM:(g + 1) * DA_VAL_DIM, :]
        row = lax.broadcasted_iota(jnp.int32, qt.shape, 0)
        zero = jnp.zeros_like(qt)
        q2t.append(jnp.concatenate([jnp.where(row < DA_HEAD_DIM, qt, zero),
                                    jnp.where(row >= DA_HEAD_DIM, qt, zero)], axis=1))
    pad_rows = acc_scr.shape[0] - DA_VAL_DIM
    ones_rows = jnp.where(lax.broadcasted_iota(jnp.int32, (pad_rows, tq), 0) == 0, 1.0, 0.0).astype(BF16)

    m_scr[...] = jnp.full(m_scr.shape, NEG_INF, F32)
    acc_scr[...] = jnp.zeros(acc_scr.shape, F32)

    n_chains = 4
    width = 2 * tq // n_chains
    chains = tuple((g, slice(c * width, (c + 1) * width),
                    slice(g * 2 * tq + c * width, g * 2 * tq + (c + 1) * width))
                   for c in range(n_chains) for g in range(heads))

    def scores(j, masked, prev_values):
        out = []
        for g, local, cols in chains:
            k = k_ref[0, pl.ds(pl.multiple_of(j * tq, tq), tq), g * LANE:(g + 1) * LANE]
            s = _dot(k, q2t[g][:, local])
            pv = values(j - 1, g, cols) if prev_values else None
            if masked:
                key = lax.broadcasted_iota(jnp.int32, s.shape, 0)
                qry = lax.broadcasted_iota(jnp.int32, s.shape, 1) + local.start % tq
                s = jnp.where(key <= qry, s, NEG_INF)
            m_prev = m_scr[:, cols]
            m_new = jnp.maximum(m_prev, jnp.max(s, axis=0, keepdims=True))
            p_scr[:, cols] = jnp.exp2(s - m_new).astype(BF16)
            m_scr[:, cols] = m_new
            out.append((jnp.exp2(m_prev - m_new), pv))
        return out

    def values(j, g, cols):
        v_aug = jnp.concatenate([vt_ref[0, j, g * DA_VAL_DIM:(g + 1) * DA_VAL_DIM, :], ones_rows], axis=0)
        return _dot(v_aug, p_scr[:, cols])

    def step(j, masked):
        for (_, _, cols), (alpha, pv) in zip(chains, scores(j, masked, True)):
            acc_scr[:, cols] = alpha * (acc_scr[:, cols] + pv)

    def full_step_pair(i, carry):
        step(1 + 2 * i, False)
        step(2 + 2 * i, False)
        return carry

    @pl.when(qi == 0)
    def _():
        scores(0, True, False)

    @pl.when(qi > 0)
    def _():
        scores(0, False, False)
        lax.fori_loop(0, (qi - 1) // 2, full_step_pair, 0)

        @pl.when((qi - 1) % 2 == 1)
        def _():
            step(qi - 1, False)

        step(qi, True)

    for g, _, cols in chains:
        acc_scr[:, cols] = acc_scr[:, cols] + values(qi, g, cols)

    lv = lam_ref[...]
    lam = (jnp.exp(jnp.sum(lv[0:1, :] * lv[1:2, :], axis=-1, keepdims=True))
           - jnp.exp(jnp.sum(lv[2:3, :] * lv[3:4, :], axis=-1, keepdims=True)) + lam_init)
    for g in range(heads):
        acc = acc_scr[:, g * 2 * tq:(g + 1) * 2 * tq]
        o = acc[0:DA_VAL_DIM, :] / acc[DA_VAL_DIM:DA_VAL_DIM + 1, :]
        o = o[:, :tq] - lam * o[:, tq:]
        y = o * lax.rsqrt(jnp.mean(o * o, axis=0, keepdims=True) + EPS) * g_ref[...] * (1.0 - lam_init)
        o_ref[0, :, g * LANE:(g + 1) * LANE] = y.T.astype(o_ref.dtype)


def _diffattn(qt, proj, vt, da_lambda, subln_g_col, lam_init):
    bsz, nt, _, tq = qt.shape
    seq = nt * tq
    hp = DA_HEADS_PER_STEP
    body = functools.partial(_diffattn_body, lam_init=lam_init)
    return pl.pallas_call(
        body,
        grid=(bsz, DA_HEADS // hp, nt),
        in_specs=[
            pl.BlockSpec((1, 1, hp * DA_VAL_DIM, tq), lambda b, h, i: (b, i, h, 0)),
            pl.BlockSpec((1, seq, hp * LANE), lambda b, h, i: (b, 0, h)),
            pl.BlockSpec((1, nt, hp * DA_VAL_DIM, tq), lambda b, h, i: (b, 0, h, 0)),
            pl.BlockSpec(da_lambda.shape, lambda b, h, i: (0, 0)),
            pl.BlockSpec((DA_VAL_DIM, 1), lambda b, h, i: (0, 0)),
        ],
        out_specs=pl.BlockSpec((1, tq, hp * LANE), lambda b, h, i: (b, i, h)),
        out_shape=jax.ShapeDtypeStruct((bsz, seq, DA_HEADS * DA_VAL_DIM), BF16),
        scratch_shapes=[
            pltpu.VMEM((1, hp * 2 * tq), F32),
            pltpu.VMEM((DA_VAL_DIM + 2 * SUBLANE, hp * 2 * tq), F32),
            pltpu.VMEM((tq, hp * 2 * tq), BF16),
        ],
        compiler_params=_params("parallel", "parallel", "arbitrary"),
        name="diffattn",
    )(qt, proj, vt, da_lambda, subln_g_col)


def _log_sigmoid(x):
    return jnp.minimum(x, 0.0) - jnp.log(1.0 + jnp.exp(-jnp.abs(x)))


def _sigmoid(x):
    return 1.0 / (1.0 + jnp.exp(-x))


def _mlstm_body(q_ref, k_ref, vt_ref, ot_ref, gcol_ref, grow_ref, cw_ref, cb_ref, gbcol_ref, gbrow_ref, ng_ref,
                y_ref, xbuf, c_scr, m_scr):
    chunk = pl.program_id(1)
    L = q_ref.shape[1]
    width = q_ref.shape[2]
    dh = width // ML_HEADS
    tail = xbuf.shape[0]
    aug = c_scr.shape[1]

    @pl.when(chunk == 0)
    def _():
        xbuf[...] = jnp.zeros(xbuf.shape, xbuf.dtype)
        c_scr[...] = jnp.zeros(c_scr.shape, F32)
        m_scr[...] = jnp.zeros(m_scr.shape, F32)

    u = jnp.concatenate([q_ref[0], k_ref[0]], axis=1)
    u_ext = jnp.concatenate([xbuf[...], u], axis=0)
    src = lax.broadcasted_iota(jnp.int32, (L, L + tail), 1) - lax.broadcasted_iota(jnp.int32, (L, L + tail), 0)
    conv = cb_ref[...] + cw_ref[ML_CONV - 1:ML_CONV, :] * u.astype(F32)
    for back in range(1, ML_CONV):
        shift = jnp.where(src == tail - back, 1.0, 0.0).astype(BF16)
        conv = conv + cw_ref[ML_CONV - 1 - back:ML_CONV - back, :] * _dot(shift, u_ext)
    xbuf[...] = u[L - tail:, :]
    qk = conv * _sigmoid(conv)

    gc = gcol_ref[0] + gbcol_ref[...]
    gr = grow_ref[0] + gbrow_ref[...]
    row_id = lax.broadcasted_iota(jnp.int32, (L, L), 0)
    col_id = lax.broadcasted_iota(jnp.int32, (L, L), 1)
    causal = row_id <= col_id
    tri = jnp.where(col_id <= row_id, 1.0, 0.0).astype(BF16)
    tri_t = jnp.where(causal, 1.0, 0.0).astype(BF16)
    lfc_hi, lfc_lo = _split_bf16(_log_sigmoid(gc))
    lfr_hi, lfr_lo = _split_bf16(_log_sigmoid(gr))
    a_c = _dot(tri, lfc_hi) + _dot(tri, lfc_lo)
    a_r = _dot(lfr_hi, tri_t) + _dot(lfr_lo, tri_t)

    ones_rows = jnp.where(lax.broadcasted_iota(jnp.int32, (aug - dh, L), 0) == 0, 1.0, 0.0).astype(BF16)

    for h in range(ML_HEADS):
        fh = ML_HEADS + h
        m_prev = m_scr[h:h + 1, 0:1]
        a_row = a_r[fh:fh + 1, :]
        ig_row = gr[h:h + 1, :]
        nb = jnp.where(causal, gc[:, h:h + 1] - a_c[:, fh:fh + 1], NEG_INF)
        m_row = a_row + jnp.maximum(m_prev, jnp.max(nb, axis=0, keepdims=True))
        w_intra = jnp.exp(nb + (a_row - m_row))
        w_inter = jnp.exp(a_row + m_prev - m_row)

        qh = qk[:, h * dh:(h + 1) * dh].astype(BF16)
        kh = (qk[:, width + h * dh:width + (h + 1) * dh] * (dh ** -0.5)).astype(BF16)
        p_t = (_dot_nt(kh, qh) * w_intra).astype(BF16)
        v_aug = jnp.concatenate([vt_ref[0, 0, h * dh:(h + 1) * dh, :], ones_rows], axis=0)
        c_t = c_scr[h]
        num = w_inter * _dot_nt(c_t.astype(BF16), qh) + _dot(v_aug, p_t)
        scale = 1.0 / jnp.maximum(jnp.abs(num[dh:dh + 1, :]), jnp.exp(-m_row))
        hid = num[0:dh, :] * scale

        g_last = a_row[:, L - 1:L]
        log_w = g_last - a_row + ig_row
        m_new = jnp.maximum(g_last + m_prev, jnp.max(log_w, axis=1, keepdims=True))
        w_state = jnp.exp(log_w - m_new)
        decay = jnp.exp(g_last + m_prev - m_new)
        wv = (w_state * v_aug.astype(F32)).astype(BF16)
        c_scr[h] = decay * c_t + _dot(wv, kh)
        m_scr[h:h + 1, :] = jnp.broadcast_to(m_new, (1, m_scr.shape[1]))

        mu = jnp.mean(hid, axis=0, keepdims=True)
        cen = hid - mu
        var = jnp.mean(cen * cen, axis=0, keepdims=True)
        g_b = ng_ref[h * dh:(h + 1) * dh, :]
        hn = cen * lax.rsqrt(var + EPS) * jnp.concatenate([g_b] * (L // LANE), axis=1)
        gate = _sigmoid(ot_ref[0, 0, h * dh:(h + 1) * dh, :].astype(F32))
        y_ref[0, :, h * dh:(h + 1) * dh] = (gate * hn).T.astype(y_ref.dtype)


def _mlstm(proj, mvt, mot, gcol, grow, conv_w, conv_b, gb_col, gb_row, norm_g_b, col0):
    bsz, seq, _ = proj.shape
    width = norm_g_b.shape[0]
    dh = width // ML_HEADS
    L = min(ML_CHUNK, seq)
    tm = mvt.shape[-1]
    per = tm // L
    blk0 = col0 // width
    spec = lambda k: pl.BlockSpec((1, L, width), lambda b, c: (b, c, blk0 + k))
    tspec = pl.BlockSpec((1, 1, width, L), lambda b, c: (b, c // per, 0, c % per))
    return pl.pallas_call(
        _mlstm_body,
        grid=(bsz, seq // L),
        in_specs=[
            spec(0), spec(1), tspec, tspec,
            pl.BlockSpec((1, L, LANE), lambda b, c: (b, c, 0)),
            pl.BlockSpec((1, SUBLANE, L), lambda b, c: (b, 0, c)),
            pl.BlockSpec(conv_w.shape, lambda b, c: (0, 0)),
            pl.BlockSpec(conv_b.shape, lambda b, c: (0, 0)),
            pl.BlockSpec(gb_col.shape, lambda b, c: (0, 0)),
            pl.BlockSpec(gb_row.shape, lambda b, c: (0, 0)),
            pl.BlockSpec(norm_g_b.shape, lambda b, c: (0, 0)),
        ],
        out_specs=pl.BlockSpec((1, L, width), lambda b, c: (b, c, 0)),
        out_shape=jax.ShapeDtypeStruct((bsz, seq, width), BF16),
        scratch_shapes=[
            pltpu.VMEM((2 * SUBLANE, 2 * width), BF16),
            pltpu.VMEM((ML_HEADS, dh + 2 * SUBLANE, dh), F32),
            pltpu.VMEM((SUBLANE, LANE), F32),
        ],
        compiler_params=_params("parallel", "arbitrary"),
        name="mlstm",
    )(proj, proj, mvt, mot, gcol, grow, conv_w, conv_b, gb_col, gb_row, norm_g_b)


def _memkv_body(mem_ref, g_ref, wk_ref, wv_ref, k_ref, v_ref):
    mb = _rms(mem_ref[0], g_ref[...]).astype(BF16)
    k_ref[0] = _dot(mb, wk_ref[...]).astype(BF16)
    v_ref[0] = _dot(mb, wv_ref[...]).astype(BF16)


def _memkv(mem, g, wk, wv):
    bsz, mlen, d = mem.shape
    full = lambda a: pl.BlockSpec(a.shape, lambda b: (0,) * a.ndim)
    blk = pl.BlockSpec((1, mlen, d), lambda b: (b, 0, 0))
    return pl.pallas_call(
        _memkv_body,
        grid=(bsz,),
        in_specs=[blk, full(g), full(wk), full(wv)],
        out_specs=[blk, blk],
        out_shape=[jax.ShapeDtypeStruct((bsz, mlen, d), BF16)] * 2,
        compiler_params=_params("parallel"),
        name="memkv",
    )(mem, g, wk, wv)


def _mix_body(x_ref, yda_ref, yml_ref, wo1_ref, wo2_ref, gx_ref, wq_ref, km_ref, vm_ref, wxo_ref, gf_ref,
              wrh_ref, wrl_ref, br_ref, x2_ref, route_ref):
    x1 = x_ref[0] + _dot(yda_ref[0], wo1_ref[...]) + _dot(yml_ref[0], wo2_ref[...])

    d = x1.shape[-1]
    dh = d // X_HEADS
    q = _dot(_rms(x1, gx_ref[...]).astype(BF16), wq_ref[...]).astype(BF16)
    heads = []
    for h in range(X_HEADS):
        sl = slice(h * dh, (h + 1) * dh)
        s = _dot_nt(q[:, sl], km_ref[0, :, sl]) * (dh ** -0.5)
        p = jnp.exp(s - jnp.max(s, axis=-1, keepdims=True))
        o = _dot(p.astype(BF16), vm_ref[0, :, sl]) / jnp.sum(p, axis=-1, keepdims=True)
        heads.append(o.astype(BF16))
    x2 = x1 + _dot(jnp.concatenate(heads, axis=1), wxo_ref[...])
    _to_token_tiles(x2_ref.at[0], 0, x2)

    h3 = _rms(x2, gf_ref[...])
    h_hi, h_lo = _split_bf16(h3)
    logits = (_dot_nt(wrh_ref[...], h_hi) + _dot_nt(wrh_ref[...], h_lo) + _dot_nt(wrl_ref[...], h_hi)
              + br_ref[...])
    tm = logits.shape[1]
    gl = logits[0:N_GROUPS, :]
    g_iota = lax.broadcasted_iota(jnp.int32, gl.shape, 0)
    g_max = jnp.max(gl, axis=0, keepdims=True)
    g_idx = jnp.min(jnp.where(gl == g_max, g_iota, N_GROUPS), axis=0, keepdims=True)
    g_w = 1.0 / jnp.sum(jnp.exp(gl - g_max), axis=0, keepdims=True)
    el = jnp.zeros((EXPERTS_PER_GROUP, tm), F32)
    for g in range(N_GROUPS):
        lo = N_GROUPS + g * EXPERTS_PER_GROUP
        el = jnp.where(g_idx == g, logits[lo:lo + EXPERTS_PER_GROUP, :], el)
    e_iota = lax.broadcasted_iota(jnp.int32, el.shape, 0)
    e1 = jnp.max(el, axis=0, keepdims=True)
    i1 = jnp.min(jnp.where(el == e1, e_iota, EXPERTS_PER_GROUP), axis=0, keepdims=True)
    el2 = jnp.where(e_iota == i1, NEG_INF, el)
    e2 = jnp.max(el2, axis=0, keepdims=True)
    i2 = jnp.min(jnp.where(el2 == e2, e_iota, EXPERTS_PER_GROUP), axis=0, keepdims=True)
    p2 = jnp.exp(e2 - e1)
    w1 = g_w / (1.0 + p2)
    w2 = g_w * p2 / (1.0 + p2)
    base = g_idx * EXPERTS_PER_GROUP
    zero = jnp.zeros((1, tm), F32)
    route_ref[0] = jnp.concatenate(
        [(base + i1).astype(F32), (base + i2).astype(F32), w1, w2, zero, zero, zero, zero], axis=0)


def _mix(x, yda, yml, wo1, wo2, gx, wq, kmem, vmem, wxo, gf, wr_hi, wr_lo, br):
    bsz, seq, d = x.shape
    tm = min(ROW_TILE, seq)
    mlen = kmem.shape[1]
    full = lambda a: pl.BlockSpec(a.shape, lambda b, i: (0,) * a.ndim)
    tok = lambda w: pl.BlockSpec((1, tm, w), lambda b, i: (b, i, 0))
    memspec = pl.BlockSpec((1, mlen, d), lambda b, i: (b, 0, 0))
    return pl.pallas_call(
        _mix_body,
        grid=(bsz, seq // tm),
        in_specs=[tok(d), tok(yda.shape[-1]), tok(yml.shape[-1]), full(wo1), full(wo2), full(gx), full(wq),
                  memspec, memspec, full(wxo), full(gf), full(wr_hi), full(wr_lo), full(br)],
        out_specs=[pl.BlockSpec((1, tm * SUBLANE, LANE), lambda b, i: (b, i, 0)),
                   pl.BlockSpec((1, SUBLANE, tm), lambda b, i: (b, 0, i))],
        out_shape=[
            jax.ShapeDtypeStruct((bsz, seq * d // LANE, LANE), F32),
            jax.ShapeDtypeStruct((bsz, SUBLANE, seq), F32),
        ],
        compiler_params=_params("parallel", "parallel"),
        name="mix_xattn_router",
    )(x, yda, yml, wo1, wo2, gx, wq, kmem, vmem, wxo, gf, wr_hi, wr_lo, br)


def _to_token_tiles(ref, first_row, dense):
    n, d = dense.shape
    for c in range(d // LANE):
        ref[pl.ds(first_row * SUBLANE + c, n, stride=SUBLANE), :] = dense[:, c * LANE:(c + 1) * LANE]


def _from_token_tiles(ref, first_row, n):
    return jnp.concatenate(
        [ref[pl.ds(first_row * SUBLANE + c, n, stride=SUBLANE), :] for c in range(SUBLANE)], axis=1)


def _moe_body(start_ref, cnt_ref, x_hbm, tok_hbm, wt_hbm, gf_ref, gfin_ref, w1_ref, w3_ref, w2_ref, out_hbm,
              acc, h3, xg, yt, stage, tok_s, wt_s, sems, load_sems, store_sems, *, final_norm):
    tile = pl.program_id(0)
    e = pl.program_id(1)
    tt = h3.shape[0] // SUBLANE
    chunk_rows = stage.shape[1]
    group = SUBLANE

    def row_tile(ref, row0):
        return ref.at[pl.ds(pl.multiple_of(row0, SUBLANE), SUBLANE), :]

    def token_tile(ref, r):
        return ref.at[pl.ds(r * SUBLANE, SUBLANE), :]

    spare_row = tt * SUBLANE

    n_chunks = tt // chunk_rows
    chunk_tiles = chunk_rows * SUBLANE

    def load_chunk(i):
        rows = pl.ds(pl.multiple_of(i * chunk_tiles, chunk_tiles), chunk_tiles)
        return pltpu.make_async_copy(x_hbm.at[tile, rows, :], acc.at[rows, :], load_sems.at[i])

    @pl.when(e == 0)
    def _():
        lists = [pltpu.make_async_copy(tok_hbm.at[tile], tok_s, sems.at[0]),
                 pltpu.make_async_copy(wt_hbm.at[tile], wt_s, sems.at[1])]
        for cp in lists:
            cp.start()
        for i in range(n_chunks):
            load_chunk(i).start()
        token_tile(acc, tt)[...] = jnp.zeros((SUBLANE, LANE), F32)

        def ffn_norm(i, c):
            load_chunk(i).wait()
            r0 = i * chunk_rows
            _to_token_tiles(h3, r0, _rms(_from_token_tiles(acc, r0, chunk_rows), gf_ref[...]))
            return c
        lax.fori_loop(0, n_chunks, ffn_norm, 0)
        for cp in lists:
            cp.wait()

    start = start_ref[tile * N_EXPERTS + e]
    count = cnt_ref[tile * N_EXPERTS + e]

    def run_block(first, cnt, rows):
        idx0 = start + first
        for r in range(rows):
            token_tile(xg, r)[...] = row_tile(h3, tok_s[idx0 + r])[...]

        xb = _from_token_tiles(xg, 0, rows).astype(BF16)
        a = _dot(xb, w1_ref[0])
        b = _dot(xb, w3_ref[0])
        hmid = (a * _sigmoid(a) * b).astype(BF16)
        _to_token_tiles(yt, 0, _dot(hmid, w2_ref[0]))

        for g0 in range(0, rows, group):
            dst = [jnp.where(g0 + u < cnt, tok_s[idx0 + g0 + u], spare_row) for u in range(group)]
            new = [row_tile(acc, dst[u])[...] + wt_s[idx0 + g0 + u] * token_tile(yt, g0 + u)[...]
                   for u in range(group)]
            for u in range(group):
                row_tile(acc, dst[u])[...] = new[u]

    big = xg.shape[0] // SUBLANE
    small = big // 2
    n_small = (count + small - 1) // small
    n_big = n_small // 2

    def big_block(i, c):
        run_block(i * big, jnp.minimum(big, count - i * big), big)
        return c

    lax.fori_loop(0, n_big, big_block, 0)

    @pl.when(n_small % 2 == 1)
    def _():
        run_block(n_big * big, count - n_big * big, small)

    @pl.when(e == pl.num_programs(1) - 1)
    def _():
        n_slots = stage.shape[0]

        def store_chunk(i, slot):
            r0 = pl.multiple_of(i * chunk_rows, chunk_rows)
            return pltpu.make_async_copy(stage.at[slot], out_hbm.at[tile, pl.ds(r0, chunk_rows), :],
                                         store_sems.at[slot])

        def write_back(p, c):
            for slot in range(n_slots):
                i = p * n_slots + slot

                @pl.when(p > 0)
                def _():
                    store_chunk(i - n_slots, slot).wait()

                rows = _from_token_tiles(acc, i * chunk_rows, chunk_rows)
                stage[slot] = _rms(rows, gfin_ref[...]) if final_norm else rows
                store_chunk(i, slot).start()
            return c
        lax.fori_loop(0, n_chunks // n_slots, write_back, 0)
        for slot in range(n_slots):
            store_chunk(n_chunks - n_slots + slot, slot).wait()


def _moe(start, cnt, tok, wt, x_token_tiles, gf, gfin, w1, w3, w2, final_norm):
    ntiles, tile_rows, _ = x_token_tiles.shape
    tt = tile_rows // SUBLANE
    d = SUBLANE * LANE
    de = w1.shape[-1]
    seg = tok.shape[1]
    body = functools.partial(_moe_body, final_norm=final_norm)
    hbm = pl.BlockSpec(memory_space=pl.ANY)
    grid_spec = pltpu.PrefetchScalarGridSpec(
        num_scalar_prefetch=2,
        grid=(ntiles, N_EXPERTS),
        in_specs=[
            hbm, hbm, hbm,
            pl.BlockSpec((1, d), lambda t, e, *_: (0, 0)),
            pl.BlockSpec((1, d), lambda t, e, *_: (0, 0)),
            pl.BlockSpec((1, d, de), lambda t, e, *_: (e, 0, 0)),
            pl.BlockSpec((1, d, de), lambda t, e, *_: (e, 0, 0)),
            pl.BlockSpec((1, de, d), lambda t, e, *_: (e, 0, 0)),
        ],
        out_specs=hbm,
        scratch_shapes=[
            pltpu.VMEM((tile_rows + SUBLANE, LANE), F32),
            pltpu.VMEM((tile_rows, LANE), F32),
            pltpu.VMEM((2 * MOE_ROWS * SUBLANE, LANE), F32),
            pltpu.VMEM((2 * MOE_ROWS * SUBLANE, LANE), F32),
            pltpu.VMEM((2, 2 * MOE_ROWS, d), F32),
            pltpu.SMEM((seg,), jnp.int32),
            pltpu.SMEM((seg,), F32),
            pltpu.SemaphoreType.DMA((2,)),
            pltpu.SemaphoreType.DMA((tt // (2 * MOE_ROWS),)),
            pltpu.SemaphoreType.DMA((2,)),
        ],
    )
    return pl.pallas_call(
        body,
        grid_spec=grid_spec,
        out_shape=jax.ShapeDtypeStruct((ntiles, tt, d), F32),
        compiler_params=_params("arbitrary", "arbitrary"),
        name="moe",
    )(start, cnt, x_token_tiles, tok, wt, gf, gfin, w1, w3, w2)


def _rope_tables(seq):
    half = DA_HEAD_DIM // 2
    inv_freq = ROPE_THETA ** (-jnp.arange(0, DA_HEAD_DIM, 2, dtype=F32) / DA_HEAD_DIM)
    ang = jnp.arange(seq, dtype=F32)[:, None] * inv_freq[None, :]
    cos, sin = jnp.cos(ang), jnp.sin(ang)
    reps = LANE // half
    cos_t = jnp.tile(cos, (1, reps))
    sin_t = jnp.tile(jnp.concatenate([-sin, sin], axis=1), (1, reps // 2))
    return cos_t, sin_t


def _sort_assignments(route, tt):
    bsz, _, seq = route.shape
    per = seq // tt
    ntiles = bsz * per
    tile = lambda a: a.reshape(bsz, TOP_K_ROWS, per, tt).transpose(0, 2, 1, 3).reshape(ntiles, TOP_K_ROWS * tt)
    keys = tile(route[:, 0:TOP_K_ROWS, :]).astype(jnp.int32)
    wts = tile(route[:, TOP_K_ROWS:2 * TOP_K_ROWS, :])
    order = jnp.argsort(keys, axis=-1).astype(jnp.int32)
    tok = (order % tt) * SUBLANE
    wt = jnp.take_along_axis(wts, order, axis=-1)
    counts = jnp.sum(keys[:, :, None] == jnp.arange(N_EXPERTS, dtype=jnp.int32)[None, None, :], axis=1,
                     dtype=jnp.int32)
    start = jnp.cumsum(counts, axis=-1, dtype=jnp.int32) - counts
    over = ((0, 0), (0, 2 * MOE_ROWS))
    return start.reshape(-1), counts.reshape(-1), jnp.pad(tok, over), jnp.pad(wt, over)


TOP_K_ROWS = 2


def kernel(x, mem, norm_mix_g, w_in, w_out, da_lambda, da_subln_g, ml_conv_w, ml_conv_b, ml_gate_b, ml_norm_g, norm_x_g, norm_mem_g, w_xq, w_xk, w_xv, w_xo, norm_ffn_g, w_router_group, b_router_group, w_router_expert, b_router_expert, w1, w3, w2, norm_final_g):
    depth = w_in.shape[0]
    bsz, seq, d = x.shape
    da_qk = DA_HEADS * 2 * DA_HEAD_DIM
    da_width = DA_HEADS * DA_VAL_DIM
    ml_width = ml_norm_g.shape[-1]
    n_main = 2 * da_qk + da_width + 4 * ml_width
    n_in = w_in.shape[-1]
    cos_t, sin_t = _rope_tables(seq)
    row = lambda v: v.reshape(1, -1).astype(F32)

    for l in range(depth):
        lam_init = 0.8 - 0.6 * math.exp(-0.3 * l)
        w_pad = jnp.pad(w_in[l], ((0, 0), (0, n_main + LANE - n_in))).astype(BF16)
        qt, vt, mvt, mot, proj, gcol, grow = _inproj(x, row(norm_mix_g[l]), w_pad, cos_t, sin_t)

        yda = _diffattn(qt, proj, vt, da_lambda[l].astype(F32), da_subln_g[l].astype(F32).reshape(-1, 1), lam_init)

        gb = ml_gate_b[l].astype(F32).reshape(1, 2 * ML_HEADS)
        gb_col = jnp.pad(gb, ((0, 0), (0, LANE - 2 * ML_HEADS)))
        gb_row = gb.reshape(2 * ML_HEADS, 1)
        ng_b = jnp.broadcast_to(ml_norm_g[l].astype(F32)[:, None], (ml_width, LANE))
        yml = _mlstm(proj, mvt, mot, gcol, grow, ml_conv_w[l].reshape(ML_CONV, 2 * ml_width).astype(F32),
                     row(ml_conv_b[l]), gb_col, gb_row, ng_b, da_qk)

        kmem, vmem = _memkv(mem, row(norm_mem_g[l]), w_xk[l].astype(BF16), w_xv[l].astype(BF16))

        wr = jnp.concatenate([w_router_group[l], w_router_expert[l]], axis=1).astype(F32).T
        wr = jnp.pad(wr, ((0, ROUTER_ROWS - wr.shape[0]), (0, 0)))
        wr_hi = wr.astype(BF16)
        wr_lo = (wr - wr_hi.astype(F32)).astype(BF16)
        br = jnp.pad(jnp.concatenate([b_router_group[l], b_router_expert[l]]).astype(F32),
                     (0, ROUTER_ROWS - N_GROUPS - N_EXPERTS)).reshape(ROUTER_ROWS, 1)
        wo = w_out[l].astype(BF16)
        x2, route = _mix(x, yda, yml, wo[:da_width], wo[da_width:], row(norm_x_g[l]), w_xq[l].astype(BF16),
                         kmem, vmem, w_xo[l].astype(BF16), row(norm_ffn_g[l]), wr_hi, wr_lo, br)

        tt = min(MOE_TILE, seq)
        x = _moe(*_sort_assignments(route, tt), x2.reshape(bsz * seq // tt, tt * d // LANE, LANE),
                 row(norm_ffn_g[l]), row(norm_final_g),
                 w1[l].astype(BF16), w3[l].astype(BF16), w2[l].astype(BF16),
                 final_norm=(l + 1 == depth)).reshape(bsz, seq, d)
    return x
```

```python
import functools
import math

import jax
import jax.numpy as jnp
from jax import lax
from jax.experimental import pallas as pl
from jax.experimental.pallas import tpu as pltpu

F32 = jnp.float32
BF16 = jnp.bfloat16

LANE = 128
SUBLANE = 8
VMEM_LIMIT_BYTES = 56 * 1024 * 1024

EPS = 1e-6
ROPE_THETA = 10000.0
DA_HEADS = 4
DA_HEAD_DIM = 64
DA_VAL_DIM = 2 * DA_HEAD_DIM
ML_HEADS = 4
ML_CONV = 4
X_HEADS = 4
N_GROUPS = 4
EXPERTS_PER_GROUP = 8
N_EXPERTS = N_GROUPS * EXPERTS_PER_GROUP
ROUTER_ROWS = 40

ROW_TILE = 512
DA_HEADS_PER_STEP = 4
ML_CHUNK = 256
ML_BATCH_PER_STEP = 2
MOE_TILE = 4096
MOE_ROWS = 128
MOE_FIRST_ROWS = 288

NEG_INF = float("-inf")
LOG2_E = math.log2(math.e)


def _rms(x, g):
    return x * lax.rsqrt(jnp.mean(x * x, axis=-1, keepdims=True) + EPS) * g


def _dot(a, b):
    return jnp.dot(a, b, preferred_element_type=F32)


def _dot_nt(a, b):
    return lax.dot_general(a, b, (((1,), (1,)), ((), ())), preferred_element_type=F32)


def _dot_tn(a, b):
    return lax.dot_general(a, b, (((0,), (0,)), ((), ())), preferred_element_type=F32)


def _split_bf16(x):
    hi = x.astype(BF16)
    lo = (x - hi.astype(F32)).astype(BF16)
    return hi, lo


def _params(*semantics):
    return pltpu.CompilerParams(dimension_semantics=semantics, vmem_limit_bytes=VMEM_LIMIT_BYTES)


def _inproj_body(x_ref, g_ref, w_ref, cos_ref, sin_ref, qt_ref, vt_ref, mvt_ref, mot_ref, proj_ref, gcol_ref,
                 grow_ref):
    hb = _rms(x_ref[0], g_ref[...]).astype(BF16)
    cos = cos_ref[...]
    sin = sin_ref[...]
    lane = lax.broadcasted_iota(jnp.int32, cos.shape, 1)
    first_half = (lane % DA_HEAD_DIM) < (DA_HEAD_DIM // 2)
    n_main = w_ref.shape[1] - LANE
    wide = 4 * LANE
    for c in range(n_main // LANE):
        if c % (wide // LANE) == 0:
            acc = _dot(hb, w_ref[:, c * LANE:c * LANE + wide])
        t = acc[:, (c * LANE) % wide:(c * LANE) % wide + LANE]
        if c < 2 * DA_HEADS:
            rot = jnp.where(first_half, pltpu.roll(t, LANE - DA_HEAD_DIM // 2, 1),
                            pltpu.roll(t, DA_HEAD_DIM // 2, 1))
            t = t * cos + rot * sin
        if c < DA_HEADS:
            qt_ref[0, 0, c * LANE:(c + 1) * LANE, :] = (t * (DA_HEAD_DIM ** -0.5 * LOG2_E)).T.astype(BF16)
        elif c < 2 * DA_HEADS:
            proj_ref[0, :, (c - DA_HEADS) * LANE:(c - DA_HEADS + 1) * LANE] = t.astype(BF16)
        elif c < 3 * DA_HEADS:
            vt_ref[0, 0, (c - 2 * DA_HEADS) * LANE:(c - 2 * DA_HEADS + 1) * LANE, :] = t.T.astype(BF16)
        elif c < 3 * DA_HEADS + 2 * ML_HEADS:
            proj_ref[0, :, (c - 2 * DA_HEADS) * LANE:(c - 2 * DA_HEADS + 1) * LANE] = t.astype(BF16)
        elif c < 3 * DA_HEADS + 3 * ML_HEADS:
            m = c - 3 * DA_HEADS - 2 * ML_HEADS
            mvt_ref[0, 0, m * LANE:(m + 1) * LANE, :] = t.T.astype(BF16)
        else:
            m = c - 3 * DA_HEADS - 3 * ML_HEADS
            mot_ref[0, 0, m * LANE:(m + 1) * LANE, :] = t.T.astype(BF16)
    gates = _dot(hb, w_ref[:, n_main:n_main + LANE])
    gcol_ref[0] = gates
    grow_ref[0] = gates.T[:SUBLANE, :]


def _inproj(x, g, w_pad, cos_t, sin_t):
    bsz, seq, d = x.shape
    tm = min(ROW_TILE, seq)
    nt = seq // tm
    da_w = DA_HEADS * DA_VAL_DIM
    ml_w = ML_HEADS * LANE
    n_proj = w_pad.shape[1] - LANE - 2 * da_w - 2 * ml_w
    return pl.pallas_call(
        _inproj_body,
        grid=(bsz, nt),
        in_specs=[
            pl.BlockSpec((1, tm, d), lambda b, i: (b, i, 0)),
            pl.BlockSpec((1, d), lambda b, i: (0, 0)),
            pl.BlockSpec(w_pad.shape, lambda b, i: (0, 0)),
            pl.BlockSpec((tm, LANE), lambda b, i: (i, 0)),
            pl.BlockSpec((tm, LANE), lambda b, i: (i, 0)),
        ],
        out_specs=[
            pl.BlockSpec((1, 1, da_w, tm), lambda b, i: (b, i, 0, 0)),
            pl.BlockSpec((1, 1, da_w, tm), lambda b, i: (b, i, 0, 0)),
            pl.BlockSpec((1, 1, ml_w, tm), lambda b, i: (b, i, 0, 0)),
            pl.BlockSpec((1, 1, ml_w, tm), lambda b, i: (b, i, 0, 0)),
            pl.BlockSpec((1, tm, n_proj), lambda b, i: (b, i, 0)),
            pl.BlockSpec((1, tm, LANE), lambda b, i: (b, i, 0)),
            pl.BlockSpec((1, SUBLANE, tm), lambda b, i: (b, 0, i)),
        ],
        out_shape=[
            jax.ShapeDtypeStruct((bsz, nt, da_w, tm), BF16),
            jax.ShapeDtypeStruct((bsz, nt, da_w, tm), BF16),
            jax.ShapeDtypeStruct((bsz, nt, ml_w, tm), BF16),
            jax.ShapeDtypeStruct((bsz, nt, ml_w, tm), BF16),
            jax.ShapeDtypeStruct((bsz, seq, n_proj), BF16),
            jax.ShapeDtypeStruct((bsz, seq, LANE), F32),
            jax.ShapeDtypeStruct((bsz, SUBLANE, seq), F32),
        ],
        compiler_params=_params("parallel", "parallel"),
        name="inproj",
    )(x, g, w_pad, cos_t, sin_t)


def _diffattn_body(qt_ref, k_ref, vt_ref, lam_ref, g_ref, o_ref, m_scr, acc_scr, p_scr, *, lam_init):
    qi = pl.program_id(2)
    tq = qt_ref.shape[3]
    heads = qt_ref.shape[2] // DA_VAL_DIM
    q2t = []
    for g in range(heads):
        qt = qt_ref[0, 0, g * DA_VAL_DIM:(g + 1) * DA_VAL_DIM, :]
        row = lax.broadcasted_iota(jnp.int32, qt.shape, 0)
        zero = jnp.zeros_like(qt)
        q2t.append(jnp.concatenate([jnp.where(row < DA_HEAD_DIM, qt, zero),
                                    jnp.where(row >= DA_HEAD_DIM, qt, zero)], axis=1))
    pad_rows = acc_scr.shape[0] - DA_VAL_DIM
    ones_rows = jnp.where(lax.broadcasted_iota(jnp.int32, (pad_rows, tq), 0) == 0, 1.0, 0.0).astype(BF16)

    m_scr[...] = jnp.full(m_scr.shape, NEG_INF, F32)
    acc_scr[...] = jnp.zeros(acc_scr.shape, F32)

    n_chains = 4
    width = 2 * tq // n_chains
    chains = tuple((g, slice(c * width, (c + 1) * width),
                    slice(g * 2 * tq + c * width, g * 2 * tq + (c + 1) * width))
                   for c in range(n_chains) for g in range(heads))

    def scores(j, masked, prev_values):
        out = []
        for g, local, cols in chains:
            k = k_ref[0, pl.ds(pl.multiple_of(j * tq, tq), tq), g * LANE:(g + 1) * LANE]
            s = _dot(k, q2t[g][:, local])
            pv = values(j - 1, g, cols) if prev_values else None
            if masked:
                key = lax.broadcasted_iota(jnp.int32, s.shape, 0)
                qry = lax.broadcasted_iota(jnp.int32, s.shape, 1) + local.start % tq
                s = jnp.where(key <= qry, s, NEG_INF)
            m_prev = m_scr[:, cols]
            m_new = jnp.maximum(m_prev, jnp.max(s, axis=0, keepdims=True))
            p_scr[:, cols] = jnp.exp2(s - m_new).astype(BF16)
            m_scr[:, cols] = m_new
            out.append((jnp.exp2(m_prev - m_new), pv))
        return out

    def values(j, g, cols):
        v_aug = jnp.concatenate([vt_ref[0, j, g * DA_VAL_DIM:(g + 1) * DA_VAL_DIM, :], ones_rows], axis=0)
        return _dot(v_aug, p_scr[:, cols])

    def step(j, masked):
        for (_, _, cols), (alpha, pv) in zip(chains, scores(j, masked, True)):
            acc_scr[:, cols] = alpha * (acc_scr[:, cols] + pv)

    def full_step_pair(i, carry):
        step(1 + 2 * i, False)
        step(2 + 2 * i, False)
        return carry

    @pl.when(qi == 0)
    def _():
        scores(0, True, False)

    @pl.when(qi > 0)
    def _():
        scores(0, False, False)
        lax.fori_loop(0, (qi - 1) // 2, full_step_pair, 0)

        @pl.when((qi - 1) % 2 == 1)
        def _():
            step(qi - 1, False)

        step(qi, True)

    for g, _, cols in chains:
        acc_scr[:, cols] = acc_scr[:, cols] + values(qi, g, cols)

    lv = lam_ref[...]
    lam = (jnp.exp(jnp.sum(lv[0:1, :] * lv[1:2, :], axis=-1, keepdims=True))
           - jnp.exp(jnp.sum(lv[2:3, :] * lv[3:4, :], axis=-1, keepdims=True)) + lam_init)
    for g in range(heads):
        acc = acc_scr[:, g * 2 * tq:(g + 1) * 2 * tq]
        o = acc[0:DA_VAL_DIM, :] / acc[DA_VAL_DIM:DA_VAL_DIM + 1, :]
        o = o[:, :tq] - lam * o[:, tq:]
        y = o * lax.rsqrt(jnp.mean(o * o, axis=0, keepdims=True) + EPS) * g_ref[...] * (1.0 - lam_init)
        o_ref[0, :, g * LANE:(g + 1) * LANE] = y.T.astype(o_ref.dtype)


def _diffattn(qt, proj, vt, da_lambda, subln_g_col, lam_init):
    bsz, nt, _, tq = qt.shape
    seq = nt * tq
    hp = DA_HEADS_PER_STEP
    body = functools.partial(_diffattn_body, lam_init=lam_init)
    return pl.pallas_call(
        body,
        grid=(bsz, DA_HEADS // hp, nt),
        in_specs=[
            pl.BlockSpec((1, 1, hp * DA_VAL_DIM, tq), lambda b, h, i: (b, i, h, 0)),
            pl.BlockSpec((1, seq, hp * LANE), lambda b, h, i: (b, 0, h)),
            pl.BlockSpec((1, nt, hp * DA_VAL_DIM, tq), lambda b, h, i: (b, 0, h, 0)),
            pl.BlockSpec(da_lambda.shape, lambda b, h, i: (0, 0)),
            pl.BlockSpec((DA_VAL_DIM, 1), lambda b, h, i: (0, 0)),
        ],
        out_specs=pl.BlockSpec((1, tq, hp * LANE), lambda b, h, i: (b, i, h)),
        out_shape=jax.ShapeDtypeStruct((bsz, seq, DA_HEADS * DA_VAL_DIM), BF16),
        scratch_shapes=[
            pltpu.VMEM((1, hp * 2 * tq), F32),
            pltpu.VMEM((DA_VAL_DIM + 2 * SUBLANE, hp * 2 * tq), F32),
            pltpu.VMEM((tq, hp * 2 * tq), BF16),
        ],
        compiler_params=_params("parallel", "parallel", "arbitrary"),
        name="diffattn",
    )(qt, proj, vt, da_lambda, subln_g_col)


def _log_sigmoid(x):
    return jnp.minimum(x, 0.0) - jnp.log(1.0 + jnp.exp(-jnp.abs(x)))


def _sigmoid(x):
    return 1.0 / (1.0 + jnp.exp(-x))


def _mlstm_body(q_ref, k_ref, vt_ref, ot_ref, gcol_ref, grow_ref, cw_ref, cb_ref, gbcol_ref, gbrow_ref, ng_ref,
                y_ref, xbuf, c_scr, m_scr):
    @pl.when(pl.program_id(1) == 0)
    def _():
        xbuf[...] = jnp.zeros(xbuf.shape, xbuf.dtype)
        c_scr[...] = jnp.zeros(c_scr.shape, F32)
        m_scr[...] = jnp.zeros(m_scr.shape, F32)

    for bb in range(q_ref.shape[0]):
        _mlstm_row(bb, q_ref, k_ref, vt_ref, ot_ref, gcol_ref, grow_ref, cw_ref, cb_ref, gbcol_ref, gbrow_ref,
                   ng_ref, y_ref, xbuf, c_scr, m_scr)


def _mlstm_row(bb, q_ref, k_ref, vt_ref, ot_ref, gcol_ref, grow_ref, cw_ref, cb_ref, gbcol_ref, gbrow_ref, ng_ref,
               y_ref, xbuf, c_scr, m_scr):
    L = q_ref.shape[1]
    width = q_ref.shape[2]
    dh = width // ML_HEADS
    tail = xbuf.shape[1]
    aug = c_scr.shape[1]

    u = jnp.concatenate([q_ref[bb], k_ref[bb]], axis=1)
    u_ext = jnp.concatenate([xbuf[bb], u], axis=0)
    src = lax.broadcasted_iota(jnp.int32, (L, L + tail), 1) - lax.broadcasted_iota(jnp.int32, (L, L + tail), 0)
    conv = cb_ref[...] + cw_ref[ML_CONV - 1:ML_CONV, :] * u.astype(F32)
    for back in range(1, ML_CONV):
        shift = jnp.where(src == tail - back, 1.0, 0.0).astype(BF16)
        conv = conv + cw_ref[ML_CONV - 1 - back:ML_CONV - back, :] * _dot(shift, u_ext)
    xbuf[bb] = u[L - tail:, :]
    qk = conv * _sigmoid(conv)

    gc = gcol_ref[bb] + gbcol_ref[...]
    gr = grow_ref[bb] + gbrow_ref[...]
    row_id = lax.broadcasted_iota(jnp.int32, (L, L), 0)
    col_id = lax.broadcasted_iota(jnp.int32, (L, L), 1)
    causal = row_id <= col_id
    tri = jnp.where(col_id <= row_id, 1.0, 0.0).astype(BF16)
    tri_t = jnp.where(causal, 1.0, 0.0).astype(BF16)
    lfc_hi, lfc_lo = _split_bf16(_log_sigmoid(gc))
    lfr_hi, lfr_lo = _split_bf16(_log_sigmoid(gr))
    a_c = _dot(tri, lfc_hi) + _dot(tri, lfc_lo)
    a_r = _dot(lfr_hi, tri_t) + _dot(lfr_lo, tri_t)

    ones_rows = jnp.where(lax.broadcasted_iota(jnp.int32, (aug - dh, L), 0) == 0, 1.0, 0.0).astype(BF16)

    for h in range(ML_HEADS):
        fh = ML_HEADS + h
        st = bb * ML_HEADS + h
        m_prev = m_scr[st:st + 1, 0:1]
        a_row = a_r[fh:fh + 1, :]
        ig_row = gr[h:h + 1, :]
        nb = jnp.where(causal, gc[:, h:h + 1] - a_c[:, fh:fh + 1], NEG_INF)
        m_row = a_row + jnp.maximum(m_prev, jnp.max(nb, axis=0, keepdims=True))
        w_intra = jnp.exp(nb + (a_row - m_row))
        w_inter = jnp.exp(a_row + m_prev - m_row)

        qh = qk[:, h * dh:(h + 1) * dh].astype(BF16)
        kh = (qk[:, width + h * dh:width + (h + 1) * dh] * (dh ** -0.5)).astype(BF16)
        p_t = (_dot_nt(kh, qh) * w_intra).astype(BF16)
        v_aug = jnp.concatenate([vt_ref[bb, 0, h * dh:(h + 1) * dh, :], ones_rows], axis=0)
        c_t = c_scr[st]
        num = w_inter * _dot_nt(c_t.astype(BF16), qh) + _dot(v_aug, p_t)
        scale = 1.0 / jnp.maximum(jnp.abs(num[dh:dh + 1, :]), jnp.exp(-m_row))
        hid = num[0:dh, :] * scale

        g_last = a_row[:, L - 1:L]
        log_w = g_last - a_row + ig_row
        m_new = jnp.maximum(g_last + m_prev, jnp.max(log_w, axis=1, keepdims=True))
        w_state = jnp.exp(log_w - m_new)
        decay = jnp.exp(g_last + m_prev - m_new)
        wv = (w_state * v_aug.astype(F32)).astype(BF16)
        c_scr[st] = decay * c_t + _dot(wv, kh)
        m_scr[st:st + 1, :] = jnp.broadcast_to(m_new, (1, m_scr.shape[1]))

        mu = jnp.mean(hid, axis=0, keepdims=True)
        cen = hid - mu
        var = jnp.mean(cen * cen, axis=0, keepdims=True)
        g_b = ng_ref[h * dh:(h + 1) * dh, :]
        hn = cen * lax.rsqrt(var + EPS) * jnp.concatenate([g_b] * (L // LANE), axis=1)
        gate = _sigmoid(ot_ref[bb, 0, h * dh:(h + 1) * dh, :].astype(F32))
        y_ref[bb, :, h * dh:(h + 1) * dh] = (gate * hn).T.astype(y_ref.dtype)


def _mlstm(proj, mvt, mot, gcol, grow, conv_w, conv_b, gb_col, gb_row, norm_g_b, col0):
    bsz, seq, _ = proj.shape
    width = norm_g_b.shape[0]
    dh = width // ML_HEADS
    L = min(ML_CHUNK, seq)
    tm = mvt.shape[-1]
    per = tm // L
    blk0 = col0 // width
    nb = ML_BATCH_PER_STEP
    spec = lambda k: pl.BlockSpec((nb, L, width), lambda b, c: (b, c, blk0 + k))
    tspec = pl.BlockSpec((nb, 1, width, L), lambda b, c: (b, c // per, 0, c % per))
    return pl.pallas_call(
        _mlstm_body,
        grid=(bsz // nb, seq // L),
        in_specs=[
            spec(0), spec(1), tspec, tspec,
            pl.BlockSpec((nb, L, LANE), lambda b, c: (b, c, 0)),
            pl.BlockSpec((nb, SUBLANE, L), lambda b, c: (b, 0, c)),
            pl.BlockSpec(conv_w.shape, lambda b, c: (0, 0)),
            pl.BlockSpec(conv_b.shape, lambda b, c: (0, 0)),
            pl.BlockSpec(gb_col.shape, lambda b, c: (0, 0)),
            pl.BlockSpec(gb_row.shape, lambda b, c: (0, 0)),
            pl.BlockSpec(norm_g_b.shape, lambda b, c: (0, 0)),
        ],
        out_specs=pl.BlockSpec((nb, L, width), lambda b, c: (b, c, 0)),
        out_shape=jax.ShapeDtypeStruct((bsz, seq, width), BF16),
        scratch_shapes=[
            pltpu.VMEM((nb, 2 * SUBLANE, 2 * width), BF16),
            pltpu.VMEM((nb * ML_HEADS, dh + 2 * SUBLANE, dh), F32),
            pltpu.VMEM((nb * ML_HEADS, LANE), F32),
        ],
        compiler_params=_params("parallel", "arbitrary"),
        name="mlstm",
    )(proj, proj, mvt, mot, gcol, grow, conv_w, conv_b, gb_col, gb_row, norm_g_b)


def _memkv_body(mem_ref, g_ref, wk_ref, wv_ref, k_ref, v_ref):
    mb = _rms(mem_ref[0], g_ref[...]).astype(BF16)
    k_ref[0] = _dot(mb, wk_ref[...]).astype(BF16)
    v_ref[0] = _dot(mb, wv_ref[...]).astype(BF16)


def _memkv(mem, g, wk, wv):
    bsz, mlen, d = mem.shape
    full = lambda a: pl.BlockSpec(a.shape, lambda b: (0,) * a.ndim)
    blk = pl.BlockSpec((1, mlen, d), lambda b: (b, 0, 0))
    return pl.pallas_call(
        _memkv_body,
        grid=(bsz,),
        in_specs=[blk, full(g), full(wk), full(wv)],
        out_specs=[blk, blk],
        out_shape=[jax.ShapeDtypeStruct((bsz, mlen, d), BF16)] * 2,
        compiler_params=_params("parallel"),
        name="memkv",
    )(mem, g, wk, wv)


def _mix_body(x_ref, yda_ref, yml_ref, wo1_ref, wo2_ref, gx_ref, wq_ref, km_ref, vm_ref, wxo_ref, gf_ref,
              wrh_ref, wrl_ref, br_ref, x2_ref, route_ref):
    x1 = x_ref[0] + _dot(yda_ref[0], wo1_ref[...]) + _dot(yml_ref[0], wo2_ref[...])

    d = x1.shape[-1]
    dh = d // X_HEADS
    q = _dot(_rms(x1, gx_ref[...]).astype(BF16), wq_ref[...]).astype(BF16)
    heads = []
    for h in range(X_HEADS):
        sl = slice(h * dh, (h + 1) * dh)
        s = _dot_nt(q[:, sl], km_ref[0, :, sl]) * (dh ** -0.5)
        p = jnp.exp(s - jnp.max(s, axis=-1, keepdims=True))
        o = _dot(p.astype(BF16), vm_ref[0, :, sl]) / jnp.sum(p, axis=-1, keepdims=True)
        heads.append(o.astype(BF16))
    x2 = x1 + _dot(jnp.concatenate(heads, axis=1), wxo_ref[...])
    _to_token_tiles(x2_ref.at[0], 0, x2)

    h3 = _rms(x2, gf_ref[...])
    h_hi, h_lo = _split_bf16(h3)
    logits = (_dot_nt(wrh_ref[...], h_hi) + _dot_nt(wrh_ref[...], h_lo) + _dot_nt(wrl_ref[...], h_hi)
              + br_ref[...])
    tm = logits.shape[1]
    gl = logits[0:N_GROUPS, :]
    g_iota = lax.broadcasted_iota(jnp.int32, gl.shape, 0)
    g_max = jnp.max(gl, axis=0, keepdims=True)
    g_idx = jnp.min(jnp.where(gl == g_max, g_iota, N_GROUPS), axis=0, keepdims=True)
    g_w = 1.0 / jnp.sum(jnp.exp(gl - g_max), axis=0, keepdims=True)
    el = jnp.zeros((EXPERTS_PER_GROUP, tm), F32)
    for g in range(N_GROUPS):
        lo = N_GROUPS + g * EXPERTS_PER_GROUP
        el = jnp.where(g_idx == g, logits[lo:lo + EXPERTS_PER_GROUP, :], el)
    e_iota = lax.broadcasted_iota(jnp.int32, el.shape, 0)
    e1 = jnp.max(el, axis=0, keepdims=True)
    i1 = jnp.min(jnp.where(el == e1, e_iota, EXPERTS_PER_GROUP), axis=0, keepdims=True)
    el2 = jnp.where(e_iota == i1, NEG_INF, el)
    e2 = jnp.max(el2, axis=0, keepdims=True)
    i2 = jnp.min(jnp.where(el2 == e2, e_iota, EXPERTS_PER_GROUP), axis=0, keepdims=True)
    p2 = jnp.exp(e2 - e1)
    w1 = g_w / (1.0 + p2)
    w2 = g_w * p2 / (1.0 + p2)
    base = g_idx * EXPERTS_PER_GROUP
    zero = jnp.zeros((1, tm), F32)
    route_ref[0] = jnp.concatenate(
        [(base + i1).astype(F32), (base + i2).astype(F32), w1, w2, zero, zero, zero, zero], axis=0)


def _mix(x, yda, yml, wo1, wo2, gx, wq, kmem, vmem, wxo, gf, wr_hi, wr_lo, br):
    bsz, seq, d = x.shape
    tm = min(ROW_TILE, seq)
    mlen = kmem.shape[1]
    full = lambda a: pl.BlockSpec(a.shape, lambda b, i: (0,) * a.ndim)
    tok = lambda w: pl.BlockSpec((1, tm, w), lambda b, i: (b, i, 0))
    memspec = pl.BlockSpec((1, mlen, d), lambda b, i: (b, 0, 0))
    return pl.pallas_call(
        _mix_body,
        grid=(bsz, seq // tm),
        in_specs=[tok(d), tok(yda.shape[-1]), tok(yml.shape[-1]), full(wo1), full(wo2), full(gx), full(wq),
                  memspec, memspec, full(wxo), full(gf), full(wr_hi), full(wr_lo), full(br)],
        out_specs=[pl.BlockSpec((1, tm * SUBLANE, LANE), lambda b, i: (b, i, 0)),
                   pl.BlockSpec((1, SUBLANE, tm), lambda b, i: (b, 0, i))],
        out_shape=[
            jax.ShapeDtypeStruct((bsz, seq * d // LANE, LANE), F32),
            jax.ShapeDtypeStruct((bsz, SUBLANE, seq), F32),
        ],
        compiler_params=_params("parallel", "parallel"),
        name="mix_xattn_router",
    )(x, yda, yml, wo1, wo2, gx, wq, kmem, vmem, wxo, gf, wr_hi, wr_lo, br)


def _to_token_tiles(ref, first_row, dense):
    n, d = dense.shape
    for c in range(d // LANE):
        ref[pl.ds(first_row * SUBLANE + c, n, stride=SUBLANE), :] = dense[:, c * LANE:(c + 1) * LANE]


def _from_token_tiles(ref, first_row, n):
    return jnp.concatenate(
        [ref[pl.ds(first_row * SUBLANE + c, n, stride=SUBLANE), :] for c in range(SUBLANE)], axis=1)


def _moe_body(start_ref, cnt_ref, x_hbm, tok_hbm, wt_hbm, gf_ref, gfin_ref, w1_ref, w3_ref, w2_ref, out_hbm,
              acc, h3, xg, yt, stage, tok_s, wt_s, sems, load_sems, store_sems, *, final_norm):
    tile = pl.program_id(0)
    e = pl.program_id(1)
    tt = h3.shape[0] // SUBLANE
    chunk_rows = stage.shape[1]
    group = SUBLANE

    def row_tile(ref, row0):
        return ref.at[pl.ds(pl.multiple_of(row0, SUBLANE), SUBLANE), :]

    def token_tile(ref, r):
        return ref.at[pl.ds(r * SUBLANE, SUBLANE), :]

    spare_row = tt * SUBLANE

    n_chunks = tt // chunk_rows
    chunk_tiles = chunk_rows * SUBLANE

    def load_chunk(i):
        rows = pl.ds(pl.multiple_of(i * chunk_tiles, chunk_tiles), chunk_tiles)
        return pltpu.make_async_copy(x_hbm.at[tile, rows, :], acc.at[rows, :], load_sems.at[i])

    @pl.when(e == 0)
    def _():
        lists = [pltpu.make_async_copy(tok_hbm.at[tile], tok_s, sems.at[0]),
                 pltpu.make_async_copy(wt_hbm.at[tile], wt_s, sems.at[1])]
        for cp in lists:
            cp.start()
        for i in range(n_chunks):
            load_chunk(i).start()
        token_tile(acc, tt)[...] = jnp.zeros((SUBLANE, LANE), F32)

        def ffn_norm(i, c):
            load_chunk(i).wait()
            r0 = i * chunk_rows
            _to_token_tiles(h3, r0, _rms(_from_token_tiles(acc, r0, chunk_rows), gf_ref[...]))
            return c
        lax.fori_loop(0, n_chunks, ffn_norm, 0)
        for cp in lists:
            cp.wait()

    start = start_ref[tile * N_EXPERTS + e]
    count = cnt_ref[tile * N_EXPERTS + e]

    def run_block(first, cnt, rows):
        idx0 = start + first
        for r in range(rows):
            token_tile(xg, r)[...] = row_tile(h3, tok_s[idx0 + r])[...]

        xb = _from_token_tiles(xg, 0, rows).astype(BF16)
        a = _dot(xb, w1_ref[0])
        b = _dot(xb, w3_ref[0])
        hmid = (a * _sigmoid(a) * b).astype(BF16)
        _to_token_tiles(yt, 0, _dot(hmid, w2_ref[0]))

        for g0 in range(0, rows, group):
            dst = [jnp.where(g0 + u < cnt, tok_s[idx0 + g0 + u], spare_row) for u in range(group)]
            new = [row_tile(acc, dst[u])[...] + wt_s[idx0 + g0 + u] * token_tile(yt, g0 + u)[...]
                   for u in range(group)]
            for u in range(group):
                row_tile(acc, dst[u])[...] = new[u]

    big = xg.shape[0] // SUBLANE
    small = MOE_ROWS

    @pl.when(count > 0)
    def _():
        run_block(0, jnp.minimum(big, count), big)

    def small_block(i, c):
        first = big + i * small
        run_block(first, jnp.minimum(small, count - first), small)
        return c

    lax.fori_loop(0, (jnp.maximum(count - big, 0) + small - 1) // small, small_block, 0)

    @pl.when(e == pl.num_programs(1) - 1)
    def _():
        n_slots = stage.shape[0]

        def store_chunk(i, slot):
            r0 = pl.multiple_of(i * chunk_rows, chunk_rows)
            return pltpu.make_async_copy(stage.at[slot], out_hbm.at[tile, pl.ds(r0, chunk_rows), :],
                                         store_sems.at[slot])

        def write_back(p, c):
            for slot in range(n_slots):
                i = p * n_slots + slot

                @pl.when(p > 0)
                def _():
                    store_chunk(i - n_slots, slot).wait()

                rows = _from_token_tiles(acc, i * chunk_rows, chunk_rows)
                stage[slot] = _rms(rows, gfin_ref[...]) if final_norm else rows
                store_chunk(i, slot).start()
            return c
        lax.fori_loop(0, n_chunks // n_slots, write_back, 0)
        for slot in range(n_slots):
            store_chunk(n_chunks - n_slots + slot, slot).wait()


def _moe(start, cnt, tok, wt, x_token_tiles, gf, gfin, w1, w3, w2, final_norm):
    ntiles, tile_rows, _ = x_token_tiles.shape
    tt = tile_rows // SUBLANE
    d = SUBLANE * LANE
    de = w1.shape[-1]
    seg = tok.shape[1]
    body = functools.partial(_moe_body, final_norm=final_norm)
    hbm = pl.BlockSpec(memory_space=pl.ANY)
    grid_spec = pltpu.PrefetchScalarGridSpec(
        num_scalar_prefetch=2,
        grid=(ntiles, N_EXPERTS),
        in_specs=[
            hbm, hbm, hbm,
            pl.BlockSpec((1, d), lambda t, e, *_: (0, 0)),
            pl.BlockSpec((1, d), lambda t, e, *_: (0, 0)),
            pl.BlockSpec((1, d, de), lambda t, e, *_: (e, 0, 0)),
            pl.BlockSpec((1, d, de), lambda t, e, *_: (e, 0, 0)),
            pl.BlockSpec((1, de, d), lambda t, e, *_: (e, 0, 0)),
        ],
        out_specs=hbm,
        scratch_shapes=[
            pltpu.VMEM((tile_rows + SUBLANE, LANE), F32),
            pltpu.VMEM((tile_rows, LANE), F32),
            pltpu.VMEM((MOE_FIRST_ROWS * SUBLANE, LANE), F32),
            pltpu.VMEM((MOE_FIRST_ROWS * SUBLANE, LANE), F32),
            pltpu.VMEM((2, 2 * MOE_ROWS, d), F32),
            pltpu.SMEM((seg,), jnp.int32),
            pltpu.SMEM((seg,), F32),
            pltpu.SemaphoreType.DMA((2,)),
            pltpu.SemaphoreType.DMA((tt // (2 * MOE_ROWS),)),
            pltpu.SemaphoreType.DMA((2,)),
        ],
    )
    return pl.pallas_call(
        body,
        grid_spec=grid_spec,
        out_shape=jax.ShapeDtypeStruct((ntiles, tt, d), F32),
        compiler_params=_params("arbitrary", "arbitrary"),
        name="moe",
    )(start, cnt, x_token_tiles, tok, wt, gf, gfin, w1, w3, w2)


def _rope_tables(seq):
    half = DA_HEAD_DIM // 2
    inv_freq = ROPE_THETA ** (-jnp.arange(0, DA_HEAD_DIM, 2, dtype=F32) / DA_HEAD_DIM)
    ang = jnp.arange(seq, dtype=F32)[:, None] * inv_freq[None, :]
    cos, sin = jnp.cos(ang), jnp.sin(ang)
    reps = LANE // half
    cos_t = jnp.tile(cos, (1, reps))
    sin_t = jnp.tile(jnp.concatenate([-sin, sin], axis=1), (1, reps // 2))
    return cos_t, sin_t


def _sort_assignments(route, tt):
    bsz, _, seq = route.shape
    per = seq // tt
    ntiles = bsz * per
    tile = lambda a: a.reshape(bsz, TOP_K_ROWS, per, tt).transpose(0, 2, 1, 3).reshape(ntiles, TOP_K_ROWS * tt)
    keys = tile(route[:, 0:TOP_K_ROWS, :]).astype(jnp.int32)
    wts = tile(route[:, TOP_K_ROWS:2 * TOP_K_ROWS, :])
    order = jnp.argsort(keys, axis=-1).astype(jnp.int32)
    tok = (order % tt) * SUBLANE
    wt = jnp.take_along_axis(wts, order, axis=-1)
    counts = jnp.sum(keys[:, :, None] == jnp.arange(N_EXPERTS, dtype=jnp.int32)[None, None, :], axis=1,
                     dtype=jnp.int32)
    start = jnp.cumsum(counts, axis=-1, dtype=jnp.int32) - counts
    over = ((0, 0), (0, MOE_FIRST_ROWS))
    return start.reshape(-1), counts.reshape(-1), jnp.pad(tok, over), jnp.pad(wt, over)


TOP_K_ROWS = 2


def kernel(x, mem, norm_mix_g, w_in, w_out, da_lambda, da_subln_g, ml_conv_w, ml_conv_b, ml_gate_b, ml_norm_g, norm_x_g, norm_mem_g, w_xq, w_xk, w_xv, w_xo, norm_ffn_g, w_router_group, b_router_group, w_router_expert, b_router_expert, w1, w3, w2, norm_final_g):
    depth = w_in.shape[0]
    bsz, seq, d = x.shape
    da_qk = DA_HEADS * 2 * DA_HEAD_DIM
    da_width = DA_HEADS * DA_VAL_DIM
    ml_width = ml_norm_g.shape[-1]
    n_main = 2 * da_qk + da_width + 4 * ml_width
    n_in = w_in.shape[-1]
    cos_t, sin_t = _rope_tables(seq)
    row = lambda v: v.reshape(1, -1).astype(F32)

    for l in range(depth):
        lam_init = 0.8 - 0.6 * math.exp(-0.3 * l)
        w_pad = jnp.pad(w_in[l], ((0, 0), (0, n_main + LANE - n_in))).astype(BF16)
        qt, vt, mvt, mot, proj, gcol, grow = _inproj(x, row(norm_mix_g[l]), w_pad, cos_t, sin_t)

        yda = _diffattn(qt, proj, vt, da_lambda[l].astype(F32), da_subln_g[l].astype(F32).reshape(-1, 1), lam_init)

        gb = ml_gate_b[l].astype(F32).reshape(1, 2 * ML_HEADS)
        gb_col = jnp.pad(gb, ((0, 0), (0, LANE - 2 * ML_HEADS)))
        gb_row = gb.reshape(2 * ML_HEADS, 1)
        ng_b = jnp.broadcast_to(ml_norm_g[l].astype(F32)[:, None], (ml_width, LANE))
        yml = _mlstm(proj, mvt, mot, gcol, grow, ml_conv_w[l].reshape(ML_CONV, 2 * ml_width).astype(F32),
                     row(ml_conv_b[l]), gb_col, gb_row, ng_b, da_qk)

        kmem, vmem = _memkv(mem, row(norm_mem_g[l]), w_xk[l].astype(BF16), w_xv[l].astype(BF16))

        wr = jnp.concatenate([w_router_group[l], w_router_expert[l]], axis=1).astype(F32).T
        wr = jnp.pad(wr, ((0, ROUTER_ROWS - wr.shape[0]), (0, 0)))
        wr_hi = wr.astype(BF16)
        wr_lo = (wr - wr_hi.astype(F32)).astype(BF16)
        br = jnp.pad(jnp.concatenate([b_router_group[l], b_router_expert[l]]).astype(F32),
                     (0, ROUTER_ROWS - N_GROUPS - N_EXPERTS)).reshape(ROUTER_ROWS, 1)
        wo = w_out[l].astype(BF16)
        x2, route = _mix(x, yda, yml, wo[:da_width], wo[da_width:], row(norm_x_g[l]), w_xq[l].astype(BF16),
                         kmem, vmem, w_xo[l].astype(BF16), row(norm_ffn_g[l]), wr_hi, wr_lo, br)

        tt = min(MOE_TILE, seq)
        x = _moe(*_sort_assignments(route, tt), x2.reshape(bsz * seq // tt, tt * d // LANE, LANE),
                 row(norm_ffn_g[l]), row(norm_final_g),
                 w1[l].astype(BF16), w3[l].astype(BF16), w2[l].astype(BF16),
                 final_norm=(l + 1 == depth)).reshape(bsz, seq, d)
    return x
```

```python
import functools
import math

import jax
import jax.numpy as jnp
from jax import lax
from jax.experimental import pallas as pl
from jax.experimental.pallas import tpu as pltpu

F32 = jnp.float32
BF16 = jnp.bfloat16

LANE = 128
SUBLANE = 8
VMEM_LIMIT_BYTES = 56 * 1024 * 1024

EPS = 1e-6
ROPE_THETA = 10000.0
DA_HEADS = 4
DA_HEAD_DIM = 64
DA_VAL_DIM = 2 * DA_HEAD_DIM
ML_HEADS = 4
ML_CONV = 4
X_HEADS = 4
N_GROUPS = 4
EXPERTS_PER_GROUP = 8
N_EXPERTS = N_GROUPS * EXPERTS_PER_GROUP
ROUTER_ROWS = 40

ROW_TILE = 512
DA_HEADS_PER_STEP = 4
ML_CHUNK = 256
ML_BATCH_PER_STEP = 2
MOE_TILE = 4096
MOE_ROWS = 128
MOE_FIRST_ROWS = 288
MOE_WEIGHT_SLOTS = 3

NEG_INF = float("-inf")
LOG2_E = math.log2(math.e)


def _rms(x, g):
    return x * lax.rsqrt(jnp.mean(x * x, axis=-1, keepdims=True) + EPS) * g


def _dot(a, b):
    return jnp.dot(a, b, preferred_element_type=F32)


def _dot_nt(a, b):
    return lax.dot_general(a, b, (((1,), (1,)), ((), ())), preferred_element_type=F32)


def _dot_tn(a, b):
    return lax.dot_general(a, b, (((0,), (0,)), ((), ())), preferred_element_type=F32)


def _split_bf16(x):
    hi = x.astype(BF16)
    lo = (x - hi.astype(F32)).astype(BF16)
    return hi, lo


def _params(*semantics):
    return pltpu.CompilerParams(dimension_semantics=semantics, vmem_limit_bytes=VMEM_LIMIT_BYTES)


def _inproj_body(x_ref, g_ref, w_ref, cos_ref, sin_ref, qt_ref, vt_ref, mvt_ref, mot_ref, proj_ref, gcol_ref,
                 grow_ref):
    hb = _rms(x_ref[0], g_ref[...]).astype(BF16)
    cos = cos_ref[...]
    sin = sin_ref[...]
    lane = lax.broadcasted_iota(jnp.int32, cos.shape, 1)
    first_half = (lane % DA_HEAD_DIM) < (DA_HEAD_DIM // 2)
    n_main = w_ref.shape[1] - LANE
    wide = 4 * LANE
    for c in range(n_main // LANE):
        if c % (wide // LANE) == 0:
            acc = _dot(hb, w_ref[:, c * LANE:c * LANE + wide])
        t = acc[:, (c * LANE) % wide:(c * LANE) % wide + LANE]
        if c < 2 * DA_HEADS:
            rot = jnp.where(first_half, pltpu.roll(t, LANE - DA_HEAD_DIM // 2, 1),
                            pltpu.roll(t, DA_HEAD_DIM // 2, 1))
            t = t * cos + rot * sin
        if c < DA_HEADS:
            qt_ref[0, 0, c * LANE:(c + 1) * LANE, :] = (t * (DA_HEAD_DIM ** -0.5 * LOG2_E)).T.astype(BF16)
        elif c < 2 * DA_HEADS:
            proj_ref[0, :, (c - DA_HEADS) * LANE:(c - DA_HEADS + 1) * LANE] = t.astype(BF16)
        elif c < 3 * DA_HEADS:
            vt_ref[0, 0, (c - 2 * DA_HEADS) * LANE:(c - 2 * DA_HEADS + 1) * LANE, :] = t.T.astype(BF16)
        elif c < 3 * DA_HEADS + 2 * ML_HEADS:
            proj_ref[0, :, (c - 2 * DA_HEADS) * LANE:(c - 2 * DA_HEADS + 1) * LANE] = t.astype(BF16)
        elif c < 3 * DA_HEADS + 3 * ML_HEADS:
            m = c - 3 * DA_HEADS - 2 * ML_HEADS
            mvt_ref[0, 0, m * LANE:(m + 1) * LANE, :] = t.T.astype(BF16)
        else:
            m = c - 3 * DA_HEADS - 3 * ML_HEADS
            mot_ref[0, 0, m * LANE:(m + 1) * LANE, :] = t.T.astype(BF16)
    gates = _dot(hb, w_ref[:, n_main:n_main + LANE])
    gcol_ref[0] = gates
    grow_ref[0] = gates.T[:SUBLANE, :]


def _inproj(x, g, w_pad, cos_t, sin_t):
    bsz, seq, d = x.shape
    tm = min(ROW_TILE, seq)
    nt = seq // tm
    da_w = DA_HEADS * DA_VAL_DIM
    ml_w = ML_HEADS * LANE
    n_proj = w_pad.shape[1] - LANE - 2 * da_w - 2 * ml_w
    return pl.pallas_call(
        _inproj_body,
        grid=(bsz, nt),
        in_specs=[
            pl.BlockSpec((1, tm, d), lambda b, i: (b, i, 0)),
            pl.BlockSpec((1, d), lambda b, i: (0, 0)),
            pl.BlockSpec(w_pad.shape, lambda b, i: (0, 0)),
            pl.BlockSpec((tm, LANE), lambda b, i: (i, 0)),
            pl.BlockSpec((tm, LANE), lambda b, i: (i, 0)),
        ],
        out_specs=[
            pl.BlockSpec((1, 1, da_w, tm), lambda b, i: (b, i, 0, 0)),
            pl.BlockSpec((1, 1, da_w, tm), lambda b, i: (b, i, 0, 0)),
            pl.BlockSpec((1, 1, ml_w, tm), lambda b, i: (b, i, 0, 0)),
            pl.BlockSpec((1, 1, ml_w, tm), lambda b, i: (b, i, 0, 0)),
            pl.BlockSpec((1, tm, n_proj), lambda b, i: (b, i, 0)),
            pl.BlockSpec((1, tm, LANE), lambda b, i: (b, i, 0)),
            pl.BlockSpec((1, SUBLANE, tm), lambda b, i: (b, 0, i)),
        ],
        out_shape=[
            jax.ShapeDtypeStruct((bsz, nt, da_w, tm), BF16),
            jax.ShapeDtypeStruct((bsz, nt, da_w, tm), BF16),
            jax.ShapeDtypeStruct((bsz, nt, ml_w, tm), BF16),
            jax.ShapeDtypeStruct((bsz, nt, ml_w, tm), BF16),
            jax.ShapeDtypeStruct((bsz, seq, n_proj), BF16),
            jax.ShapeDtypeStruct((bsz, seq, LANE), F32),
            jax.ShapeDtypeStruct((bsz, SUBLANE, seq), F32),
        ],
        compiler_params=_params("parallel", "parallel"),
        name="inproj",
    )(x, g, w_pad, cos_t, sin_t)


def _diffattn_body(qt_ref, k_ref, vt_ref, lam_ref, g_ref, o_ref, m_scr, acc_scr, p_scr, *, lam_init):
    qi = pl.program_id(2)
    tq = qt_ref.shape[3]
    heads = qt_ref.shape[2] // DA_VAL_DIM
    q2t = []
    for g in range(heads):
        qt = qt_ref[0, 0, g * DA_VAL_DIM:(g + 1) * DA_VAL_DIM, :]
        row = lax.broadcasted_iota(jnp.int32, qt.shape, 0)
        zero = jnp.zeros_like(qt)
        q2t.append(jnp.concatenate([jnp.where(row < DA_HEAD_DIM, qt, zero),
                                    jnp.where(row >= DA_HEAD_DIM, qt, zero)], axis=1))
    pad_rows = acc_scr.shape[0] - DA_VAL_DIM
    ones_rows = jnp.where(lax.broadcasted_iota(jnp.int32, (pad_rows, tq), 0) == 0, 1.0, 0.0).astype(BF16)

    m_scr[...] = jnp.full(m_scr.shape, NEG_INF, F32)
    acc_scr[...] = jnp.zeros(acc_scr.shape, F32)

    n_chains = 4
    width = 2 * tq // n_chains
    chains = tuple((g, slice(c * width, (c + 1) * width),
                    slice(g * 2 * tq + c * width, g * 2 * tq + (c + 1) * width))
                   for c in range(n_chains) for g in range(heads))

    def scores(j, masked, prev_values):
        out = []
        for g, local, cols in chains:
            k = k_ref[0, pl.ds(pl.multiple_of(j * tq, tq), tq), g * LANE:(g + 1) * LANE]
            s = _dot(k, q2t[g][:, local])
            pv = values(j - 1, g, cols) if prev_values else None
            if masked:
                key = lax.broadcasted_iota(jnp.int32, s.shape, 0)
                qry = lax.broadcasted_iota(jnp.int32, s.shape, 1) + local.start % tq
                s = jnp.where(key <= qry, s, NEG_INF)
            m_prev = m_scr[:, cols]
            m_new = jnp.maximum(m_prev, jnp.max(s, axis=0, keepdims=True))
            p_scr[:, cols] = jnp.exp2(s - m_new).astype(BF16)
            m_scr[:, cols] = m_new
            out.append((jnp.exp2(m_prev - m_new), pv))
        return out

    def values(j, g, cols):
        v_aug = jnp.concatenate([vt_ref[0, j, g * DA_VAL_DIM:(g + 1) * DA_VAL_DIM, :], ones_rows], axis=0)
        return _dot(v_aug, p_scr[:, cols])

    def step(j, masked):
        for (_, _, cols), (alpha, pv) in zip(chains, scores(j, masked, True)):
            acc_scr[:, cols] = alpha * (acc_scr[:, cols] + pv)

    def full_step_pair(i, carry):
        step(1 + 2 * i, False)
        step(2 + 2 * i, False)
        return carry

    @pl.when(qi == 0)
    def _():
        scores(0, True, False)

    @pl.when(qi > 0)
    def _():
        scores(0, False, False)
        lax.fori_loop(0, (qi - 1) // 2, full_step_pair, 0)

        @pl.when((qi - 1) % 2 == 1)
        def _():
            step(qi - 1, False)

        step(qi, True)

    for g, _, cols in chains:
        acc_scr[:, cols] = acc_scr[:, cols] + values(qi, g, cols)

    lv = lam_ref[...]
    lam = (jnp.exp(jnp.sum(lv[0:1, :] * lv[1:2, :], axis=-1, keepdims=True))
           - jnp.exp(jnp.sum(lv[2:3, :] * lv[3:4, :], axis=-1, keepdims=True)) + lam_init)
    for g in range(heads):
        acc = acc_scr[:, g * 2 * tq:(g + 1) * 2 * tq]
        o = acc[0:DA_VAL_DIM, :] / acc[DA_VAL_DIM:DA_VAL_DIM + 1, :]
        o = o[:, :tq] - lam * o[:, tq:]
        y = o * lax.rsqrt(jnp.mean(o * o, axis=0, keepdims=True) + EPS) * g_ref[...] * (1.0 - lam_init)
        o_ref[0, :, g * LANE:(g + 1) * LANE] = y.T.astype(o_ref.dtype)


def _diffattn(qt, proj, vt, da_lambda, subln_g_col, lam_init):
    bsz, nt, _, tq = qt.shape
    seq = nt * tq
    hp = DA_HEADS_PER_STEP
    body = functools.partial(_diffattn_body, lam_init=lam_init)
    return pl.pallas_call(
        body,
        grid=(bsz, DA_HEADS // hp, nt),
        in_specs=[
            pl.BlockSpec((1, 1, hp * DA_VAL_DIM, tq), lambda b, h, i: (b, i, h, 0)),
            pl.BlockSpec((1, seq, hp * LANE), lambda b, h, i: (b, 0, h)),
            pl.BlockSpec((1, nt, hp * DA_VAL_DIM, tq), lambda b, h, i: (b, 0, h, 0)),
            pl.BlockSpec(da_lambda.shape, lambda b, h, i: (0, 0)),
            pl.BlockSpec((DA_VAL_DIM, 1), lambda b, h, i: (0, 0)),
        ],
        out_specs=pl.BlockSpec((1, tq, hp * LANE), lambda b, h, i: (b, i, h)),
        out_shape=jax.ShapeDtypeStruct((bsz, seq, DA_HEADS * DA_VAL_DIM), BF16),
        scratch_shapes=[
            pltpu.VMEM((1, hp * 2 * tq), F32),
            pltpu.VMEM((DA_VAL_DIM + 2 * SUBLANE, hp * 2 * tq), F32),
            pltpu.VMEM((tq, hp * 2 * tq), BF16),
        ],
        compiler_params=_params("parallel", "parallel", "arbitrary"),
        name="diffattn",
    )(qt, proj, vt, da_lambda, subln_g_col)


def _log_sigmoid(x):
    return jnp.minimum(x, 0.0) - jnp.log(1.0 + jnp.exp(-jnp.abs(x)))


def _sigmoid(x):
    return 1.0 / (1.0 + jnp.exp(-x))


def _mlstm_body(q_ref, k_ref, vt_ref, ot_ref, gcol_ref, grow_ref, cw_ref, cb_ref, gbcol_ref, gbrow_ref, ng_ref,
                y_ref, xbuf, c_scr, m_scr):
    @pl.when(pl.program_id(1) == 0)
    def _():
        xbuf[...] = jnp.zeros(xbuf.shape, xbuf.dtype)
        c_scr[...] = jnp.zeros(c_scr.shape, F32)
        m_scr[...] = jnp.zeros(m_scr.shape, F32)

    for bb in range(q_ref.shape[0]):
        _mlstm_row(bb, q_ref, k_ref, vt_ref, ot_ref, gcol_ref, grow_ref, cw_ref, cb_ref, gbcol_ref, gbrow_ref,
                   ng_ref, y_ref, xbuf, c_scr, m_scr)


def _mlstm_row(bb, q_ref, k_ref, vt_ref, ot_ref, gcol_ref, grow_ref, cw_ref, cb_ref, gbcol_ref, gbrow_ref, ng_ref,
               y_ref, xbuf, c_scr, m_scr):
    L = q_ref.shape[1]
    width = q_ref.shape[2]
    dh = width // ML_HEADS
    tail = xbuf.shape[1]
    aug = c_scr.shape[1]

    u = jnp.concatenate([q_ref[bb], k_ref[bb]], axis=1)
    u_ext = jnp.concatenate([xbuf[bb], u], axis=0)
    src = lax.broadcasted_iota(jnp.int32, (L, L + tail), 1) - lax.broadcasted_iota(jnp.int32, (L, L + tail), 0)
    conv = cb_ref[...] + cw_ref[ML_CONV - 1:ML_CONV, :] * u.astype(F32)
    for back in range(1, ML_CONV):
        shift = jnp.where(src == tail - back, 1.0, 0.0).astype(BF16)
        conv = conv + cw_ref[ML_CONV - 1 - back:ML_CONV - back, :] * _dot(shift, u_ext)
    xbuf[bb] = u[L - tail:, :]
    qk = conv * _sigmoid(conv)

    gc = gcol_ref[bb] + gbcol_ref[...]
    gr = grow_ref[bb] + gbrow_ref[...]
    row_id = lax.broadcasted_iota(jnp.int32, (L, L), 0)
    col_id = lax.broadcasted_iota(jnp.int32, (L, L), 1)
    causal = row_id <= col_id
    tri = jnp.where(col_id <= row_id, 1.0, 0.0).astype(BF16)
    tri_t = jnp.where(causal, 1.0, 0.0).astype(BF16)
    lfc_hi, lfc_lo = _split_bf16(_log_sigmoid(gc))
    lfr_hi, lfr_lo = _split_bf16(_log_sigmoid(gr))
    a_c = _dot(tri, lfc_hi) + _dot(tri, lfc_lo)
    a_r = _dot(lfr_hi, tri_t) + _dot(lfr_lo, tri_t)

    ones_rows = jnp.where(lax.broadcasted_iota(jnp.int32, (aug - dh, L), 0) == 0, 1.0, 0.0).astype(BF16)

    for h in range(ML_HEADS):
        fh = ML_HEADS + h
        st = bb * ML_HEADS + h
        m_prev = m_scr[st:st + 1, 0:1]
        a_row = a_r[fh:fh + 1, :]
        ig_row = gr[h:h + 1, :]
        nb = jnp.where(causal, gc[:, h:h + 1] - a_c[:, fh:fh + 1], NEG_INF)
        m_row = a_row + jnp.maximum(m_prev, jnp.max(nb, axis=0, keepdims=True))
        w_intra = jnp.exp(nb + (a_row - m_row))
        w_inter = jnp.exp(a_row + m_prev - m_row)

        qh = qk[:, h * dh:(h + 1) * dh].astype(BF16)
        kh = (qk[:, width + h * dh:width + (h + 1) * dh] * (dh ** -0.5)).astype(BF16)
        p_t = (_dot_nt(kh, qh) * w_intra).astype(BF16)
        v_aug = jnp.concatenate([vt_ref[bb, 0, h * dh:(h + 1) * dh, :], ones_rows], axis=0)
        c_t = c_scr[st]
        num = w_inter * _dot_nt(c_t.astype(BF16), qh) + _dot(v_aug, p_t)
        scale = 1.0 / jnp.maximum(jnp.abs(num[dh:dh + 1, :]), jnp.exp(-m_row))
        hid = num[0:dh, :] * scale

        g_last = a_row[:, L - 1:L]
        log_w = g_last - a_row + ig_row
        m_new = jnp.maximum(g_last + m_prev, jnp.max(log_w, axis=1, keepdims=True))
        w_state = jnp.exp(log_w - m_new)
        decay = jnp.exp(g_last + m_prev - m_new)
        wv = (w_state * v_aug.astype(F32)).astype(BF16)
        c_scr[st] = decay * c_t + _dot(wv, kh)
        m_scr[st:st + 1, :] = jnp.broadcast_to(m_new, (1, m_scr.shape[1]))

        mu = jnp.mean(hid, axis=0, keepdims=True)
        cen = hid - mu
        var = jnp.mean(cen * cen, axis=0, keepdims=True)
        g_b = ng_ref[h * dh:(h + 1) * dh, :]
        hn = cen * lax.rsqrt(var + EPS) * jnp.concatenate([g_b] * (L // LANE), axis=1)
        gate = _sigmoid(ot_ref[bb, 0, h * dh:(h + 1) * dh, :].astype(F32))
        y_ref[bb, :, h * dh:(h + 1) * dh] = (gate * hn).T.astype(y_ref.dtype)


def _mlstm(proj, mvt, mot, gcol, grow, conv_w, conv_b, gb_col, gb_row, norm_g_b, col0):
    bsz, seq, _ = proj.shape
    width = norm_g_b.shape[0]
    dh = width // ML_HEADS
    L = min(ML_CHUNK, seq)
    tm = mvt.shape[-1]
    per = tm // L
    blk0 = col0 // width
    nb = ML_BATCH_PER_STEP
    spec = lambda k: pl.BlockSpec((nb, L, width), lambda b, c: (b, c, blk0 + k))
    tspec = pl.BlockSpec((nb, 1, width, L), lambda b, c: (b, c // per, 0, c % per))
    return pl.pallas_call(
        _mlstm_body,
        grid=(bsz // nb, seq // L),
        in_specs=[
            spec(0), spec(1), tspec, tspec,
            pl.BlockSpec((nb, L, LANE), lambda b, c: (b, c, 0)),
            pl.BlockSpec((nb, SUBLANE, L), lambda b, c: (b, 0, c)),
            pl.BlockSpec(conv_w.shape, lambda b, c: (0, 0)),
            pl.BlockSpec(conv_b.shape, lambda b, c: (0, 0)),
            pl.BlockSpec(gb_col.shape, lambda b, c: (0, 0)),
            pl.BlockSpec(gb_row.shape, lambda b, c: (0, 0)),
            pl.BlockSpec(norm_g_b.shape, lambda b, c: (0, 0)),
        ],
        out_specs=pl.BlockSpec((nb, L, width), lambda b, c: (b, c, 0)),
        out_shape=jax.ShapeDtypeStruct((bsz, seq, width), BF16),
        scratch_shapes=[
            pltpu.VMEM((nb, 2 * SUBLANE, 2 * width), BF16),
            pltpu.VMEM((nb * ML_HEADS, dh + 2 * SUBLANE, dh), F32),
            pltpu.VMEM((nb * ML_HEADS, LANE), F32),
        ],
        compiler_params=_params("parallel", "arbitrary"),
        name="mlstm",
    )(proj, proj, mvt, mot, gcol, grow, conv_w, conv_b, gb_col, gb_row, norm_g_b)


def _memkv_body(mem_ref, g_ref, wk_ref, wv_ref, k_ref, v_ref):
    mb = _rms(mem_ref[0], g_ref[...]).astype(BF16)
    k_ref[0] = _dot(mb, wk_ref[...]).astype(BF16)
    v_ref[0] = _dot(mb, wv_ref[...]).astype(BF16)


def _memkv(mem, g, wk, wv):
    bsz, mlen, d = mem.shape
    full = lambda a: pl.BlockSpec(a.shape, lambda b: (0,) * a.ndim)
    blk = pl.BlockSpec((1, mlen, d), lambda b: (b, 0, 0))
    return pl.pallas_call(
        _memkv_body,
        grid=(bsz,),
        in_specs=[blk, full(g), full(wk), full(wv)],
        out_specs=[blk, blk],
        out_shape=[jax.ShapeDtypeStruct((bsz, mlen, d), BF16)] * 2,
        compiler_params=_params("parallel"),
        name="memkv",
    )(mem, g, wk, wv)


def _mix_body(x_ref, yda_ref, yml_ref, wo1_ref, wo2_ref, gx_ref, wq_ref, km_ref, vm_ref, wxo_ref, gf_ref,
              wrh_ref, wrl_ref, br_ref, x2_ref, route_ref):
    x1 = x_ref[0] + _dot(yda_ref[0], wo1_ref[...]) + _dot(yml_ref[0], wo2_ref[...])

    d = x1.shape[-1]
    dh = d // X_HEADS
    q = _dot(_rms(x1, gx_ref[...]).astype(BF16), wq_ref[...]).astype(BF16)
    heads = []
    for h in range(X_HEADS):
        sl = slice(h * dh, (h + 1) * dh)
        s = _dot_nt(q[:, sl], km_ref[0, :, sl]) * (dh ** -0.5)
        p = jnp.exp(s - jnp.max(s, axis=-1, keepdims=True))
        o = _dot(p.astype(BF16), vm_ref[0, :, sl]) / jnp.sum(p, axis=-1, keepdims=True)
        heads.append(o.astype(BF16))
    x2 = x1 + _dot(jnp.concatenate(heads, axis=1), wxo_ref[...])
    _to_token_tiles(x2_ref.at[0], 0, x2)

    h3 = _rms(x2, gf_ref[...])
    h_hi, h_lo = _split_bf16(h3)
    logits = (_dot_nt(wrh_ref[...], h_hi) + _dot_nt(wrh_ref[...], h_lo) + _dot_nt(wrl_ref[...], h_hi)
              + br_ref[...])
    tm = logits.shape[1]
    gl = logits[0:N_GROUPS, :]
    g_iota = lax.broadcasted_iota(jnp.int32, gl.shape, 0)
    g_max = jnp.max(gl, axis=0, keepdims=True)
    g_idx = jnp.min(jnp.where(gl == g_max, g_iota, N_GROUPS), axis=0, keepdims=True)
    g_w = 1.0 / jnp.sum(jnp.exp(gl - g_max), axis=0, keepdims=True)
    el = jnp.zeros((EXPERTS_PER_GROUP, tm), F32)
    for g in range(N_GROUPS):
        lo = N_GROUPS + g * EXPERTS_PER_GROUP
        el = jnp.where(g_idx == g, logits[lo:lo + EXPERTS_PER_GROUP, :], el)
    e_iota = lax.broadcasted_iota(jnp.int32, el.shape, 0)
    e1 = jnp.max(el, axis=0, keepdims=True)
    i1 = jnp.min(jnp.where(el == e1, e_iota, EXPERTS_PER_GROUP), axis=0, keepdims=True)
    el2 = jnp.where(e_iota == i1, NEG_INF, el)
    e2 = jnp.max(el2, axis=0, keepdims=True)
    i2 = jnp.min(jnp.where(el2 == e2, e_iota, EXPERTS_PER_GROUP), axis=0, keepdims=True)
    p2 = jnp.exp(e2 - e1)
    w1 = g_w / (1.0 + p2)
    w2 = g_w * p2 / (1.0 + p2)
    base = g_idx * EXPERTS_PER_GROUP
    zero = jnp.zeros((1, tm), F32)
    route_ref[0] = jnp.concatenate(
        [(base + i1).astype(F32), (base + i2).astype(F32), w1, w2, zero, zero, zero, zero], axis=0)


def _mix(x, yda, yml, wo1, wo2, gx, wq, kmem, vmem, wxo, gf, wr_hi, wr_lo, br):
    bsz, seq, d = x.shape
    tm = min(ROW_TILE, seq)
    mlen = kmem.shape[1]
    full = lambda a: pl.BlockSpec(a.shape, lambda b, i: (0,) * a.ndim)
    tok = lambda w: pl.BlockSpec((1, tm, w), lambda b, i: (b, i, 0))
    memspec = pl.BlockSpec((1, mlen, d), lambda b, i: (b, 0, 0))
    return pl.pallas_call(
        _mix_body,
        grid=(bsz, seq // tm),
        in_specs=[tok(d), tok(yda.shape[-1]), tok(yml.shape[-1]), full(wo1), full(wo2), full(gx), full(wq),
                  memspec, memspec, full(wxo), full(gf), full(wr_hi), full(wr_lo), full(br)],
        out_specs=[pl.BlockSpec((1, tm * SUBLANE, LANE), lambda b, i: (b, i, 0)),
                   pl.BlockSpec((1, SUBLANE, tm), lambda b, i: (b, 0, i))],
        out_shape=[
            jax.ShapeDtypeStruct((bsz, seq * d // LANE, LANE), F32),
            jax.ShapeDtypeStruct((bsz, SUBLANE, seq), F32),
        ],
        compiler_params=_params("parallel", "parallel"),
        name="mix_xattn_router",
    )(x, yda, yml, wo1, wo2, gx, wq, kmem, vmem, wxo, gf, wr_hi, wr_lo, br)


def _to_token_tiles(ref, first_row, dense):
    n, d = dense.shape
    for c in range(d // LANE):
        ref[pl.ds(first_row * SUBLANE + c, n, stride=SUBLANE), :] = dense[:, c * LANE:(c + 1) * LANE]


def _from_token_tiles(ref, first_row, n):
    return jnp.concatenate(
        [ref[pl.ds(first_row * SUBLANE + c, n, stride=SUBLANE), :] for c in range(SUBLANE)], axis=1)


def _moe_body(start_ref, cnt_ref, x_hbm, tok_hbm, wt_hbm, gf_ref, gfin_ref, w1_hbm, w3_hbm, w2_hbm, out_hbm,
              acc, h3, xg, yt, stage, tok_s, wt_s, w1_buf, w3_buf, w2_buf, sems, load_sems, store_sems, w_sems,
              *, final_norm):
    depth = w1_buf.shape[0]
    step_id = pl.program_id(0) * pl.num_programs(1) + pl.program_id(1)
    n_steps = pl.num_programs(0) * pl.num_programs(1)

    def weight_copies(g):
        expert = g % pl.num_programs(1)
        slot = g % depth
        return [pltpu.make_async_copy(src.at[expert], dst.at[slot], w_sems.at[slot * 3 + k])
                for k, (src, dst) in enumerate(((w1_hbm, w1_buf), (w3_hbm, w3_buf), (w2_hbm, w2_buf)))]

    @pl.when(step_id == 0)
    def _():
        for ahead in range(depth - 1):
            for cp in weight_copies(step_id + ahead):
                cp.start()

    @pl.when(step_id + depth - 1 < n_steps)
    def _():
        for cp in weight_copies(step_id + depth - 1):
            cp.start()

    for cp in weight_copies(step_id):
        cp.wait()
    w_slot = step_id % depth
    tile = pl.program_id(0)
    e = pl.program_id(1)
    tt = h3.shape[0] // SUBLANE
    chunk_rows = stage.shape[1]
    group = SUBLANE

    def row_tile(ref, row0):
        return ref.at[pl.ds(pl.multiple_of(row0, SUBLANE), SUBLANE), :]

    def token_tile(ref, r):
        return ref.at[pl.ds(r * SUBLANE, SUBLANE), :]

    spare_row = tt * SUBLANE

    n_chunks = tt // chunk_rows
    chunk_tiles = chunk_rows * SUBLANE

    def load_chunk(i):
        rows = pl.ds(pl.multiple_of(i * chunk_tiles, chunk_tiles), chunk_tiles)
        return pltpu.make_async_copy(x_hbm.at[tile, rows, :], acc.at[rows, :], load_sems.at[i])

    @pl.when(e == 0)
    def _():
        lists = [pltpu.make_async_copy(tok_hbm.at[tile], tok_s, sems.at[0]),
                 pltpu.make_async_copy(wt_hbm.at[tile], wt_s, sems.at[1])]
        for cp in lists:
            cp.start()
        for i in range(n_chunks):
            load_chunk(i).start()
        token_tile(acc, tt)[...] = jnp.zeros((SUBLANE, LANE), F32)

        def ffn_norm(i, c):
            load_chunk(i).wait()
            r0 = i * chunk_rows
            _to_token_tiles(h3, r0, _rms(_from_token_tiles(acc, r0, chunk_rows), gf_ref[...]))
            return c
        lax.fori_loop(0, n_chunks, ffn_norm, 0)
        for cp in lists:
            cp.wait()

    start = start_ref[tile * N_EXPERTS + e]
    count = cnt_ref[tile * N_EXPERTS + e]

    def run_block(first, cnt, rows):
        idx0 = start + first
        for r in range(rows):
            token_tile(xg, r)[...] = row_tile(h3, tok_s[idx0 + r])[...]

        xb = _from_token_tiles(xg, 0, rows).astype(BF16)
        a = _dot(xb, w1_buf[w_slot])
        b = _dot(xb, w3_buf[w_slot])
        hmid = (a * _sigmoid(a) * b).astype(BF16)
        _to_token_tiles(yt, 0, _dot(hmid, w2_buf[w_slot]))

        for g0 in range(0, rows, group):
            dst = [jnp.where(g0 + u < cnt, tok_s[idx0 + g0 + u], spare_row) for u in range(group)]
            new = [row_tile(acc, dst[u])[...] + wt_s[idx0 + g0 + u] * token_tile(yt, g0 + u)[...]
                   for u in range(group)]
            for u in range(group):
                row_tile(acc, dst[u])[...] = new[u]

    big = xg.shape[0] // SUBLANE
    small = MOE_ROWS

    @pl.when(count > 0)
    def _():
        run_block(0, jnp.minimum(big, count), big)

    def small_block(i, c):
        first = big + i * small
        run_block(first, jnp.minimum(small, count - first), small)
        return c

    lax.fori_loop(0, (jnp.maximum(count - big, 0) + small - 1) // small, small_block, 0)

    @pl.when(e == pl.num_programs(1) - 1)
    def _():
        n_slots = stage.shape[0]

        def store_chunk(i, slot):
            r0 = pl.multiple_of(i * chunk_rows, chunk_rows)
            return pltpu.make_async_copy(stage.at[slot], out_hbm.at[tile, pl.ds(r0, chunk_rows), :],
                                         store_sems.at[slot])

        def write_back(p, c):
            for slot in range(n_slots):
                i = p * n_slots + slot

                @pl.when(p > 0)
                def _():
                    store_chunk(i - n_slots, slot).wait()

                rows = _from_token_tiles(acc, i * chunk_rows, chunk_rows)
                stage[slot] = _rms(rows, gfin_ref[...]) if final_norm else rows
                store_chunk(i, slot).start()
            return c
        lax.fori_loop(0, n_chunks // n_slots, write_back, 0)
        for slot in range(n_slots):
            store_chunk(n_chunks - n_slots + slot, slot).wait()


def _moe(start, cnt, tok, wt, x_token_tiles, gf, gfin, w1, w3, w2, final_norm):
    ntiles, tile_rows, _ = x_token_tiles.shape
    tt = tile_rows // SUBLANE
    d = SUBLANE * LANE
    de = w1.shape[-1]
    seg = tok.shape[1]
    body = functools.partial(_moe_body, final_norm=final_norm)
    hbm = pl.BlockSpec(memory_space=pl.ANY)
    depth = MOE_WEIGHT_SLOTS
    grid_spec = pltpu.PrefetchScalarGridSpec(
        num_scalar_prefetch=2,
        grid=(ntiles, N_EXPERTS),
        in_specs=[
            hbm, hbm, hbm,
            pl.BlockSpec((1, d), lambda t, e, *_: (0, 0)),
            pl.BlockSpec((1, d), lambda t, e, *_: (0, 0)),
            hbm, hbm, hbm,
        ],
        out_specs=hbm,
        scratch_shapes=[
            pltpu.VMEM((tile_rows + SUBLANE, LANE), F32),
            pltpu.VMEM((tile_rows, LANE), F32),
            pltpu.VMEM((MOE_FIRST_ROWS * SUBLANE, LANE), F32),
            pltpu.VMEM((MOE_FIRST_ROWS * SUBLANE, LANE), F32),
            pltpu.VMEM((2, 2 * MOE_ROWS, d), F32),
            pltpu.SMEM((seg,), jnp.int32),
            pltpu.SMEM((seg,), F32),
            pltpu.VMEM((depth, d, de), BF16),
            pltpu.VMEM((depth, d, de), BF16),
            pltpu.VMEM((depth, de, d), BF16),
            pltpu.SemaphoreType.DMA((2,)),
            pltpu.SemaphoreType.DMA((tt // (2 * MOE_ROWS),)),
            pltpu.SemaphoreType.DMA((2,)),
            pltpu.SemaphoreType.DMA((3 * depth,)),
        ],
    )
    return pl.pallas_call(
        body,
        grid_spec=grid_spec,
        out_shape=jax.ShapeDtypeStruct((ntiles, tt, d), F32),
        compiler_params=_params("arbitrary", "arbitrary"),
        name="moe",
    )(start, cnt, x_token_tiles, tok, wt, gf, gfin, w1, w3, w2)


def _rope_tables(seq):
    half = DA_HEAD_DIM // 2
    inv_freq = ROPE_THETA ** (-jnp.arange(0, DA_HEAD_DIM, 2, dtype=F32) / DA_HEAD_DIM)
    ang = jnp.arange(seq, dtype=F32)[:, None] * inv_freq[None, :]
    cos, sin = jnp.cos(ang), jnp.sin(ang)
    reps = LANE // half
    cos_t = jnp.tile(cos, (1, reps))
    sin_t = jnp.tile(jnp.concatenate([-sin, sin], axis=1), (1, reps // 2))
    return cos_t, sin_t


def _sort_assignments(route, tt):
    bsz, _, seq = route.shape
    per = seq // tt
    ntiles = bsz * per
    tile = lambda a: a.reshape(bsz, TOP_K_ROWS, per, tt).transpose(0, 2, 1, 3).reshape(ntiles, TOP_K_ROWS * tt)
    keys = tile(route[:, 0:TOP_K_ROWS, :]).astype(jnp.int32)
    wts = tile(route[:, TOP_K_ROWS:2 * TOP_K_ROWS, :])
    order = jnp.argsort(keys, axis=-1).astype(jnp.int32)
    tok = (order % tt) * SUBLANE
    wt = jnp.take_along_axis(wts, order, axis=-1)
    counts = jnp.sum(keys[:, :, None] == jnp.arange(N_EXPERTS, dtype=jnp.int32)[None, None, :], axis=1,
                     dtype=jnp.int32)
    start = jnp.cumsum(counts, axis=-1, dtype=jnp.int32) - counts
    over = ((0, 0), (0, MOE_FIRST_ROWS))
    return start.reshape(-1), counts.reshape(-1), jnp.pad(tok, over), jnp.pad(wt, over)


TOP_K_ROWS = 2


def kernel(x, mem, norm_mix_g, w_in, w_out, da_lambda, da_subln_g, ml_conv_w, ml_conv_b, ml_gate_b, ml_norm_g, norm_x_g, norm_mem_g, w_xq, w_xk, w_xv, w_xo, norm_ffn_g, w_router_group, b_router_group, w_router_expert, b_router_expert, w1, w3, w2, norm_final_g):
    depth = w_in.shape[0]
    bsz, seq, d = x.shape
    da_qk = DA_HEADS * 2 * DA_HEAD_DIM
    da_width = DA_HEADS * DA_VAL_DIM
    ml_width = ml_norm_g.shape[-1]
    n_main = 2 * da_qk + da_width + 4 * ml_width
    n_in = w_in.shape[-1]
    cos_t, sin_t = _rope_tables(seq)
    row = lambda v: v.reshape(1, -1).astype(F32)

    for l in range(depth):
        lam_init = 0.8 - 0.6 * math.exp(-0.3 * l)
        w_pad = jnp.pad(w_in[l], ((0, 0), (0, n_main + LANE - n_in))).astype(BF16)
        qt, vt, mvt, mot, proj, gcol, grow = _inproj(x, row(norm_mix_g[l]), w_pad, cos_t, sin_t)

        yda = _diffattn(qt, proj, vt, da_lambda[l].astype(F32), da_subln_g[l].astype(F32).reshape(-1, 1), lam_init)

        gb = ml_gate_b[l].astype(F32).reshape(1, 2 * ML_HEADS)
        gb_col = jnp.pad(gb, ((0, 0), (0, LANE - 2 * ML_HEADS)))
        gb_row = gb.reshape(2 * ML_HEADS, 1)
        ng_b = jnp.broadcast_to(ml_norm_g[l].astype(F32)[:, None], (ml_width, LANE))
        yml = _mlstm(proj, mvt, mot, gcol, grow, ml_conv_w[l].reshape(ML_CONV, 2 * ml_width).astype(F32),
                     row(ml_conv_b[l]), gb_col, gb_row, ng_b, da_qk)

        kmem, vmem = _memkv(mem, row(norm_mem_g[l]), w_xk[l].astype(BF16), w_xv[l].astype(BF16))

        wr = jnp.concatenate([w_router_group[l], w_router_expert[l]], axis=1).astype(F32).T
        wr = jnp.pad(wr, ((0, ROUTER_ROWS - wr.shape[0]), (0, 0)))
        wr_hi = wr.astype(BF16)
        wr_lo = (wr - wr_hi.astype(F32)).astype(BF16)
        br = jnp.pad(jnp.concatenate([b_router_group[l], b_router_expert[l]]).astype(F32),
                     (0, ROUTER_ROWS - N_GROUPS - N_EXPERTS)).reshape(ROUTER_ROWS, 1)
        wo = w_out[l].astype(BF16)
        x2, route = _mix(x, yda, yml, wo[:da_width], wo[da_width:], row(norm_x_g[l]), w_xq[l].astype(BF16),
                         kmem, vmem, w_xo[l].astype(BF16), row(norm_ffn_g[l]), wr_hi, wr_lo, br)

        tt = min(MOE_TILE, seq)
        x = _moe(*_sort_assignments(route, tt), x2.reshape(bsz * seq // tt, tt * d // LANE, LANE),
                 row(norm_ffn_g[l]), row(norm_final_g),
                 w1[l].astype(BF16), w3[l].astype(BF16), w2[l].astype(BF16),
                 final_norm=(l + 1 == depth)).reshape(bsz, seq, d)
    return x
```

```python
import functools
import math

import jax
import jax.numpy as jnp
from jax import lax
from jax.experimental import pallas as pl
from jax.experimental.pallas import tpu as pltpu

F32 = jnp.float32
BF16 = jnp.bfloat16

LANE = 128
SUBLANE = 8
VMEM_LIMIT_BYTES = 56 * 1024 * 1024

EPS = 1e-6
ROPE_THETA = 10000.0
DA_HEADS = 4
DA_HEAD_DIM = 64
DA_VAL_DIM = 2 * DA_HEAD_DIM
ML_HEADS = 4
ML_CONV = 4
X_HEADS = 4
N_GROUPS = 4
EXPERTS_PER_GROUP = 8
N_EXPERTS = N_GROUPS * EXPERTS_PER_GROUP
ROUTER_ROWS = 40

ROW_TILE = 512
DA_HEADS_PER_STEP = 4
ML_CHUNK = 256
ML_BATCH_PER_STEP = 2
MOE_TILE = 4096
MOE_ROWS = 128
MOE_FIRST_ROWS = 288
MOE_WEIGHT_SLOTS = 3

NEG_INF = float("-inf")
LOG2_E = math.log2(math.e)


def _rms(x, g):
    return x * lax.rsqrt(jnp.mean(x * x, axis=-1, keepdims=True) + EPS) * g


def _dot(a, b):
    return jnp.dot(a, b, preferred_element_type=F32)


def _dot_nt(a, b):
    return lax.dot_general(a, b, (((1,), (1,)), ((), ())), preferred_element_type=F32)


def _dot_tn(a, b):
    return lax.dot_general(a, b, (((0,), (0,)), ((), ())), preferred_element_type=F32)


def _split_bf16(x):
    hi = x.astype(BF16)
    lo = (x - hi.astype(F32)).astype(BF16)
    return hi, lo


def _params(*semantics):
    return pltpu.CompilerParams(dimension_semantics=semantics, vmem_limit_bytes=VMEM_LIMIT_BYTES)


def _inproj_body(x_ref, g_ref, w_ref, cos_ref, sin_ref, qt_ref, vt_ref, mvt_ref, mot_ref, proj_ref, gcol_ref,
                 grow_ref):
    hb = _rms(x_ref[0], g_ref[...]).astype(BF16)
    cos = cos_ref[...]
    sin = sin_ref[...]
    lane = lax.broadcasted_iota(jnp.int32, cos.shape, 1)
    first_half = (lane % DA_HEAD_DIM) < (DA_HEAD_DIM // 2)
    n_main = w_ref.shape[1] - LANE
    wide = 4 * LANE
    for c in range(n_main // LANE):
        if c % (wide // LANE) == 0:
            acc = _dot(hb, w_ref[:, c * LANE:c * LANE + wide])
        t = acc[:, (c * LANE) % wide:(c * LANE) % wide + LANE]
        if c < 2 * DA_HEADS:
            rot = jnp.where(first_half, pltpu.roll(t, LANE - DA_HEAD_DIM // 2, 1),
                            pltpu.roll(t, DA_HEAD_DIM // 2, 1))
            t = t * cos + rot * sin
        if c < DA_HEADS:
            qt_ref[0, 0, c * LANE:(c + 1) * LANE, :] = (t * (DA_HEAD_DIM ** -0.5 * LOG2_E)).T.astype(BF16)
        elif c < 2 * DA_HEADS:
            proj_ref[0, :, (c - DA_HEADS) * LANE:(c - DA_HEADS + 1) * LANE] = t.astype(BF16)
        elif c < 3 * DA_HEADS:
            vt_ref[0, 0, (c - 2 * DA_HEADS) * LANE:(c - 2 * DA_HEADS + 1) * LANE, :] = t.T.astype(BF16)
        elif c < 3 * DA_HEADS + 2 * ML_HEADS:
            proj_ref[0, :, (c - 2 * DA_HEADS) * LANE:(c - 2 * DA_HEADS + 1) * LANE] = t.astype(BF16)
        elif c < 3 * DA_HEADS + 3 * ML_HEADS:
            m = c - 3 * DA_HEADS - 2 * ML_HEADS
            mvt_ref[0, 0, m * LANE:(m + 1) * LANE, :] = t.T.astype(BF16)
        else:
            m = c - 3 * DA_HEADS - 3 * ML_HEADS
            mot_ref[0, 0, m * LANE:(m + 1) * LANE, :] = t.T.astype(BF16)
    gates = _dot(hb, w_ref[:, n_main:n_main + LANE])
    gcol_ref[0] = gates
    grow_ref[0] = gates.T[:SUBLANE, :]


def _inproj(x, g, w_pad, cos_t, sin_t):
    bsz, seq, d = x.shape
    tm = min(ROW_TILE, seq)
    nt = seq // tm
    da_w = DA_HEADS * DA_VAL_DIM
    ml_w = ML_HEADS * LANE
    n_proj = w_pad.shape[1] - LANE - 2 * da_w - 2 * ml_w
    return pl.pallas_call(
        _inproj_body,
        grid=(bsz, nt),
        in_specs=[
            pl.BlockSpec((1, tm, d), lambda b, i: (b, i, 0)),
            pl.BlockSpec((1, d), lambda b, i: (0, 0)),
            pl.BlockSpec(w_pad.shape, lambda b, i: (0, 0)),
            pl.BlockSpec((tm, LANE), lambda b, i: (i, 0)),
            pl.BlockSpec((tm, LANE), lambda b, i: (i, 0)),
        ],
        out_specs=[
            pl.BlockSpec((1, 1, da_w, tm), lambda b, i: (b, i, 0, 0)),
            pl.BlockSpec((1, 1, da_w, tm), lambda b, i: (b, i, 0, 0)),
            pl.BlockSpec((1, 1, ml_w, tm), lambda b, i: (b, i, 0, 0)),
            pl.BlockSpec((1, 1, ml_w, tm), lambda b, i: (b, i, 0, 0)),
            pl.BlockSpec((1, tm, n_proj), lambda b, i: (b, i, 0)),
            pl.BlockSpec((1, tm, LANE), lambda b, i: (b, i, 0)),
            pl.BlockSpec((1, SUBLANE, tm), lambda b, i: (b, 0, i)),
        ],
        out_shape=[
            jax.ShapeDtypeStruct((bsz, nt, da_w, tm), BF16),
            jax.ShapeDtypeStruct((bsz, nt, da_w, tm), BF16),
            jax.ShapeDtypeStruct((bsz, nt, ml_w, tm), BF16),
            jax.ShapeDtypeStruct((bsz, nt, ml_w, tm), BF16),
            jax.ShapeDtypeStruct((bsz, seq, n_proj), BF16),
            jax.ShapeDtypeStruct((bsz, seq, LANE), F32),
            jax.ShapeDtypeStruct((bsz, SUBLANE, seq), F32),
        ],
        compiler_params=_params("parallel", "parallel"),
        name="inproj",
    )(x, g, w_pad, cos_t, sin_t)


def _diffattn_body(qt_ref, k_ref, vt_ref, lam_ref, g_ref, o_ref, m_scr, acc_scr, p_scr, *, lam_init):
    qi = pl.program_id(2)
    tq = qt_ref.shape[3]
    heads = qt_ref.shape[2] // DA_VAL_DIM
    q2t = []
    for g in range(heads):
        qt = qt_ref[0, 0, g * DA_VAL_DIM:(g + 1) * DA_VAL_DIM, :]
        row = lax.broadcasted_iota(jnp.int32, qt.shape, 0)
        zero = jnp.zeros_like(qt)
        q2t.append(jnp.concatenate([jnp.where(row < DA_HEAD_DIM, qt, zero),
                                    jnp.where(row >= DA_HEAD_DIM, qt, zero)], axis=1))
    pad_rows = acc_scr.shape[0] - DA_VAL_DIM
    ones_rows = jnp.where(lax.broadcasted_iota(jnp.int32, (pad_rows, tq), 0) == 0, 1.0, 0.0).astype(BF16)

    m_scr[...] = jnp.full(m_scr.shape, NEG_INF, F32)
    acc_scr[...] = jnp.zeros(acc_scr.shape, F32)

    n_chains = 4
    width = 2 * tq // n_chains
    chains = tuple((g, slice(c * width, (c + 1) * width),
                    slice(g * 2 * tq + c * width, g * 2 * tq + (c + 1) * width))
                   for c in range(n_chains) for g in range(heads))

    def scores(j, masked, prev_values):
        out = []
        for g, local, cols in chains:
            k = k_ref[0, pl.ds(pl.multiple_of(j * tq, tq), tq), g * LANE:(g + 1) * LANE]
            s = _dot(k, q2t[g][:, local])
            pv = values(j - 1, g, cols) if prev_values else None
            if masked:
                key = lax.broadcasted_iota(jnp.int32, s.shape, 0)
                qry = lax.broadcasted_iota(jnp.int32, s.shape, 1) + local.start % tq
                s = jnp.where(key <= qry, s, NEG_INF)
            m_prev = m_scr[:, cols]
            m_new = jnp.maximum(m_prev, jnp.max(s, axis=0, keepdims=True))
            p_scr[:, cols] = jnp.exp2(s - m_new).astype(BF16)
            m_scr[:, cols] = m_new
            out.append((jnp.exp2(m_prev - m_new), pv))
        return out

    def values(j, g, cols):
        v_aug = jnp.concatenate([vt_ref[0, j, g * DA_VAL_DIM:(g + 1) * DA_VAL_DIM, :], ones_rows], axis=0)
        return _dot(v_aug, p_scr[:, cols])

    def step(j, masked):
        for (_, _, cols), (alpha, pv) in zip(chains, scores(j, masked, True)):
            acc_scr[:, cols] = alpha * (acc_scr[:, cols] + pv)

    def full_step_pair(i, carry):
        step(1 + 2 * i, False)
        step(2 + 2 * i, False)
        return carry

    @pl.when(qi == 0)
    def _():
        scores(0, True, False)

    @pl.when(qi > 0)
    def _():
        scores(0, False, False)
        lax.fori_loop(0, (qi - 1) // 2, full_step_pair, 0)

        @pl.when((qi - 1) % 2 == 1)
        def _():
            step(qi - 1, False)

        step(qi, True)

    for g, _, cols in chains:
        acc_scr[:, cols] = acc_scr[:, cols] + values(qi, g, cols)

    lv = lam_ref[...]
    lam = (jnp.exp(jnp.sum(lv[0:1, :] * lv[1:2, :], axis=-1, keepdims=True))
           - jnp.exp(jnp.sum(lv[2:3, :] * lv[3:4, :], axis=-1, keepdims=True)) + lam_init)
    for g in range(heads):
        acc = acc_scr[:, g * 2 * tq:(g + 1) * 2 * tq]
        o = acc[0:DA_VAL_DIM, :] / acc[DA_VAL_DIM:DA_VAL_DIM + 1, :]
        o = o[:, :tq] - lam * o[:, tq:]
        y = o * lax.rsqrt(jnp.mean(o * o, axis=0, keepdims=True) + EPS) * g_ref[...] * (1.0 - lam_init)
        o_ref[0, :, g * LANE:(g + 1) * LANE] = y.T.astype(o_ref.dtype)


def _diffattn(qt, proj, vt, da_lambda, subln_g_col, lam_init):
    bsz, nt, _, tq = qt.shape
    seq = nt * tq
    hp = DA_HEADS_PER_STEP
    body = functools.partial(_diffattn_body, lam_init=lam_init)
    return pl.pallas_call(
        body,
        grid=(bsz, DA_HEADS // hp, nt),
        in_specs=[
            pl.BlockSpec((1, 1, hp * DA_VAL_DIM, tq), lambda b, h, i: (b, i, h, 0)),
            pl.BlockSpec((1, seq, hp * LANE), lambda b, h, i: (b, 0, h)),
            pl.BlockSpec((1, nt, hp * DA_VAL_DIM, tq), lambda b, h, i: (b, 0, h, 0)),
            pl.BlockSpec(da_lambda.shape, lambda b, h, i: (0, 0)),
            pl.BlockSpec((DA_VAL_DIM, 1), lambda b, h, i: (0, 0)),
        ],
        out_specs=pl.BlockSpec((1, tq, hp * LANE), lambda b, h, i: (b, i, h)),
        out_shape=jax.ShapeDtypeStruct((bsz, seq, DA_HEADS * DA_VAL_DIM), BF16),
        scratch_shapes=[
            pltpu.VMEM((1, hp * 2 * tq), F32),
            pltpu.VMEM((DA_VAL_DIM + 2 * SUBLANE, hp * 2 * tq), F32),
            pltpu.VMEM((tq, hp * 2 * tq), BF16),
        ],
        compiler_params=_params("parallel", "parallel", "arbitrary"),
        name="diffattn",
    )(qt, proj, vt, da_lambda, subln_g_col)


def _log_sigmoid(x):
    return jnp.minimum(x, 0.0) - jnp.log(1.0 + jnp.exp(-jnp.abs(x)))


def _sigmoid(x):
    return 1.0 / (1.0 + jnp.exp(-x))


def _mlstm_body(q_ref, k_ref, vt_ref, ot_ref, gcol_ref, grow_ref, cw_ref, cb_ref, gbcol_ref, gbrow_ref, ng_ref,
                y_ref, xbuf, c_scr, m_scr):
    @pl.when(pl.program_id(1) == 0)
    def _():
        xbuf[...] = jnp.zeros(xbuf.shape, xbuf.dtype)
        c_scr[...] = jnp.zeros(c_scr.shape, F32)
        m_scr[...] = jnp.zeros(m_scr.shape, F32)

    for bb in range(q_ref.shape[0]):
        _mlstm_row(bb, q_ref, k_ref, vt_ref, ot_ref, gcol_ref, grow_ref, cw_ref, cb_ref, gbcol_ref, gbrow_ref,
                   ng_ref, y_ref, xbuf, c_scr, m_scr)


def _mlstm_row(bb, q_ref, k_ref, vt_ref, ot_ref, gcol_ref, grow_ref, cw_ref, cb_ref, gbcol_ref, gbrow_ref, ng_ref,
               y_ref, xbuf, c_scr, m_scr):
    L = q_ref.shape[1]
    width = q_ref.shape[2]
    dh = width // ML_HEADS
    tail = xbuf.shape[1]
    aug = c_scr.shape[1]

    u = jnp.concatenate([q_ref[bb], k_ref[bb]], axis=1)
    u_ext = jnp.concatenate([xbuf[bb], u], axis=0)
    src = lax.broadcasted_iota(jnp.int32, (L, L + tail), 1) - lax.broadcasted_iota(jnp.int32, (L, L + tail), 0)
    conv = cb_ref[...] + cw_ref[ML_CONV - 1:ML_CONV, :] * u.astype(F32)
    for back in range(1, ML_CONV):
        shift = jnp.where(src == tail - back, 1.0, 0.0).astype(BF16)
        conv = conv + cw_ref[ML_CONV - 1 - back:ML_CONV - back, :] * _dot(shift, u_ext)
    xbuf[bb] = u[L - tail:, :]
    qk = conv * _sigmoid(conv)

    gc = gcol_ref[bb] + gbcol_ref[...]
    gr = grow_ref[bb] + gbrow_ref[...]
    row_id = lax.broadcasted_iota(jnp.int32, (L, L), 0)
    col_id = lax.broadcasted_iota(jnp.int32, (L, L), 1)
    causal = row_id <= col_id
    tri = jnp.where(col_id <= row_id, 1.0, 0.0).astype(BF16)
    tri_t = jnp.where(causal, 1.0, 0.0).astype(BF16)
    lfc_hi, lfc_lo = _split_bf16(_log_sigmoid(gc))
    lfr_hi, lfr_lo = _split_bf16(_log_sigmoid(gr))
    a_c = _dot(tri, lfc_hi) + _dot(tri, lfc_lo)
    a_r = _dot(lfr_hi, tri_t) + _dot(lfr_lo, tri_t)

    ones_rows = jnp.where(lax.broadcasted_iota(jnp.int32, (aug - dh, L), 0) == 0, 1.0, 0.0).astype(BF16)

    for h in range(ML_HEADS):
        fh = ML_HEADS + h
        st = bb * ML_HEADS + h
        m_prev = m_scr[st:st + 1, 0:1]
        a_row = a_r[fh:fh + 1, :]
        ig_row = gr[h:h + 1, :]
        nb = jnp.where(causal, gc[:, h:h + 1] - a_c[:, fh:fh + 1], NEG_INF)
        m_row = a_row + jnp.maximum(m_prev, jnp.max(nb, axis=0, keepdims=True))
        w_intra = jnp.exp(nb + (a_row - m_row))
        w_inter = jnp.exp(a_row + m_prev - m_row)

        qh = qk[:, h * dh:(h + 1) * dh].astype(BF16)
        kh = (qk[:, width + h * dh:width + (h + 1) * dh] * (dh ** -0.5)).astype(BF16)
        p_t = (_dot_nt(kh, qh) * w_intra).astype(BF16)
        v_aug = jnp.concatenate([vt_ref[bb, 0, h * dh:(h + 1) * dh, :], ones_rows], axis=0)
        c_t = c_scr[st]
        num = w_inter * _dot_nt(c_t.astype(BF16), qh) + _dot(v_aug, p_t)
        scale = 1.0 / jnp.maximum(jnp.abs(num[dh:dh + 1, :]), jnp.exp(-m_row))
        hid = num[0:dh, :] * scale

        g_last = a_row[:, L - 1:L]
        log_w = g_last - a_row + ig_row
        m_new = jnp.maximum(g_last + m_prev, jnp.max(log_w, axis=1, keepdims=True))
        w_state = jnp.exp(log_w - m_new)
        decay = jnp.exp(g_last + m_prev - m_new)
        wv = (w_state * v_aug.astype(F32)).astype(BF16)
        c_scr[st] = decay * c_t + _dot(wv, kh)
        m_scr[st:st + 1, :] = jnp.broadcast_to(m_new, (1, m_scr.shape[1]))

        mu = jnp.mean(hid, axis=0, keepdims=True)
        cen = hid - mu
        var = jnp.mean(cen * cen, axis=0, keepdims=True)
        g_b = ng_ref[h * dh:(h + 1) * dh, :]
        hn = cen * lax.rsqrt(var + EPS) * jnp.concatenate([g_b] * (L // LANE), axis=1)
        gate = _sigmoid(ot_ref[bb, 0, h * dh:(h + 1) * dh, :].astype(F32))
        y_ref[bb, :, h * dh:(h + 1) * dh] = (gate * hn).T.astype(y_ref.dtype)


def _mlstm(proj, mvt, mot, gcol, grow, conv_w, conv_b, gb_col, gb_row, norm_g_b, col0):
    bsz, seq, _ = proj.shape
    width = norm_g_b.shape[0]
    dh = width // ML_HEADS
    L = min(ML_CHUNK, seq)
    tm = mvt.shape[-1]
    per = tm // L
    blk0 = col0 // width
    nb = ML_BATCH_PER_STEP
    spec = lambda k: pl.BlockSpec((nb, L, width), lambda b, c: (b, c, blk0 + k))
    tspec = pl.BlockSpec((nb, 1, width, L), lambda b, c: (b, c // per, 0, c % per))
    return pl.pallas_call(
        _mlstm_body,
        grid=(bsz // nb, seq // L),
        in_specs=[
            spec(0), spec(1), tspec, tspec,
            pl.BlockSpec((nb, L, LANE), lambda b, c: (b, c, 0)),
            pl.BlockSpec((nb, SUBLANE, L), lambda b, c: (b, 0, c)),
            pl.BlockSpec(conv_w.shape, lambda b, c: (0, 0)),
            pl.BlockSpec(conv_b.shape, lambda b, c: (0, 0)),
            pl.BlockSpec(gb_col.shape, lambda b, c: (0, 0)),
            pl.BlockSpec(gb_row.shape, lambda b, c: (0, 0)),
            pl.BlockSpec(norm_g_b.shape, lambda b, c: (0, 0)),
        ],
        out_specs=pl.BlockSpec((nb, L, width), lambda b, c: (b, c, 0)),
        out_shape=jax.ShapeDtypeStruct((bsz, seq, width), BF16),
        scratch_shapes=[
            pltpu.VMEM((nb, 2 * SUBLANE, 2 * width), BF16),
            pltpu.VMEM((nb * ML_HEADS, dh + 2 * SUBLANE, dh), F32),
            pltpu.VMEM((nb * ML_HEADS, LANE), F32),
        ],
        compiler_params=_params("parallel", "arbitrary"),
        name="mlstm",
    )(proj, proj, mvt, mot, gcol, grow, conv_w, conv_b, gb_col, gb_row, norm_g_b)


def _memkv_body(mem_ref, g_ref, wk_ref, wv_ref, k_ref, v_ref):
    mb = _rms(mem_ref[0], g_ref[...]).astype(BF16)
    k_ref[0] = _dot(mb, wk_ref[...]).astype(BF16)
    v_ref[0] = _dot(mb, wv_ref[...]).astype(BF16)


def _memkv(mem, g, wk, wv):
    bsz, mlen, d = mem.shape
    full = lambda a: pl.BlockSpec(a.shape, lambda b: (0,) * a.ndim)
    blk = pl.BlockSpec((1, mlen, d), lambda b: (b, 0, 0))
    return pl.pallas_call(
        _memkv_body,
        grid=(bsz,),
        in_specs=[blk, full(g), full(wk), full(wv)],
        out_specs=[blk, blk],
        out_shape=[jax.ShapeDtypeStruct((bsz, mlen, d), BF16)] * 2,
        compiler_params=_params("parallel"),
        name="memkv",
    )(mem, g, wk, wv)


def _mix_body(x_ref, yda_ref, yml_ref, wo1_ref, wo2_ref, gx_ref, wq_ref, km_ref, vm_ref, wxo_ref, gf_ref,
              wrh_ref, wrl_ref, br_ref, x2_ref, route_ref):
    x1 = x_ref[0] + _dot(yda_ref[0], wo1_ref[...]) + _dot(yml_ref[0], wo2_ref[...])

    d = x1.shape[-1]
    dh = d // X_HEADS
    q = _dot(_rms(x1, gx_ref[...]).astype(BF16), wq_ref[...]).astype(BF16)
    heads = []
    for h in range(X_HEADS):
        sl = slice(h * dh, (h + 1) * dh)
        s = _dot_nt(q[:, sl], km_ref[0, :, sl]) * (dh ** -0.5)
        p = jnp.exp(s - jnp.max(s, axis=-1, keepdims=True))
        o = _dot(p.astype(BF16), vm_ref[0, :, sl]) / jnp.sum(p, axis=-1, keepdims=True)
        heads.append(o.astype(BF16))
    x2 = x1 + _dot(jnp.concatenate(heads, axis=1), wxo_ref[...])
    _to_token_tiles(x2_ref.at[0], 0, x2)

    h3 = _rms(x2, gf_ref[...])
    h_hi, h_lo = _split_bf16(h3)
    logits = (_dot_nt(wrh_ref[...], h_hi) + _dot_nt(wrh_ref[...], h_lo) + _dot_nt(wrl_ref[...], h_hi)
              + br_ref[...])
    tm = logits.shape[1]
    gl = logits[0:N_GROUPS, :]
    g_iota = lax.broadcasted_iota(jnp.int32, gl.shape, 0)
    g_max = jnp.max(gl, axis=0, keepdims=True)
    g_idx = jnp.min(jnp.where(gl == g_max, g_iota, N_GROUPS), axis=0, keepdims=True)
    g_w = 1.0 / jnp.sum(jnp.exp(gl - g_max), axis=0, keepdims=True)
    el = jnp.zeros((EXPERTS_PER_GROUP, tm), F32)
    for g in range(N_GROUPS):
        lo = N_GROUPS + g * EXPERTS_PER_GROUP
        el = jnp.where(g_idx == g, logits[lo:lo + EXPERTS_PER_GROUP, :], el)
    e_iota = lax.broadcasted_iota(jnp.int32, el.shape, 0)
    e1 = jnp.max(el, axis=0, keepdims=True)
    i1 = jnp.min(jnp.where(el == e1, e_iota, EXPERTS_PER_GROUP), axis=0, keepdims=True)
    el2 = jnp.where(e_iota == i1, NEG_INF, el)
    e2 = jnp.max(el2, axis=0, keepdims=True)
    i2 = jnp.min(jnp.where(el2 == e2, e_iota, EXPERTS_PER_GROUP), axis=0, keepdims=True)
    p2 = jnp.exp(e2 - e1)
    w1 = g_w / (1.0 + p2)
    w2 = g_w * p2 / (1.0 + p2)
    base = g_idx * EXPERTS_PER_GROUP
    zero = jnp.zeros((1, tm), F32)
    route_ref[0] = jnp.concatenate(
        [(base + i1).astype(F32), (base + i2).astype(F32), w1, w2, zero, zero, zero, zero], axis=0)


def _mix(x, yda, yml, wo1, wo2, gx, wq, kmem, vmem, wxo, gf, wr_hi, wr_lo, br):
    bsz, seq, d = x.shape
    tm = min(ROW_TILE, seq)
    mlen = kmem.shape[1]
    full = lambda a: pl.BlockSpec(a.shape, lambda b, i: (0,) * a.ndim)
    tok = lambda w: pl.BlockSpec((1, tm, w), lambda b, i: (b, i, 0))
    memspec = pl.BlockSpec((1, mlen, d), lambda b, i: (b, 0, 0))
    return pl.pallas_call(
        _mix_body,
        grid=(bsz, seq // tm),
        in_specs=[tok(d), tok(yda.shape[-1]), tok(yml.shape[-1]), full(wo1), full(wo2), full(gx), full(wq),
                  memspec, memspec, full(wxo), full(gf), full(wr_hi), full(wr_lo), full(br)],
        out_specs=[pl.BlockSpec((1, tm * SUBLANE, LANE), lambda b, i: (b, i, 0)),
                   pl.BlockSpec((1, SUBLANE, tm), lambda b, i: (b, 0, i))],
        out_shape=[
            jax.ShapeDtypeStruct((bsz, seq * d // LANE, LANE), F32),
            jax.ShapeDtypeStruct((bsz, SUBLANE, seq), F32),
        ],
        compiler_params=_params("parallel", "parallel"),
        name="mix_xattn_router",
    )(x, yda, yml, wo1, wo2, gx, wq, kmem, vmem, wxo, gf, wr_hi, wr_lo, br)


def _to_token_tiles(ref, first_row, dense):
    n, d = dense.shape
    for c in range(d // LANE):
        ref[pl.ds(first_row * SUBLANE + c, n, stride=SUBLANE), :] = dense[:, c * LANE:(c + 1) * LANE]


def _from_token_tiles(ref, first_row, n):
    return jnp.concatenate(
        [ref[pl.ds(first_row * SUBLANE + c, n, stride=SUBLANE), :] for c in range(SUBLANE)], axis=1)


def _moe_body(start_ref, cnt_ref, x_hbm, tok_hbm, wt_hbm, gf_ref, gfin_ref, w1_hbm, w3_hbm, w2_hbm, out_hbm,
              acc, h3, xg, yt, stage, tok_s, wt_s, w1_buf, w3_buf, w2_buf, sems, load_sems, store_sems, w_sems,
              *, final_norm):
    tile = pl.program_id(0)
    tt = h3.shape[0] // SUBLANE
    chunk_rows = stage.shape[1]
    group = SUBLANE
    depth = w1_buf.shape[0]
    n_steps = pl.num_programs(0) * N_EXPERTS

    def weight_copies(g):
        expert = g % N_EXPERTS
        slot = g % depth
        return [pltpu.make_async_copy(src.at[expert], dst.at[slot], w_sems.at[slot * 3 + k])
                for k, (src, dst) in enumerate(((w1_hbm, w1_buf), (w3_hbm, w3_buf), (w2_hbm, w2_buf)))]

    @pl.when(tile == 0)
    def _():
        for ahead in range(depth - 1):
            for cp in weight_copies(ahead):
                cp.start()

    def row_tile(ref, row0):
        return ref.at[pl.ds(pl.multiple_of(row0, SUBLANE), SUBLANE), :]

    def token_tile(ref, r):
        return ref.at[pl.ds(r * SUBLANE, SUBLANE), :]

    spare_row = tt * SUBLANE

    n_chunks = tt // chunk_rows
    chunk_tiles = chunk_rows * SUBLANE

    def load_chunk(i):
        rows = pl.ds(pl.multiple_of(i * chunk_tiles, chunk_tiles), chunk_tiles)
        return pltpu.make_async_copy(x_hbm.at[tile, rows, :], acc.at[rows, :], load_sems.at[i])

    lists = [pltpu.make_async_copy(tok_hbm.at[tile], tok_s, sems.at[0]),
             pltpu.make_async_copy(wt_hbm.at[tile], wt_s, sems.at[1])]
    for cp in lists:
        cp.start()
    for i in range(n_chunks):
        load_chunk(i).start()
    token_tile(acc, tt)[...] = jnp.zeros((SUBLANE, LANE), F32)

    def ffn_norm(i, c):
        load_chunk(i).wait()
        r0 = i * chunk_rows
        _to_token_tiles(h3, r0, _rms(_from_token_tiles(acc, r0, chunk_rows), gf_ref[...]))
        return c
    lax.fori_loop(0, n_chunks, ffn_norm, 0)
    for cp in lists:
        cp.wait()

    big = xg.shape[0] // SUBLANE
    small = MOE_ROWS

    def expert_step(e, carry):
        step_id = tile * N_EXPERTS + e

        @pl.when(step_id + depth - 1 < n_steps)
        def _():
            for cp in weight_copies(step_id + depth - 1):
                cp.start()

        for cp in weight_copies(step_id):
            cp.wait()
        w_slot = step_id % depth
        start = start_ref[step_id]
        count = cnt_ref[step_id]

        def run_block(first, cnt, rows):
            idx0 = start + first
            for r in range(rows):
                token_tile(xg, r)[...] = row_tile(h3, tok_s[idx0 + r])[...]

            xb = _from_token_tiles(xg, 0, rows).astype(BF16)
            a = _dot(xb, w1_buf[w_slot])
            b = _dot(xb, w3_buf[w_slot])
            hmid = (a * _sigmoid(a) * b).astype(BF16)
            _to_token_tiles(yt, 0, _dot(hmid, w2_buf[w_slot]))

            for g0 in range(0, rows, group):
                dst = [jnp.where(g0 + u < cnt, tok_s[idx0 + g0 + u], spare_row) for u in range(group)]
                new = [row_tile(acc, dst[u])[...] + wt_s[idx0 + g0 + u] * token_tile(yt, g0 + u)[...]
                       for u in range(group)]
                for u in range(group):
                    row_tile(acc, dst[u])[...] = new[u]

        @pl.when(count > 0)
        def _():
            run_block(0, jnp.minimum(big, count), big)

        def small_block(i, c):
            first = big + i * small
            run_block(first, jnp.minimum(small, count - first), small)
            return c

        lax.fori_loop(0, (jnp.maximum(count - big, 0) + small - 1) // small, small_block, 0)
        return carry

    lax.fori_loop(0, N_EXPERTS, expert_step, 0)

    n_slots = stage.shape[0]

    def store_chunk(i, slot):
        r0 = pl.multiple_of(i * chunk_rows, chunk_rows)
        return pltpu.make_async_copy(stage.at[slot], out_hbm.at[tile, pl.ds(r0, chunk_rows), :],
                                     store_sems.at[slot])

    def write_back(p, c):
        for slot in range(n_slots):
            i = p * n_slots + slot

            @pl.when(p > 0)
            def _():
                store_chunk(i - n_slots, slot).wait()

            rows = _from_token_tiles(acc, i * chunk_rows, chunk_rows)
            stage[slot] = _rms(rows, gfin_ref[...]) if final_norm else rows
            store_chunk(i, slot).start()
        return c
    lax.fori_loop(0, n_chunks // n_slots, write_back, 0)
    for slot in range(n_slots):
        store_chunk(n_chunks - n_slots + slot, slot).wait()


def _moe(start, cnt, tok, wt, x_token_tiles, gf, gfin, w1, w3, w2, final_norm):
    ntiles, tile_rows, _ = x_token_tiles.shape
    tt = tile_rows // SUBLANE
    d = SUBLANE * LANE
    de = w1.shape[-1]
    seg = tok.shape[1]
    body = functools.partial(_moe_body, final_norm=final_norm)
    hbm = pl.BlockSpec(memory_space=pl.ANY)
    depth = MOE_WEIGHT_SLOTS
    grid_spec = pltpu.PrefetchScalarGridSpec(
        num_scalar_prefetch=2,
        grid=(ntiles,),
        in_specs=[
            hbm, hbm, hbm,
            pl.BlockSpec((1, d), lambda t, *_: (0, 0)),
            pl.BlockSpec((1, d), lambda t, *_: (0, 0)),
            hbm, hbm, hbm,
        ],
        out_specs=hbm,
        scratch_shapes=[
            pltpu.VMEM((tile_rows + SUBLANE, LANE), F32),
            pltpu.VMEM((tile_rows, LANE), F32),
            pltpu.VMEM((MOE_FIRST_ROWS * SUBLANE, LANE), F32),
            pltpu.VMEM((MOE_FIRST_ROWS * SUBLANE, LANE), F32),
            pltpu.VMEM((2, 2 * MOE_ROWS, d), F32),
            pltpu.SMEM((seg,), jnp.int32),
            pltpu.SMEM((seg,), F32),
            pltpu.VMEM((depth, d, de), BF16),
            pltpu.VMEM((depth, d, de), BF16),
            pltpu.VMEM((depth, de, d), BF16),
            pltpu.SemaphoreType.DMA((2,)),
            pltpu.SemaphoreType.DMA((tt // (2 * MOE_ROWS),)),
            pltpu.SemaphoreType.DMA((2,)),
            pltpu.SemaphoreType.DMA((3 * depth,)),
        ],
    )
    return pl.pallas_call(
        body,
        grid_spec=grid_spec,
        out_shape=jax.ShapeDtypeStruct((ntiles, tt, d), F32),
        compiler_params=_params("arbitrary"),
        name="moe",
    )(start, cnt, x_token_tiles, tok, wt, gf, gfin, w1, w3, w2)


def _rope_tables(seq):
    half = DA_HEAD_DIM // 2
    inv_freq = ROPE_THETA ** (-jnp.arange(0, DA_HEAD_DIM, 2, dtype=F32) / DA_HEAD_DIM)
    ang = jnp.arange(seq, dtype=F32)[:, None] * inv_freq[None, :]
    cos, sin = jnp.cos(ang), jnp.sin(ang)
    reps = LANE // half
    cos_t = jnp.tile(cos, (1, reps))
    sin_t = jnp.tile(jnp.concatenate([-sin, sin], axis=1), (1, reps // 2))
    return cos_t, sin_t


def _sort_assignments(route, tt):
    bsz, _, seq = route.shape
    per = seq // tt
    ntiles = bsz * per
    tile = lambda a: a.reshape(bsz, TOP_K_ROWS, per, tt).transpose(0, 2, 1, 3).reshape(ntiles, TOP_K_ROWS * tt)
    keys = tile(route[:, 0:TOP_K_ROWS, :]).astype(jnp.int32)
    wts = tile(route[:, TOP_K_ROWS:2 * TOP_K_ROWS, :])
    order = jnp.argsort(keys, axis=-1).astype(jnp.int32)
    tok = (order % tt) * SUBLANE
    wt = jnp.take_along_axis(wts, order, axis=-1)
    counts = jnp.sum(keys[:, :, None] == jnp.arange(N_EXPERTS, dtype=jnp.int32)[None, None, :], axis=1,
                     dtype=jnp.int32)
    start = jnp.cumsum(counts, axis=-1, dtype=jnp.int32) - counts
    over = ((0, 0), (0, MOE_FIRST_ROWS))
    return start.reshape(-1), counts.reshape(-1), jnp.pad(tok, over), jnp.pad(wt, over)


TOP_K_ROWS = 2


def kernel(x, mem, norm_mix_g, w_in, w_out, da_lambda, da_subln_g, ml_conv_w, ml_conv_b, ml_gate_b, ml_norm_g, norm_x_g, norm_mem_g, w_xq, w_xk, w_xv, w_xo, norm_ffn_g, w_router_group, b_router_group, w_router_expert, b_router_expert, w1, w3, w2, norm_final_g):
    depth = w_in.shape[0]
    bsz, seq, d = x.shape
    da_qk = DA_HEADS * 2 * DA_HEAD_DIM
    da_width = DA_HEADS * DA_VAL_DIM
    ml_width = ml_norm_g.shape[-1]
    n_main = 2 * da_qk + da_width + 4 * ml_width
    n_in = w_in.shape[-1]
    cos_t, sin_t = _rope_tables(seq)
    row = lambda v: v.reshape(1, -1).astype(F32)

    for l in range(depth):
        lam_init = 0.8 - 0.6 * math.exp(-0.3 * l)
        w_pad = jnp.pad(w_in[l], ((0, 0), (0, n_main + LANE - n_in))).astype(BF16)
        qt, vt, mvt, mot, proj, gcol, grow = _inproj(x, row(norm_mix_g[l]), w_pad, cos_t, sin_t)

        yda = _diffattn(qt, proj, vt, da_lambda[l].astype(F32), da_subln_g[l].astype(F32).reshape(-1, 1), lam_init)

        gb = ml_gate_b[l].astype(F32).reshape(1, 2 * ML_HEADS)
        gb_col = jnp.pad(gb, ((0, 0), (0, LANE - 2 * ML_HEADS)))
        gb_row = gb.reshape(2 * ML_HEADS, 1)
        ng_b = jnp.broadcast_to(ml_norm_g[l].astype(F32)[:, None], (ml_width, LANE))
        yml = _mlstm(proj, mvt, mot, gcol, grow, ml_conv_w[l].reshape(ML_CONV, 2 * ml_width).astype(F32),
                     row(ml_conv_b[l]), gb_col, gb_row, ng_b, da_qk)

        kmem, vmem = _memkv(mem, row(norm_mem_g[l]), w_xk[l].astype(BF16), w_xv[l].astype(BF16))

        wr = jnp.concatenate([w_router_group[l], w_router_expert[l]], axis=1).astype(F32).T
        wr = jnp.pad(wr, ((0, ROUTER_ROWS - wr.shape[0]), (0, 0)))
        wr_hi = wr.astype(BF16)
        wr_lo = (wr - wr_hi.astype(F32)).astype(BF16)
        br = jnp.pad(jnp.concatenate([b_router_group[l], b_router_expert[l]]).astype(F32),
                     (0, ROUTER_ROWS - N_GROUPS - N_EXPERTS)).reshape(ROUTER_ROWS, 1)
        wo = w_out[l].astype(BF16)
        x2, route = _mix(x, yda, yml, wo[:da_width], wo[da_width:], row(norm_x_g[l]), w_xq[l].astype(BF16),
                         kmem, vmem, w_xo[l].astype(BF16), row(norm_ffn_g[l]), wr_hi, wr_lo, br)

        tt = min(MOE_TILE, seq)
        x = _moe(*_sort_assignments(route, tt), x2.reshape(bsz * seq // tt, tt * d // LANE, LANE),
                 row(norm_ffn_g[l]), row(norm_final_g),
                 w1[l].astype(BF16), w3[l].astype(BF16), w2[l].astype(BF16),
                 final_norm=(l + 1 == depth)).reshape(bsz, seq, d)
    return x
```

```python
import functools
import math

import jax
import jax.numpy as jnp
from jax import lax
from jax.experimental import pallas as pl
from jax.experimental.pallas import tpu as pltpu

F32 = jnp.float32
BF16 = jnp.bfloat16

LANE = 128
SUBLANE = 8
VMEM_LIMIT_BYTES = 56 * 1024 * 1024

EPS = 1e-6
ROPE_THETA = 10000.0
DA_HEADS = 4
DA_HEAD_DIM = 64
DA_VAL_DIM = 2 * DA_HEAD_DIM
ML_HEADS = 4
ML_CONV = 4
X_HEADS = 4
N_GROUPS = 4
EXPERTS_PER_GROUP = 8
N_EXPERTS = N_GROUPS * EXPERTS_PER_GROUP
ROUTER_ROWS = 40

ROW_TILE = 512
DA_HEADS_PER_STEP = 4
ML_CHUNK = 256
ML_BATCH_PER_STEP = 4
MOE_TILE = 4096
MOE_ROWS = 128
MOE_FIRST_ROWS = 288
MOE_WEIGHT_SLOTS = 3

NEG_INF = float("-inf")
LOG2_E = math.log2(math.e)


def _rms(x, g):
    return x * lax.rsqrt(jnp.mean(x * x, axis=-1, keepdims=True) + EPS) * g


def _dot(a, b):
    return jnp.dot(a, b, preferred_element_type=F32)


def _dot_nt(a, b):
    return lax.dot_general(a, b, (((1,), (1,)), ((), ())), preferred_element_type=F32)


def _dot_tn(a, b):
    return lax.dot_general(a, b, (((0,), (0,)), ((), ())), preferred_element_type=F32)


def _split_bf16(x):
    hi = x.astype(BF16)
    lo = (x - hi.astype(F32)).astype(BF16)
    return hi, lo


def _params(*semantics):
    return pltpu.CompilerParams(dimension_semantics=semantics, vmem_limit_bytes=VMEM_LIMIT_BYTES)


def _inproj_body(x_ref, g_ref, w_ref, cos_ref, sin_ref, qt_ref, vt_ref, mvt_ref, mot_ref, proj_ref, gcol_ref,
                 grow_ref):
    hb = _rms(x_ref[0], g_ref[...]).astype(BF16)
    cos = cos_ref[...]
    sin = sin_ref[...]
    lane = lax.broadcasted_iota(jnp.int32, cos.shape, 1)
    first_half = (lane % DA_HEAD_DIM) < (DA_HEAD_DIM // 2)
    n_main = w_ref.shape[1] - LANE
    wide = 4 * LANE
    for c in range(n_main // LANE):
        if c % (wide // LANE) == 0:
            acc = _dot(hb, w_ref[:, c * LANE:c * LANE + wide])
        t = acc[:, (c * LANE) % wide:(c * LANE) % wide + LANE]
        if c < 2 * DA_HEADS:
            rot = jnp.where(first_half, pltpu.roll(t, LANE - DA_HEAD_DIM // 2, 1),
                            pltpu.roll(t, DA_HEAD_DIM // 2, 1))
            t = t * cos + rot * sin
        if c < DA_HEADS:
            qt_ref[0, 0, c * LANE:(c + 1) * LANE, :] = (t * (DA_HEAD_DIM ** -0.5 * LOG2_E)).T.astype(BF16)
        elif c < 2 * DA_HEADS:
            proj_ref[0, :, (c - DA_HEADS) * LANE:(c - DA_HEADS + 1) * LANE] = t.astype(BF16)
        elif c < 3 * DA_HEADS:
            vt_ref[0, 0, (c - 2 * DA_HEADS) * LANE:(c - 2 * DA_HEADS + 1) * LANE, :] = t.T.astype(BF16)
        elif c < 3 * DA_HEADS + 2 * ML_HEADS:
            proj_ref[0, :, (c - 2 * DA_HEADS) * LANE:(c - 2 * DA_HEADS + 1) * LANE] = t.astype(BF16)
        elif c < 3 * DA_HEADS + 3 * ML_HEADS:
            m = c - 3 * DA_HEADS - 2 * ML_HEADS
            mvt_ref[0, 0, m * LANE:(m + 1) * LANE, :] = t.T.astype(BF16)
        else:
            m = c - 3 * DA_HEADS - 3 * ML_HEADS
            mot_ref[0, 0, m * LANE:(m + 1) * LANE, :] = t.T.astype(BF16)
    gates = _dot(hb, w_ref[:, n_main:n_main + LANE])
    gcol_ref[0] = gates
    grow_ref[0] = gates.T[:SUBLANE, :]


def _inproj(x, g, w_pad, cos_t, sin_t):
    bsz, seq, d = x.shape
    tm = min(ROW_TILE, seq)
    nt = seq // tm
    da_w = DA_HEADS * DA_VAL_DIM
    ml_w = ML_HEADS * LANE
    n_proj = w_pad.shape[1] - LANE - 2 * da_w - 2 * ml_w
    return pl.pallas_call(
        _inproj_body,
        grid=(bsz, nt),
        in_specs=[
            pl.BlockSpec((1, tm, d), lambda b, i: (b, i, 0)),
            pl.BlockSpec((1, d), lambda b, i: (0, 0)),
            pl.BlockSpec(w_pad.shape, lambda b, i: (0, 0)),
            pl.BlockSpec((tm, LANE), lambda b, i: (i, 0)),
            pl.BlockSpec((tm, LANE), lambda b, i: (i, 0)),
        ],
        out_specs=[
            pl.BlockSpec((1, 1, da_w, tm), lambda b, i: (b, i, 0, 0)),
            pl.BlockSpec((1, 1, da_w, tm), lambda b, i: (b, i, 0, 0)),
            pl.BlockSpec((1, 1, ml_w, tm), lambda b, i: (b, i, 0, 0)),
            pl.BlockSpec((1, 1, ml_w, tm), lambda b, i: (b, i, 0, 0)),
            pl.BlockSpec((1, tm, n_proj), lambda b, i: (b, i, 0)),
            pl.BlockSpec((1, tm, LANE), lambda b, i: (b, i, 0)),
            pl.BlockSpec((1, SUBLANE, tm), lambda b, i: (b, 0, i)),
        ],
        out_shape=[
            jax.ShapeDtypeStruct((bsz, nt, da_w, tm), BF16),
            jax.ShapeDtypeStruct((bsz, nt, da_w, tm), BF16),
            jax.ShapeDtypeStruct((bsz, nt, ml_w, tm), BF16),
            jax.ShapeDtypeStruct((bsz, nt, ml_w, tm), BF16),
            jax.ShapeDtypeStruct((bsz, seq, n_proj), BF16),
            jax.ShapeDtypeStruct((bsz, seq, LANE), F32),
            jax.ShapeDtypeStruct((bsz, SUBLANE, seq), F32),
        ],
        compiler_params=_params("parallel", "parallel"),
        name="inproj",
    )(x, g, w_pad, cos_t, sin_t)


def _diffattn_body(qt_ref, k_ref, vt_ref, lam_ref, g_ref, o_ref, m_scr, acc_scr, p_scr, *, lam_init):
    qi = pl.program_id(2)
    tq = qt_ref.shape[3]
    heads = qt_ref.shape[2] // DA_VAL_DIM
    q2t = []
    for g in range(heads):
        qt = qt_ref[0, 0, g * DA_VAL_DIM:(g + 1) * DA_VAL_DIM, :]
        row = lax.broadcasted_iota(jnp.int32, qt.shape, 0)
        zero = jnp.zeros_like(qt)
        q2t.append(jnp.concatenate([jnp.where(row < DA_HEAD_DIM, qt, zero),
                                    jnp.where(row >= DA_HEAD_DIM, qt, zero)], axis=1))
    pad_rows = acc_scr.shape[0] - DA_VAL_DIM
    ones_rows = jnp.where(lax.broadcasted_iota(jnp.int32, (pad_rows, tq), 0) == 0, 1.0, 0.0).astype(BF16)

    m_scr[...] = jnp.full(m_scr.shape, NEG_INF, F32)
    acc_scr[...] = jnp.zeros(acc_scr.shape, F32)

    n_chains = 4
    width = 2 * tq // n_chains
    chains = tuple((g, slice(c * width, (c + 1) * width),
                    slice(g * 2 * tq + c * width, g * 2 * tq + (c + 1) * width))
                   for c in range(n_chains) for g in range(heads))

    def scores(j, masked, prev_values):
        out = []
        for g, local, cols in chains:
            k = k_ref[0, pl.ds(pl.multiple_of(j * tq, tq), tq), g * LANE:(g + 1) * LANE]
            s = _dot(k, q2t[g][:, local])
            pv = values(j - 1, g, cols) if prev_values else None
            if masked:
                key = lax.broadcasted_iota(jnp.int32, s.shape, 0)
                qry = lax.broadcasted_iota(jnp.int32, s.shape, 1) + local.start % tq
                s = jnp.where(key <= qry, s, NEG_INF)
            m_prev = m_scr[:, cols]
            m_new = jnp.maximum(m_prev, jnp.max(s, axis=0, keepdims=True))
            p_scr[:, cols] = jnp.exp2(s - m_new).astype(BF16)
            m_scr[:, cols] = m_new
            out.append((jnp.exp2(m_prev - m_new), pv))
        return out

    def values(j, g, cols):
        v_aug = jnp.concatenate([vt_ref[0, j, g * DA_VAL_DIM:(g + 1) * DA_VAL_DIM, :], ones_rows], axis=0)
        return _dot(v_aug, p_scr[:, cols])

    def step(j, masked):
        for (_, _, cols), (alpha, pv) in zip(chains, scores(j, masked, True)):
            acc_scr[:, cols] = alpha * (acc_scr[:, cols] + pv)

    def full_step_pair(i, carry):
        step(1 + 2 * i, False)
        step(2 + 2 * i, False)
        return carry

    @pl.when(qi == 0)
    def _():
        scores(0, True, False)

    @pl.when(qi > 0)
    def _():
        scores(0, False, False)
        lax.fori_loop(0, (qi - 1) // 2, full_step_pair, 0)

        @pl.when((qi - 1) % 2 == 1)
        def _():
            step(qi - 1, False)

        step(qi, True)

    for g, _, cols in chains:
        acc_scr[:, cols] = acc_scr[:, cols] + values(qi, g, cols)

    lv = lam_ref[...]
    lam = (jnp.exp(jnp.sum(lv[0:1, :] * lv[1:2, :], axis=-1, keepdims=True))
           - jnp.exp(jnp.sum(lv[2:3, :] * lv[3:4, :], axis=-1, keepdims=True)) + lam_init)
    for g in range(heads):
        acc = acc_scr[:, g * 2 * tq:(g + 1) * 2 * tq]
        o = acc[0:DA_VAL_DIM, :] / acc[DA_VAL_DIM:DA_VAL_DIM + 1, :]
        o = o[:, :tq] - lam * o[:, tq:]
        y = o * lax.rsqrt(jnp.mean(o * o, axis=0, keepdims=True) + EPS) * g_ref[...] * (1.0 - lam_init)
        o_ref[0, :, g * LANE:(g + 1) * LANE] = y.T.astype(o_ref.dtype)


def _diffattn(qt, proj, vt, da_lambda, subln_g_col, lam_init):
    bsz, nt, _, tq = qt.shape
    seq = nt * tq
    hp = DA_HEADS_PER_STEP
    body = functools.partial(_diffattn_body, lam_init=lam_init)
    return pl.pallas_call(
        body,
        grid=(bsz, DA_HEADS // hp, nt),
        in_specs=[
            pl.BlockSpec((1, 1, hp * DA_VAL_DIM, tq), lambda b, h, i: (b, i, h, 0)),
            pl.BlockSpec((1, seq, hp * LANE), lambda b, h, i: (b, 0, h)),
            pl.BlockSpec((1, nt, hp * DA_VAL_DIM, tq), lambda b, h, i: (b, 0, h, 0)),
            pl.BlockSpec(da_lambda.shape, lambda b, h, i: (0, 0)),
            pl.BlockSpec((DA_VAL_DIM, 1), lambda b, h, i: (0, 0)),
        ],
        out_specs=pl.BlockSpec((1, tq, hp * LANE), lambda b, h, i: (b, i, h)),
        out_shape=jax.ShapeDtypeStruct((bsz, seq, DA_HEADS * DA_VAL_DIM), BF16),
        scratch_shapes=[
            pltpu.VMEM((1, hp * 2 * tq), F32),
            pltpu.VMEM((DA_VAL_DIM + 2 * SUBLANE, hp * 2 * tq), F32),
            pltpu.VMEM((tq, hp * 2 * tq), BF16),
        ],
        compiler_params=_params("parallel", "parallel", "arbitrary"),
        name="diffattn",
    )(qt, proj, vt, da_lambda, subln_g_col)


def _log_sigmoid(x):
    return jnp.minimum(x, 0.0) - jnp.log(1.0 + jnp.exp(-jnp.abs(x)))


def _sigmoid(x):
    return 1.0 / (1.0 + jnp.exp(-x))


def _mlstm_body(q_ref, k_ref, vt_ref, ot_ref, gcol_ref, grow_ref, cw_ref, cb_ref, gbcol_ref, gbrow_ref, ng_ref,
                y_ref, xbuf, c_scr, m_scr):
    @pl.when(pl.program_id(1) == 0)
    def _():
        xbuf[...] = jnp.zeros(xbuf.shape, xbuf.dtype)
        c_scr[...] = jnp.zeros(c_scr.shape, F32)
        m_scr[...] = jnp.zeros(m_scr.shape, F32)

    for bb in range(q_ref.shape[0]):
        _mlstm_row(bb, q_ref, k_ref, vt_ref, ot_ref, gcol_ref, grow_ref, cw_ref, cb_ref, gbcol_ref, gbrow_ref,
                   ng_ref, y_ref, xbuf, c_scr, m_scr)


def _mlstm_row(bb, q_ref, k_ref, vt_ref, ot_ref, gcol_ref, grow_ref, cw_ref, cb_ref, gbcol_ref, gbrow_ref, ng_ref,
               y_ref, xbuf, c_scr, m_scr):
    L = q_ref.shape[1]
    width = q_ref.shape[2]
    dh = width // ML_HEADS
    tail = xbuf.shape[1]
    aug = c_scr.shape[1]

    u = jnp.concatenate([q_ref[bb], k_ref[bb]], axis=1)
    u_ext = jnp.concatenate([xbuf[bb], u], axis=0)
    src = lax.broadcasted_iota(jnp.int32, (L, L + tail), 1) - lax.broadcasted_iota(jnp.int32, (L, L + tail), 0)
    conv = cb_ref[...] + cw_ref[ML_CONV - 1:ML_CONV, :] * u.astype(F32)
    for back in range(1, ML_CONV):
        shift = jnp.where(src == tail - back, 1.0, 0.0).astype(BF16)
        conv = conv + cw_ref[ML_CONV - 1 - back:ML_CONV - back, :] * _dot(shift, u_ext)
    xbuf[bb] = u[L - tail:, :]
    qk = conv * _sigmoid(conv)

    gc = gcol_ref[bb] + gbcol_ref[...]
    gr = grow_ref[bb] + gbrow_ref[...]
    row_id = lax.broadcasted_iota(jnp.int32, (L, L), 0)
    col_id = lax.broadcasted_iota(jnp.int32, (L, L), 1)
    causal = row_id <= col_id
    tri = jnp.where(col_id <= row_id, 1.0, 0.0).astype(BF16)
    tri_t = jnp.where(causal, 1.0, 0.0).astype(BF16)
    lfc_hi, lfc_lo = _split_bf16(_log_sigmoid(gc))
    lfr_hi, lfr_lo = _split_bf16(_log_sigmoid(gr))
    a_c = _dot(tri, lfc_hi) + _dot(tri, lfc_lo)
    a_r = _dot(lfr_hi, tri_t) + _dot(lfr_lo, tri_t)

    ones_rows = jnp.where(lax.broadcasted_iota(jnp.int32, (aug - dh, L), 0) == 0, 1.0, 0.0).astype(BF16)

    for h in range(ML_HEADS):
        fh = ML_HEADS + h
        st = bb * ML_HEADS + h
        m_prev = m_scr[st:st + 1, 0:1]
        a_row = a_r[fh:fh + 1, :]
        ig_row = gr[h:h + 1, :]
        nb = jnp.where(causal, gc[:, h:h + 1] - a_c[:, fh:fh + 1], NEG_INF)
        m_row = a_row + jnp.maximum(m_prev, jnp.max(nb, axis=0, keepdims=True))
        w_intra = jnp.exp(nb + (a_row - m_row))
        w_inter = jnp.exp(a_row + m_prev - m_row)

        qh = qk[:, h * dh:(h + 1) * dh].astype(BF16)
        kh = (qk[:, width + h * dh:width + (h + 1) * dh] * (dh ** -0.5)).astype(BF16)
        p_t = (_dot_nt(kh, qh) * w_intra).astype(BF16)
        v_aug = jnp.concatenate([vt_ref[bb, 0, h * dh:(h + 1) * dh, :], ones_rows], axis=0)
        c_t = c_scr[st]
        num = w_inter * _dot_nt(c_t.astype(BF16), qh) + _dot(v_aug, p_t)
        scale = 1.0 / jnp.maximum(jnp.abs(num[dh:dh + 1, :]), jnp.exp(-m_row))
        hid = num[0:dh, :] * scale

        g_last = a_row[:, L - 1:L]
        log_w = g_last - a_row + ig_row
        m_new = jnp.maximum(g_last + m_prev, jnp.max(log_w, axis=1, keepdims=True))
        w_state = jnp.exp(log_w - m_new)
        decay = jnp.exp(g_last + m_prev - m_new)
        wv = (w_state * v_aug.astype(F32)).astype(BF16)
        c_scr[st] = decay * c_t + _dot(wv, kh)
        m_scr[st:st + 1, :] = jnp.broadcast_to(m_new, (1, m_scr.shape[1]))

        mu = jnp.mean(hid, axis=0, keepdims=True)
        cen = hid - mu
        var = jnp.mean(cen * cen, axis=0, keepdims=True)
        g_b = ng_ref[h * dh:(h + 1) * dh, :]
        hn = cen * lax.rsqrt(var + EPS) * jnp.concatenate([g_b] * (L // LANE), axis=1)
        gate = _sigmoid(ot_ref[bb, 0, h * dh:(h + 1) * dh, :].astype(F32))
        y_ref[bb, :, h * dh:(h + 1) * dh] = (gate * hn).T.astype(y_ref.dtype)


def _mlstm(proj, mvt, mot, gcol, grow, conv_w, conv_b, gb_col, gb_row, norm_g_b, col0):
    bsz, seq, _ = proj.shape
    width = norm_g_b.shape[0]
    dh = width // ML_HEADS
    L = min(ML_CHUNK, seq)
    tm = mvt.shape[-1]
    per = tm // L
    blk0 = col0 // width
    nb = ML_BATCH_PER_STEP if bsz % ML_BATCH_PER_STEP == 0 else 1
    spec = lambda k: pl.BlockSpec((nb, L, width), lambda b, c: (b, c, blk0 + k))
    tspec = pl.BlockSpec((nb, 1, width, L), lambda b, c: (b, c // per, 0, c % per))
    return pl.pallas_call(
        _mlstm_body,
        grid=(bsz // nb, seq // L),
        in_specs=[
            spec(0), spec(1), tspec, tspec,
            pl.BlockSpec((nb, L, LANE), lambda b, c: (b, c, 0)),
            pl.BlockSpec((nb, SUBLANE, L), lambda b, c: (b, 0, c)),
            pl.BlockSpec(conv_w.shape, lambda b, c: (0, 0)),
            pl.BlockSpec(conv_b.shape, lambda b, c: (0, 0)),
            pl.BlockSpec(gb_col.shape, lambda b, c: (0, 0)),
            pl.BlockSpec(gb_row.shape, lambda b, c: (0, 0)),
            pl.BlockSpec(norm_g_b.shape, lambda b, c: (0, 0)),
        ],
        out_specs=pl.BlockSpec((nb, L, width), lambda b, c: (b, c, 0)),
        out_shape=jax.ShapeDtypeStruct((bsz, seq, width), BF16),
        scratch_shapes=[
            pltpu.VMEM((nb, 2 * SUBLANE, 2 * width), BF16),
            pltpu.VMEM((nb * ML_HEADS, dh + 2 * SUBLANE, dh), F32),
            pltpu.VMEM((nb * ML_HEADS, LANE), F32),
        ],
        compiler_params=_params("parallel", "arbitrary"),
        name="mlstm",
    )(proj, proj, mvt, mot, gcol, grow, conv_w, conv_b, gb_col, gb_row, norm_g_b)


def _memkv_body(mem_ref, g_ref, wk_ref, wv_ref, k_ref, v_ref):
    mb = _rms(mem_ref[0], g_ref[...]).astype(BF16)
    k_ref[0] = _dot(mb, wk_ref[...]).astype(BF16)
    v_ref[0] = _dot(mb, wv_ref[...]).astype(BF16)


def _memkv(mem, g, wk, wv):
    bsz, mlen, d = mem.shape
    full = lambda a: pl.BlockSpec(a.shape, lambda b: (0,) * a.ndim)
    blk = pl.BlockSpec((1, mlen, d), lambda b: (b, 0, 0))
    return pl.pallas_call(
        _memkv_body,
        grid=(bsz,),
        in_specs=[blk, full(g), full(wk), full(wv)],
        out_specs=[blk, blk],
        out_shape=[jax.ShapeDtypeStruct((bsz, mlen, d), BF16)] * 2,
        compiler_params=_params("parallel"),
        name="memkv",
    )(mem, g, wk, wv)


def _mix_body(x_ref, yda_ref, yml_ref, wo1_ref, wo2_ref, gx_ref, wq_ref, km_ref, vm_ref, wxo_ref, gf_ref,
              wrh_ref, wrl_ref, br_ref, x2_ref, route_ref):
    x1 = x_ref[0] + _dot(yda_ref[0], wo1_ref[...]) + _dot(yml_ref[0], wo2_ref[...])

    d = x1.shape[-1]
    dh = d // X_HEADS
    q = _dot(_rms(x1, gx_ref[...]).astype(BF16), wq_ref[...]).astype(BF16)
    heads = []
    for h in range(X_HEADS):
        sl = slice(h * dh, (h + 1) * dh)
        s = _dot_nt(q[:, sl], km_ref[0, :, sl]) * (dh ** -0.5)
        p = jnp.exp(s - jnp.max(s, axis=-1, keepdims=True))
        o = _dot(p.astype(BF16), vm_ref[0, :, sl]) / jnp.sum(p, axis=-1, keepdims=True)
        heads.append(o.astype(BF16))
    x2 = x1 + _dot(jnp.concatenate(heads, axis=1), wxo_ref[...])
    _to_token_tiles(x2_ref.at[0], 0, x2)

    h3 = _rms(x2, gf_ref[...])
    h_hi, h_lo = _split_bf16(h3)
    logits = (_dot_nt(wrh_ref[...], h_hi) + _dot_nt(wrh_ref[...], h_lo) + _dot_nt(wrl_ref[...], h_hi)
              + br_ref[...])
    tm = logits.shape[1]
    gl = logits[0:N_GROUPS, :]
    g_iota = lax.broadcasted_iota(jnp.int32, gl.shape, 0)
    g_max = jnp.max(gl, axis=0, keepdims=True)
    g_idx = jnp.min(jnp.where(gl == g_max, g_iota, N_GROUPS), axis=0, keepdims=True)
    g_w = 1.0 / jnp.sum(jnp.exp(gl - g_max), axis=0, keepdims=True)
    el = jnp.zeros((EXPERTS_PER_GROUP, tm), F32)
    for g in range(N_GROUPS):
        lo = N_GROUPS + g * EXPERTS_PER_GROUP
        el = jnp.where(g_idx == g, logits[lo:lo + EXPERTS_PER_GROUP, :], el)
    e_iota = lax.broadcasted_iota(jnp.int32, el.shape, 0)
    e1 = jnp.max(el, axis=0, keepdims=True)
    i1 = jnp.min(jnp.where(el == e1, e_iota, EXPERTS_PER_GROUP), axis=0, keepdims=True)
    el2 = jnp.where(e_iota == i1, NEG_INF, el)
    e2 = jnp.max(el2, axis=0, keepdims=True)
    i2 = jnp.min(jnp.where(el2 == e2, e_iota, EXPERTS_PER_GROUP), axis=0, keepdims=True)
    p2 = jnp.exp(e2 - e1)
    w1 = g_w / (1.0 + p2)
    w2 = g_w * p2 / (1.0 + p2)
    base = g_idx * EXPERTS_PER_GROUP
    zero = jnp.zeros((1, tm), F32)
    route_ref[0] = jnp.concatenate(
        [(base + i1).astype(F32), (base + i2).astype(F32), w1, w2, zero, zero, zero, zero], axis=0)


def _mix(x, yda, yml, wo1, wo2, gx, wq, kmem, vmem, wxo, gf, wr_hi, wr_lo, br):
    bsz, seq, d = x.shape
    tm = min(ROW_TILE, seq)
    mlen = kmem.shape[1]
    full = lambda a: pl.BlockSpec(a.shape, lambda b, i: (0,) * a.ndim)
    tok = lambda w: pl.BlockSpec((1, tm, w), lambda b, i: (b, i, 0))
    memspec = pl.BlockSpec((1, mlen, d), lambda b, i: (b, 0, 0))
    return pl.pallas_call(
        _mix_body,
        grid=(bsz, seq // tm),
        in_specs=[tok(d), tok(yda.shape[-1]), tok(yml.shape[-1]), full(wo1), full(wo2), full(gx), full(wq),
                  memspec, memspec, full(wxo), full(gf), full(wr_hi), full(wr_lo), full(br)],
        out_specs=[pl.BlockSpec((1, tm * SUBLANE, LANE), lambda b, i: (b, i, 0)),
                   pl.BlockSpec((1, SUBLANE, tm), lambda b, i: (b, 0, i))],
        out_shape=[
            jax.ShapeDtypeStruct((bsz, seq * d // LANE, LANE), F32),
            jax.ShapeDtypeStruct((bsz, SUBLANE, seq), F32),
        ],
        compiler_params=_params("parallel", "parallel"),
        name="mix_xattn_router",
    )(x, yda, yml, wo1, wo2, gx, wq, kmem, vmem, wxo, gf, wr_hi, wr_lo, br)


def _to_token_tiles(ref, first_row, dense):
    n, d = dense.shape
    for c in range(d // LANE):
        ref[pl.ds(first_row * SUBLANE + c, n, stride=SUBLANE), :] = dense[:, c * LANE:(c + 1) * LANE]


def _from_token_tiles(ref, first_row, n):
    return jnp.concatenate(
        [ref[pl.ds(first_row * SUBLANE + c, n, stride=SUBLANE), :] for c in range(SUBLANE)], axis=1)


def _moe_body(start_ref, cnt_ref, x_hbm, tok_hbm, wt_hbm, gf_ref, gfin_ref, w1_hbm, w3_hbm, w2_hbm, out_hbm,
              acc, h3, xg, yt, stage, tok_s, wt_s, w1_buf, w3_buf, w2_buf, sems, load_sems, store_sems, w_sems,
              *, final_norm):
    tile = pl.program_id(0)
    tt = h3.shape[0] // SUBLANE
    chunk_rows = stage.shape[1]
    group = SUBLANE
    depth = w1_buf.shape[0]
    n_steps = pl.num_programs(0) * N_EXPERTS

    def weight_copies(g):
        expert = g % N_EXPERTS
        slot = g % depth
        return [pltpu.make_async_copy(src.at[expert], dst.at[slot], w_sems.at[slot * 3 + k])
                for k, (src, dst) in enumerate(((w1_hbm, w1_buf), (w3_hbm, w3_buf), (w2_hbm, w2_buf)))]

    @pl.when(tile == 0)
    def _():
        for ahead in range(depth - 1):
            for cp in weight_copies(ahead):
                cp.start()

    def row_tile(ref, row0):
        return ref.at[pl.ds(pl.multiple_of(row0, SUBLANE), SUBLANE), :]

    def token_tile(ref, r):
        return ref.at[pl.ds(r * SUBLANE, SUBLANE), :]

    spare_row = tt * SUBLANE

    n_chunks = tt // chunk_rows
    chunk_tiles = chunk_rows * SUBLANE

    def load_chunk(i):
        rows = pl.ds(pl.multiple_of(i * chunk_tiles, chunk_tiles), chunk_tiles)
        return pltpu.make_async_copy(x_hbm.at[tile, rows, :], acc.at[rows, :], load_sems.at[i])

    lists = [pltpu.make_async_copy(tok_hbm.at[tile], tok_s, sems.at[0]),
             pltpu.make_async_copy(wt_hbm.at[tile], wt_s, sems.at[1])]
    for cp in lists:
        cp.start()
    for i in range(n_chunks):
        load_chunk(i).start()
    token_tile(acc, tt)[...] = jnp.zeros((SUBLANE, LANE), F32)

    def ffn_norm(i, c):
        load_chunk(i).wait()
        r0 = i * chunk_rows
        _to_token_tiles(h3, r0, _rms(_from_token_tiles(acc, r0, chunk_rows), gf_ref[...]))
        return c
    lax.fori_loop(0, n_chunks, ffn_norm, 0)
    for cp in lists:
        cp.wait()

    big = xg.shape[0] // SUBLANE
    small = MOE_ROWS

    def expert_step(e, carry):
        step_id = tile * N_EXPERTS + e

        @pl.when(step_id + depth - 1 < n_steps)
        def _():
            for cp in weight_copies(step_id + depth - 1):
                cp.start()

        for cp in weight_copies(step_id):
            cp.wait()
        w_slot = step_id % depth
        start = start_ref[step_id]
        count = cnt_ref[step_id]

        def run_block(first, cnt, rows):
            idx0 = start + first
            for r in range(rows):
                token_tile(xg, r)[...] = row_tile(h3, tok_s[idx0 + r])[...]

            xb = _from_token_tiles(xg, 0, rows).astype(BF16)
            a = _dot(xb, w1_buf[w_slot])
            b = _dot(xb, w3_buf[w_slot])
            hmid = (a * _sigmoid(a) * b).astype(BF16)
            _to_token_tiles(yt, 0, _dot(hmid, w2_buf[w_slot]))

            for g0 in range(0, rows, group):
                dst = [jnp.where(g0 + u < cnt, tok_s[idx0 + g0 + u], spare_row) for u in range(group)]
                new = [row_tile(acc, dst[u])[...] + wt_s[idx0 + g0 + u] * token_tile(yt, g0 + u)[...]
                       for u in range(group)]
                for u in range(group):
                    row_tile(acc, dst[u])[...] = new[u]

        @pl.when(count > 0)
        def _():
            run_block(0, jnp.minimum(big, count), big)

        def small_block(i, c):
            first = big + i * small
            run_block(first, jnp.minimum(small, count - first), small)
            return c

        lax.fori_loop(0, (jnp.maximum(count - big, 0) + small - 1) // small, small_block, 0)
        return carry

    lax.fori_loop(0, N_EXPERTS, expert_step, 0)

    n_slots = stage.shape[0]

    def store_chunk(i, slot):
        r0 = pl.multiple_of(i * chunk_rows, chunk_rows)
        return pltpu.make_async_copy(stage.at[slot], out_hbm.at[tile, pl.ds(r0, chunk_rows), :],
                                     store_sems.at[slot])

    def write_back(p, c):
        for slot in range(n_slots):
            i = p * n_slots + slot

            @pl.when(p > 0)
            def _():
                store_chunk(i - n_slots, slot).wait()

            rows = _from_token_tiles(acc, i * chunk_rows, chunk_rows)
            stage[slot] = _rms(rows, gfin_ref[...]) if final_norm else rows
            store_chunk(i, slot).start()
        return c
    lax.fori_loop(0, n_chunks // n_slots, write_back, 0)
    for slot in range(n_slots):
        store_chunk(n_chunks - n_slots + slot, slot).wait()


def _moe(start, cnt, tok, wt, x_token_tiles, gf, gfin, w1, w3, w2, final_norm):
    ntiles, tile_rows, _ = x_token_tiles.shape
    tt = tile_rows // SUBLANE
    d = SUBLANE * LANE
    de = w1.shape[-1]
    seg = tok.shape[1]
    body = functools.partial(_moe_body, final_norm=final_norm)
    hbm = pl.BlockSpec(memory_space=pl.ANY)
    depth = MOE_WEIGHT_SLOTS
    grid_spec = pltpu.PrefetchScalarGridSpec(
        num_scalar_prefetch=2,
        grid=(ntiles,),
        in_specs=[
            hbm, hbm, hbm,
            pl.BlockSpec((1, d), lambda t, *_: (0, 0)),
            pl.BlockSpec((1, d), lambda t, *_: (0, 0)),
            hbm, hbm, hbm,
        ],
        out_specs=hbm,
        scratch_shapes=[
            pltpu.VMEM((tile_rows + SUBLANE, LANE), F32),
            pltpu.VMEM((tile_rows, LANE), F32),
            pltpu.VMEM((MOE_FIRST_ROWS * SUBLANE, LANE), F32),
            pltpu.VMEM((MOE_FIRST_ROWS * SUBLANE, LANE), F32),
            pltpu.VMEM((2, 2 * MOE_ROWS, d), F32),
            pltpu.SMEM((seg,), jnp.int32),
            pltpu.SMEM((seg,), F32),
            pltpu.VMEM((depth, d, de), BF16),
            pltpu.VMEM((depth, d, de), BF16),
            pltpu.VMEM((depth, de, d), BF16),
            pltpu.SemaphoreType.DMA((2,)),
            pltpu.SemaphoreType.DMA((tt // (2 * MOE_ROWS),)),
            pltpu.SemaphoreType.DMA((2,)),
            pltpu.SemaphoreType.DMA((3 * depth,)),
        ],
    )
    return pl.pallas_call(
        body,
        grid_spec=grid_spec,
        out_shape=jax.ShapeDtypeStruct((ntiles, tt, d), F32),
        compiler_params=_params("arbitrary"),
        name="moe",
    )(start, cnt, x_token_tiles, tok, wt, gf, gfin, w1, w3, w2)


def _rope_tables(seq):
    half = DA_HEAD_DIM // 2
    inv_freq = ROPE_THETA ** (-jnp.arange(0, DA_HEAD_DIM, 2, dtype=F32) / DA_HEAD_DIM)
    ang = jnp.arange(seq, dtype=F32)[:, None] * inv_freq[None, :]
    cos, sin = jnp.cos(ang), jnp.sin(ang)
    reps = LANE // half
    cos_t = jnp.tile(cos, (1, reps))
    sin_t = jnp.tile(jnp.concatenate([-sin, sin], axis=1), (1, reps // 2))
    return cos_t, sin_t


def _sort_assignments(route, tt):
    bsz, _, seq = route.shape
    per = seq // tt
    ntiles = bsz * per
    tile = lambda a: a.reshape(bsz, TOP_K_ROWS, per, tt).transpose(0, 2, 1, 3).reshape(ntiles, TOP_K_ROWS * tt)
    keys = tile(route[:, 0:TOP_K_ROWS, :]).astype(jnp.int32)
    wts = tile(route[:, TOP_K_ROWS:2 * TOP_K_ROWS, :])
    order = jnp.argsort(keys, axis=-1).astype(jnp.int32)
    tok = (order % tt) * SUBLANE
    wt = jnp.take_along_axis(wts, order, axis=-1)
    counts = jnp.sum(keys[:, :, None] == jnp.arange(N_EXPERTS, dtype=jnp.int32)[None, None, :], axis=1,
                     dtype=jnp.int32)
    start = jnp.cumsum(counts, axis=-1, dtype=jnp.int32) - counts
    over = ((0, 0), (0, MOE_FIRST_ROWS))
    return start.reshape(-1), counts.reshape(-1), jnp.pad(tok, over), jnp.pad(wt, over)


TOP_K_ROWS = 2


def kernel(x, mem, norm_mix_g, w_in, w_out, da_lambda, da_subln_g, ml_conv_w, ml_conv_b, ml_gate_b, ml_norm_g, norm_x_g, norm_mem_g, w_xq, w_xk, w_xv, w_xo, norm_ffn_g, w_router_group, b_router_group, w_router_expert, b_router_expert, w1, w3, w2, norm_final_g):
    depth = w_in.shape[0]
    bsz, seq, d = x.shape
    da_qk = DA_HEADS * 2 * DA_HEAD_DIM
    da_width = DA_HEADS * DA_VAL_DIM
    ml_width = ml_norm_g.shape[-1]
    n_main = 2 * da_qk + da_width + 4 * ml_width
    n_in = w_in.shape[-1]
    cos_t, sin_t = _rope_tables(seq)
    row = lambda v: v.reshape(1, -1).astype(F32)

    for l in range(depth):
        lam_init = 0.8 - 0.6 * math.exp(-0.3 * l)
        w_pad = jnp.pad(w_in[l], ((0, 0), (0, n_main + LANE - n_in))).astype(BF16)
        qt, vt, mvt, mot, proj, gcol, grow = _inproj(x, row(norm_mix_g[l]), w_pad, cos_t, sin_t)

        yda = _diffattn(qt, proj, vt, da_lambda[l].astype(F32), da_subln_g[l].astype(F32).reshape(-1, 1), lam_init)

        gb = ml_gate_b[l].astype(F32).reshape(1, 2 * ML_HEADS)
        gb_col = jnp.pad(gb, ((0, 0), (0, LANE - 2 * ML_HEADS)))
        gb_row = gb.reshape(2 * ML_HEADS, 1)
        ng_b = jnp.broadcast_to(ml_norm_g[l].astype(F32)[:, None], (ml_width, LANE))
        yml = _mlstm(proj, mvt, mot, gcol, grow, ml_conv_w[l].reshape(ML_CONV, 2 * ml_width).astype(F32),
                     row(ml_conv_b[l]), gb_col, gb_row, ng_b, da_qk)

        kmem, vmem = _memkv(mem, row(norm_mem_g[l]), w_xk[l].astype(BF16), w_xv[l].astype(BF16))

        wr = jnp.concatenate([w_router_group[l], w_router_expert[l]], axis=1).astype(F32).T
        wr = jnp.pad(wr, ((0, ROUTER_ROWS - wr.shape[0]), (0, 0)))
        wr_hi = wr.astype(BF16)
        wr_lo = (wr - wr_hi.astype(F32)).astype(BF16)
        br = jnp.pad(jnp.concatenate([b_router_group[l], b_router_expert[l]]).astype(F32),
                     (0, ROUTER_ROWS - N_GROUPS - N_EXPERTS)).reshape(ROUTER_ROWS, 1)
        wo = w_out[l].astype(BF16)
        x2, route = _mix(x, yda, yml, wo[:da_width], wo[da_width:], row(norm_x_g[l]), w_xq[l].astype(BF16),
                         kmem, vmem, w_xo[l].astype(BF16), row(norm_ffn_g[l]), wr_hi, wr_lo, br)

        tt = min(MOE_TILE, seq)
        x = _moe(*_sort_assignments(route, tt), x2.reshape(bsz * seq // tt, tt * d // LANE, LANE),
                 row(norm_ffn_g[l]), row(norm_final_g),
                 w1[l].astype(BF16), w3[l].astype(BF16), w2[l].astype(BF16),
                 final_norm=(l + 1 == depth)).reshape(bsz, seq, d)
    return x
```

```python
import functools
import math

import jax
import jax.numpy as jnp
from jax import lax
from jax.experimental import pallas as pl
from jax.experimental.pallas import tpu as pltpu

F32 = jnp.float32
BF16 = jnp.bfloat16

LANE = 128
SUBLANE = 8
VMEM_LIMIT_BYTES = 56 * 1024 * 1024

EPS = 1e-6
ROPE_THETA = 10000.0
DA_HEADS = 4
DA_HEAD_DIM = 64
DA_VAL_DIM = 2 * DA_HEAD_DIM
ML_HEADS = 4
ML_CONV = 4
X_HEADS = 4
N_GROUPS = 4
EXPERTS_PER_GROUP = 8
N_EXPERTS = N_GROUPS * EXPERTS_PER_GROUP
ROUTER_ROWS = 40

ROW_TILE = 512
MIX_CHAINS = 2
DA_HEADS_PER_STEP = 4
ML_CHUNK = 256
ML_BATCH_PER_STEP = 4
MOE_TILE = 4096
MOE_ROWS = 128
MOE_FIRST_ROWS = 288
MOE_WEIGHT_SLOTS = 3

NEG_INF = float("-inf")
LOG2_E = math.log2(math.e)


def _rms(x, g):
    return x * lax.rsqrt(jnp.mean(x * x, axis=-1, keepdims=True) + EPS) * g


def _dot(a, b):
    return jnp.dot(a, b, preferred_element_type=F32)


def _dot_nt(a, b):
    return lax.dot_general(a, b, (((1,), (1,)), ((), ())), preferred_element_type=F32)


def _dot_tn(a, b):
    return lax.dot_general(a, b, (((0,), (0,)), ((), ())), preferred_element_type=F32)


def _split_bf16(x):
    hi = x.astype(BF16)
    lo = (x - hi.astype(F32)).astype(BF16)
    return hi, lo


def _params(*semantics):
    return pltpu.CompilerParams(dimension_semantics=semantics, vmem_limit_bytes=VMEM_LIMIT_BYTES)


def _inproj_body(x_ref, g_ref, w_ref, cos_ref, sin_ref, qt_ref, vt_ref, mvt_ref, mot_ref, proj_ref, gcol_ref,
                 grow_ref):
    hb = _rms(x_ref[0], g_ref[...]).astype(BF16)
    cos = cos_ref[...]
    sin = sin_ref[...]
    lane = lax.broadcasted_iota(jnp.int32, cos.shape, 1)
    first_half = (lane % DA_HEAD_DIM) < (DA_HEAD_DIM // 2)
    n_main = w_ref.shape[1] - LANE
    wide = 4 * LANE
    for c in range(n_main // LANE):
        if c % (wide // LANE) == 0:
            acc = _dot(hb, w_ref[:, c * LANE:c * LANE + wide])
        t = acc[:, (c * LANE) % wide:(c * LANE) % wide + LANE]
        if c < 2 * DA_HEADS:
            rot = jnp.where(first_half, pltpu.roll(t, LANE - DA_HEAD_DIM // 2, 1),
                            pltpu.roll(t, DA_HEAD_DIM // 2, 1))
            t = t * cos + rot * sin
        if c < DA_HEADS:
            qt_ref[0, 0, c * LANE:(c + 1) * LANE, :] = (t * (DA_HEAD_DIM ** -0.5 * LOG2_E)).T.astype(BF16)
        elif c < 2 * DA_HEADS:
            proj_ref[0, :, (c - DA_HEADS) * LANE:(c - DA_HEADS + 1) * LANE] = t.astype(BF16)
        elif c < 3 * DA_HEADS:
            vt_ref[0, 0, (c - 2 * DA_HEADS) * LANE:(c - 2 * DA_HEADS + 1) * LANE, :] = t.T.astype(BF16)
        elif c < 3 * DA_HEADS + 2 * ML_HEADS:
            proj_ref[0, :, (c - 2 * DA_HEADS) * LANE:(c - 2 * DA_HEADS + 1) * LANE] = t.astype(BF16)
        elif c < 3 * DA_HEADS + 3 * ML_HEADS:
            m = c - 3 * DA_HEADS - 2 * ML_HEADS
            mvt_ref[0, 0, m * LANE:(m + 1) * LANE, :] = t.T.astype(BF16)
        else:
            m = c - 3 * DA_HEADS - 3 * ML_HEADS
            mot_ref[0, 0, m * LANE:(m + 1) * LANE, :] = t.T.astype(BF16)
    gates = _dot(hb, w_ref[:, n_main:n_main + LANE])
    gcol_ref[0] = gates
    grow_ref[0] = gates.T[:SUBLANE, :]


def _inproj(x, g, w_pad, cos_t, sin_t):
    bsz, seq, d = x.shape
    tm = min(ROW_TILE, seq)
    nt = seq // tm
    da_w = DA_HEADS * DA_VAL_DIM
    ml_w = ML_HEADS * LANE
    n_proj = w_pad.shape[1] - LANE - 2 * da_w - 2 * ml_w
    return pl.pallas_call(
        _inproj_body,
        grid=(bsz, nt),
        in_specs=[
            pl.BlockSpec((1, tm, d), lambda b, i: (b, i, 0)),
            pl.BlockSpec((1, d), lambda b, i: (0, 0)),
            pl.BlockSpec(w_pad.shape, lambda b, i: (0, 0)),
            pl.BlockSpec((tm, LANE), lambda b, i: (i, 0)),
            pl.BlockSpec((tm, LANE), lambda b, i: (i, 0)),
        ],
        out_specs=[
            pl.BlockSpec((1, 1, da_w, tm), lambda b, i: (b, i, 0, 0)),
            pl.BlockSpec((1, 1, da_w, tm), lambda b, i: (b, i, 0, 0)),
            pl.BlockSpec((1, 1, ml_w, tm), lambda b, i: (b, i, 0, 0)),
            pl.BlockSpec((1, 1, ml_w, tm), lambda b, i: (b, i, 0, 0)),
            pl.BlockSpec((1, tm, n_proj), lambda b, i: (b, i, 0)),
            pl.BlockSpec((1, tm, LANE), lambda b, i: (b, i, 0)),
            pl.BlockSpec((1, SUBLANE, tm), lambda b, i: (b, 0, i)),
        ],
        out_shape=[
            jax.ShapeDtypeStruct((bsz, nt, da_w, tm), BF16),
            jax.ShapeDtypeStruct((bsz, nt, da_w, tm), BF16),
            jax.ShapeDtypeStruct((bsz, nt, ml_w, tm), BF16),
            jax.ShapeDtypeStruct((bsz, nt, ml_w, tm), BF16),
            jax.ShapeDtypeStruct((bsz, seq, n_proj), BF16),
            jax.ShapeDtypeStruct((bsz, seq, LANE), F32),
            jax.ShapeDtypeStruct((bsz, SUBLANE, seq), F32),
        ],
        compiler_params=_params("parallel", "parallel"),
        name="inproj",
    )(x, g, w_pad, cos_t, sin_t)


def _diffattn_body(qt_ref, k_ref, vt_ref, lam_ref, g_ref, o_ref, m_scr, acc_scr, p_scr, *, lam_init):
    qi = pl.program_id(2)
    tq = qt_ref.shape[3]
    heads = qt_ref.shape[2] // DA_VAL_DIM
    q2t = []
    for g in range(heads):
        qt = qt_ref[0, 0, g * DA_VAL_DIM:(g + 1) * DA_VAL_DIM, :]
        row = lax.broadcasted_iota(jnp.int32, qt.shape, 0)
        zero = jnp.zeros_like(qt)
        q2t.append(jnp.concatenate([jnp.where(row < DA_HEAD_DIM, qt, zero),
                                    jnp.where(row >= DA_HEAD_DIM, qt, zero)], axis=1))
    pad_rows = acc_scr.shape[0] - DA_VAL_DIM
    ones_rows = jnp.where(lax.broadcasted_iota(jnp.int32, (pad_rows, tq), 0) == 0, 1.0, 0.0).astype(BF16)

    m_scr[...] = jnp.full(m_scr.shape, NEG_INF, F32)
    acc_scr[...] = jnp.zeros(acc_scr.shape, F32)

    n_chains = 4
    width = 2 * tq // n_chains
    chains = tuple((g, slice(c * width, (c + 1) * width),
                    slice(g * 2 * tq + c * width, g * 2 * tq + (c + 1) * width))
                   for c in range(n_chains) for g in range(heads))

    def scores(j, masked, prev_values):
        out = []
        for g, local, cols in chains:
            k = k_ref[0, pl.ds(pl.multiple_of(j * tq, tq), tq), g * LANE:(g + 1) * LANE]
            s = _dot(k, q2t[g][:, local])
            pv = values(j - 1, g, cols) if prev_values else None
            if masked:
                key = lax.broadcasted_iota(jnp.int32, s.shape, 0)
                qry = lax.broadcasted_iota(jnp.int32, s.shape, 1) + local.start % tq
                s = jnp.where(key <= qry, s, NEG_INF)
            m_prev = m_scr[:, cols]
            m_new = jnp.maximum(m_prev, jnp.max(s, axis=0, keepdims=True))
            p_scr[:, cols] = jnp.exp2(s - m_new).astype(BF16)
            m_scr[:, cols] = m_new
            out.append((jnp.exp2(m_prev - m_new), pv))
        return out

    def values(j, g, cols):
        v_aug = jnp.concatenate([vt_ref[0, j, g * DA_VAL_DIM:(g + 1) * DA_VAL_DIM, :], ones_rows], axis=0)
        return _dot(v_aug, p_scr[:, cols])

    def step(j, masked):
        for (_, _, cols), (alpha, pv) in zip(chains, scores(j, masked, True)):
            acc_scr[:, cols] = alpha * (acc_scr[:, cols] + pv)

    def full_step_pair(i, carry):
        step(1 + 2 * i, False)
        step(2 + 2 * i, False)
        return carry

    @pl.when(qi == 0)
    def _():
        scores(0, True, False)

    @pl.when(qi > 0)
    def _():
        scores(0, False, False)
        lax.fori_loop(0, (qi - 1) // 2, full_step_pair, 0)

        @pl.when((qi - 1) % 2 == 1)
        def _():
            step(qi - 1, False)

        step(qi, True)

    for g, _, cols in chains:
        acc_scr[:, cols] = acc_scr[:, cols] + values(qi, g, cols)

    lv = lam_ref[...]
    lam = (jnp.exp(jnp.sum(lv[0:1, :] * lv[1:2, :], axis=-1, keepdims=True))
           - jnp.exp(jnp.sum(lv[2:3, :] * lv[3:4, :], axis=-1, keepdims=True)) + lam_init)
    for g in range(heads):
        acc = acc_scr[:, g * 2 * tq:(g + 1) * 2 * tq]
        o = acc[0:DA_VAL_DIM, :] / acc[DA_VAL_DIM:DA_VAL_DIM + 1, :]
        o = o[:, :tq] - lam * o[:, tq:]
        y = o * lax.rsqrt(jnp.mean(o * o, axis=0, keepdims=True) + EPS) * g_ref[...] * (1.0 - lam_init)
        o_ref[0, :, g * LANE:(g + 1) * LANE] = y.T.astype(o_ref.dtype)


def _diffattn(qt, proj, vt, da_lambda, subln_g_col, lam_init):
    bsz, nt, _, tq = qt.shape
    seq = nt * tq
    hp = DA_HEADS_PER_STEP
    body = functools.partial(_diffattn_body, lam_init=lam_init)
    return pl.pallas_call(
        body,
        grid=(bsz, DA_HEADS // hp, nt),
        in_specs=[
            pl.BlockSpec((1, 1, hp * DA_VAL_DIM, tq), lambda b, h, i: (b, i, h, 0)),
            pl.BlockSpec((1, seq, hp * LANE), lambda b, h, i: (b, 0, h)),
            pl.BlockSpec((1, nt, hp * DA_VAL_DIM, tq), lambda b, h, i: (b, 0, h, 0)),
            pl.BlockSpec(da_lambda.shape, lambda b, h, i: (0, 0)),
            pl.BlockSpec((DA_VAL_DIM, 1), lambda b, h, i: (0, 0)),
        ],
        out_specs=pl.BlockSpec((1, tq, hp * LANE), lambda b, h, i: (b, i, h)),
        out_shape=jax.ShapeDtypeStruct((bsz, seq, DA_HEADS * DA_VAL_DIM), BF16),
        scratch_shapes=[
            pltpu.VMEM((1, hp * 2 * tq), F32),
            pltpu.VMEM((DA_VAL_DIM + 2 * SUBLANE, hp * 2 * tq), F32),
            pltpu.VMEM((tq, hp * 2 * tq), BF16),
        ],
        compiler_params=_params("parallel", "parallel", "arbitrary"),
        name="diffattn",
    )(qt, proj, vt, da_lambda, subln_g_col)


def _log_sigmoid(x):
    return jnp.minimum(x, 0.0) - jnp.log(1.0 + jnp.exp(-jnp.abs(x)))


def _sigmoid(x):
    return 1.0 / (1.0 + jnp.exp(-x))


def _mlstm_body(q_ref, k_ref, vt_ref, ot_ref, gcol_ref, grow_ref, cw_ref, cb_ref, gbcol_ref, gbrow_ref, ng_ref,
                y_ref, xbuf, c_scr, m_scr):
    @pl.when(pl.program_id(1) == 0)
    def _():
        xbuf[...] = jnp.zeros(xbuf.shape, xbuf.dtype)
        c_scr[...] = jnp.zeros(c_scr.shape, F32)
        m_scr[...] = jnp.zeros(m_scr.shape, F32)

    for bb in range(q_ref.shape[0]):
        _mlstm_row(bb, q_ref, k_ref, vt_ref, ot_ref, gcol_ref, grow_ref, cw_ref, cb_ref, gbcol_ref, gbrow_ref,
                   ng_ref, y_ref, xbuf, c_scr, m_scr)


def _mlstm_row(bb, q_ref, k_ref, vt_ref, ot_ref, gcol_ref, grow_ref, cw_ref, cb_ref, gbcol_ref, gbrow_ref, ng_ref,
               y_ref, xbuf, c_scr, m_scr):
    L = q_ref.shape[1]
    width = q_ref.shape[2]
    dh = width // ML_HEADS
    tail = xbuf.shape[1]
    aug = c_scr.shape[1]

    u = jnp.concatenate([q_ref[bb], k_ref[bb]], axis=1)
    u_ext = jnp.concatenate([xbuf[bb], u], axis=0)
    src = lax.broadcasted_iota(jnp.int32, (L, L + tail), 1) - lax.broadcasted_iota(jnp.int32, (L, L + tail), 0)
    conv = cb_ref[...] + cw_ref[ML_CONV - 1:ML_CONV, :] * u.astype(F32)
    for back in range(1, ML_CONV):
        shift = jnp.where(src == tail - back, 1.0, 0.0).astype(BF16)
        conv = conv + cw_ref[ML_CONV - 1 - back:ML_CONV - back, :] * _dot(shift, u_ext)
    xbuf[bb] = u[L - tail:, :]
    qk = conv * _sigmoid(conv)

    gc = gcol_ref[bb] + gbcol_ref[...]
    gr = grow_ref[bb] + gbrow_ref[...]
    row_id = lax.broadcasted_iota(jnp.int32, (L, L), 0)
    col_id = lax.broadcasted_iota(jnp.int32, (L, L), 1)
    causal = row_id <= col_id
    tri = jnp.where(col_id <= row_id, 1.0, 0.0).astype(BF16)
    tri_t = jnp.where(causal, 1.0, 0.0).astype(BF16)
    lfc_hi, lfc_lo = _split_bf16(_log_sigmoid(gc))
    lfr_hi, lfr_lo = _split_bf16(_log_sigmoid(gr))
    a_c = _dot(tri, lfc_hi) + _dot(tri, lfc_lo)
    a_r = _dot(lfr_hi, tri_t) + _dot(lfr_lo, tri_t)

    ones_rows = jnp.where(lax.broadcasted_iota(jnp.int32, (aug - dh, L), 0) == 0, 1.0, 0.0).astype(BF16)

    for h in range(ML_HEADS):
        fh = ML_HEADS + h
        st = bb * ML_HEADS + h
        m_prev = m_scr[st:st + 1, 0:1]
        a_row = a_r[fh:fh + 1, :]
        ig_row = gr[h:h + 1, :]
        nb = jnp.where(causal, gc[:, h:h + 1] - a_c[:, fh:fh + 1], NEG_INF)
        m_row = a_row + jnp.maximum(m_prev, jnp.max(nb, axis=0, keepdims=True))
        w_intra = jnp.exp(nb + (a_row - m_row))
        w_inter = jnp.exp(a_row + m_prev - m_row)

        qh = qk[:, h * dh:(h + 1) * dh].astype(BF16)
        kh = (qk[:, width + h * dh:width + (h + 1) * dh] * (dh ** -0.5)).astype(BF16)
        p_t = (_dot_nt(kh, qh) * w_intra).astype(BF16)
        v_aug = jnp.concatenate([vt_ref[bb, 0, h * dh:(h + 1) * dh, :], ones_rows], axis=0)
        c_t = c_scr[st]
        num = w_inter * _dot_nt(c_t.astype(BF16), qh) + _dot(v_aug, p_t)
        scale = 1.0 / jnp.maximum(jnp.abs(num[dh:dh + 1, :]), jnp.exp(-m_row))
        hid = num[0:dh, :] * scale

        g_last = a_row[:, L - 1:L]
        log_w = g_last - a_row + ig_row
        m_new = jnp.maximum(g_last + m_prev, jnp.max(log_w, axis=1, keepdims=True))
        w_state = jnp.exp(log_w - m_new)
        decay = jnp.exp(g_last + m_prev - m_new)
        wv = (w_state * v_aug.astype(F32)).astype(BF16)
        c_scr[st] = decay * c_t + _dot(wv, kh)
        m_scr[st:st + 1, :] = jnp.broadcast_to(m_new, (1, m_scr.shape[1]))

        mu = jnp.mean(hid, axis=0, keepdims=True)
        cen = hid - mu
        var = jnp.mean(cen * cen, axis=0, keepdims=True)
        g_b = ng_ref[h * dh:(h + 1) * dh, :]
        hn = cen * lax.rsqrt(var + EPS) * jnp.concatenate([g_b] * (L // LANE), axis=1)
        gate = _sigmoid(ot_ref[bb, 0, h * dh:(h + 1) * dh, :].astype(F32))
        y_ref[bb, :, h * dh:(h + 1) * dh] = (gate * hn).T.astype(y_ref.dtype)


def _mlstm(proj, mvt, mot, gcol, grow, conv_w, conv_b, gb_col, gb_row, norm_g_b, col0):
    bsz, seq, _ = proj.shape
    width = norm_g_b.shape[0]
    dh = width // ML_HEADS
    L = min(ML_CHUNK, seq)
    tm = mvt.shape[-1]
    per = tm // L
    blk0 = col0 // width
    nb = ML_BATCH_PER_STEP if bsz % ML_BATCH_PER_STEP == 0 else 1
    spec = lambda k: pl.BlockSpec((nb, L, width), lambda b, c: (b, c, blk0 + k))
    tspec = pl.BlockSpec((nb, 1, width, L), lambda b, c: (b, c // per, 0, c % per))
    return pl.pallas_call(
        _mlstm_body,
        grid=(bsz // nb, seq // L),
        in_specs=[
            spec(0), spec(1), tspec, tspec,
            pl.BlockSpec((nb, L, LANE), lambda b, c: (b, c, 0)),
            pl.BlockSpec((nb, SUBLANE, L), lambda b, c: (b, 0, c)),
            pl.BlockSpec(conv_w.shape, lambda b, c: (0, 0)),
            pl.BlockSpec(conv_b.shape, lambda b, c: (0, 0)),
            pl.BlockSpec(gb_col.shape, lambda b, c: (0, 0)),
            pl.BlockSpec(gb_row.shape, lambda b, c: (0, 0)),
            pl.BlockSpec(norm_g_b.shape, lambda b, c: (0, 0)),
        ],
        out_specs=pl.BlockSpec((nb, L, width), lambda b, c: (b, c, 0)),
        out_shape=jax.ShapeDtypeStruct((bsz, seq, width), BF16),
        scratch_shapes=[
            pltpu.VMEM((nb, 2 * SUBLANE, 2 * width), BF16),
            pltpu.VMEM((nb * ML_HEADS, dh + 2 * SUBLANE, dh), F32),
            pltpu.VMEM((nb * ML_HEADS, LANE), F32),
        ],
        compiler_params=_params("parallel", "arbitrary"),
        name="mlstm",
    )(proj, proj, mvt, mot, gcol, grow, conv_w, conv_b, gb_col, gb_row, norm_g_b)


def _memkv_body(mem_ref, g_ref, wk_ref, wv_ref, k_ref, v_ref):
    mb = _rms(mem_ref[0], g_ref[...]).astype(BF16)
    k_ref[0] = _dot(mb, wk_ref[...]).astype(BF16)
    v_ref[0] = _dot(mb, wv_ref[...]).astype(BF16)


def _memkv(mem, g, wk, wv):
    bsz, mlen, d = mem.shape
    full = lambda a: pl.BlockSpec(a.shape, lambda b: (0,) * a.ndim)
    blk = pl.BlockSpec((1, mlen, d), lambda b: (b, 0, 0))
    return pl.pallas_call(
        _memkv_body,
        grid=(bsz,),
        in_specs=[blk, full(g), full(wk), full(wv)],
        out_specs=[blk, blk],
        out_shape=[jax.ShapeDtypeStruct((bsz, mlen, d), BF16)] * 2,
        compiler_params=_params("parallel"),
        name="memkv",
    )(mem, g, wk, wv)


def _mix_body(x_ref, yda_ref, yml_ref, wo1_ref, wo2_ref, gx_ref, wq_ref, km_ref, vm_ref, wxo_ref, gf_ref,
              wrh_ref, wrl_ref, br_ref, x2_ref, route_ref):
    chain = min(ROW_TILE, x_ref.shape[1])
    chains = [slice(c * chain, (c + 1) * chain) for c in range(x_ref.shape[1] // chain)]
    d = x_ref.shape[2]
    dh = d // X_HEADS

    x1 = [x_ref[0, r, :] + _dot(yda_ref[0, r, :], wo1_ref[...]) + _dot(yml_ref[0, r, :], wo2_ref[...])
          for r in chains]
    q = [_dot(_rms(v, gx_ref[...]).astype(BF16), wq_ref[...]).astype(BF16) for v in x1]
    heads = [[] for _ in chains]
    for h in range(X_HEADS):
        sl = slice(h * dh, (h + 1) * dh)
        for c, qc in enumerate(q):
            s = _dot_nt(qc[:, sl], km_ref[0, :, sl]) * (dh ** -0.5)
            p = jnp.exp(s - jnp.max(s, axis=-1, keepdims=True))
            o = _dot(p.astype(BF16), vm_ref[0, :, sl]) / jnp.sum(p, axis=-1, keepdims=True)
            heads[c].append(o.astype(BF16))
    x2 = [v + _dot(jnp.concatenate(hs, axis=1), wxo_ref[...]) for v, hs in zip(x1, heads)]
    for r, v in zip(chains, x2):
        _to_token_tiles(x2_ref.at[0], r.start, v)
    for r, v in zip(chains, x2):
        route_ref[0, :, r] = _route(v, gf_ref, wrh_ref, wrl_ref, br_ref)


def _route(x2, gf_ref, wrh_ref, wrl_ref, br_ref):
    h3 = _rms(x2, gf_ref[...])
    h_hi, h_lo = _split_bf16(h3)
    logits = (_dot_nt(wrh_ref[...], h_hi) + _dot_nt(wrh_ref[...], h_lo) + _dot_nt(wrl_ref[...], h_hi)
              + br_ref[...])
    tm = logits.shape[1]
    gl = logits[0:N_GROUPS, :]
    g_iota = lax.broadcasted_iota(jnp.int32, gl.shape, 0)
    g_max = jnp.max(gl, axis=0, keepdims=True)
    g_idx = jnp.min(jnp.where(gl == g_max, g_iota, N_GROUPS), axis=0, keepdims=True)
    g_w = 1.0 / jnp.sum(jnp.exp(gl - g_max), axis=0, keepdims=True)
    el = jnp.zeros((EXPERTS_PER_GROUP, tm), F32)
    for g in range(N_GROUPS):
        lo = N_GROUPS + g * EXPERTS_PER_GROUP
        el = jnp.where(g_idx == g, logits[lo:lo + EXPERTS_PER_GROUP, :], el)
    e_iota = lax.broadcasted_iota(jnp.int32, el.shape, 0)
    e1 = jnp.max(el, axis=0, keepdims=True)
    i1 = jnp.min(jnp.where(el == e1, e_iota, EXPERTS_PER_GROUP), axis=0, keepdims=True)
    el2 = jnp.where(e_iota == i1, NEG_INF, el)
    e2 = jnp.max(el2, axis=0, keepdims=True)
    i2 = jnp.min(jnp.where(el2 == e2, e_iota, EXPERTS_PER_GROUP), axis=0, keepdims=True)
    p2 = jnp.exp(e2 - e1)
    w1 = g_w / (1.0 + p2)
    w2 = g_w * p2 / (1.0 + p2)
    base = g_idx * EXPERTS_PER_GROUP
    zero = jnp.zeros((1, tm), F32)
    return jnp.concatenate(
        [(base + i1).astype(F32), (base + i2).astype(F32), w1, w2, zero, zero, zero, zero], axis=0)


def _mix(x, yda, yml, wo1, wo2, gx, wq, kmem, vmem, wxo, gf, wr_hi, wr_lo, br):
    bsz, seq, d = x.shape
    tm = min(MIX_CHAINS * ROW_TILE, seq)
    mlen = kmem.shape[1]
    full = lambda a: pl.BlockSpec(a.shape, lambda b, i: (0,) * a.ndim)
    tok = lambda w: pl.BlockSpec((1, tm, w), lambda b, i: (b, i, 0))
    memspec = pl.BlockSpec((1, mlen, d), lambda b, i: (b, 0, 0))
    return pl.pallas_call(
        _mix_body,
        grid=(bsz, seq // tm),
        in_specs=[tok(d), tok(yda.shape[-1]), tok(yml.shape[-1]), full(wo1), full(wo2), full(gx), full(wq),
                  memspec, memspec, full(wxo), full(gf), full(wr_hi), full(wr_lo), full(br)],
        out_specs=[pl.BlockSpec((1, tm * SUBLANE, LANE), lambda b, i: (b, i, 0)),
                   pl.BlockSpec((1, SUBLANE, tm), lambda b, i: (b, 0, i))],
        out_shape=[
            jax.ShapeDtypeStruct((bsz, seq * d // LANE, LANE), F32),
            jax.ShapeDtypeStruct((bsz, SUBLANE, seq), F32),
        ],
        compiler_params=_params("parallel", "parallel"),
        name="mix_xattn_router",
    )(x, yda, yml, wo1, wo2, gx, wq, kmem, vmem, wxo, gf, wr_hi, wr_lo, br)


def _to_token_tiles(ref, first_row, dense):
    n, d = dense.shape
    for c in range(d // LANE):
        ref[pl.ds(first_row * SUBLANE + c, n, stride=SUBLANE), :] = dense[:, c * LANE:(c + 1) * LANE]


def _from_token_tiles(ref, first_row, n):
    return jnp.concatenate(
        [ref[pl.ds(first_row * SUBLANE + c, n, stride=SUBLANE), :] for c in range(SUBLANE)], axis=1)


def _moe_body(start_ref, cnt_ref, x_hbm, tok_hbm, wt_hbm, gf_ref, gfin_ref, w1_hbm, w3_hbm, w2_hbm, out_hbm,
              acc, h3, xg, yt, stage, tok_s, wt_s, w1_buf, w3_buf, w2_buf, sems, load_sems, store_sems, w_sems,
              *, final_norm):
    tile = pl.program_id(0)
    tt = h3.shape[0] // SUBLANE
    chunk_rows = stage.shape[1]
    group = SUBLANE
    depth = w1_buf.shape[0]
    n_steps = pl.num_programs(0) * N_EXPERTS

    def weight_copies(g):
        expert = g % N_EXPERTS
        slot = g % depth
        return [pltpu.make_async_copy(src.at[expert], dst.at[slot], w_sems.at[slot * 3 + k])
                for k, (src, dst) in enumerate(((w1_hbm, w1_buf), (w3_hbm, w3_buf), (w2_hbm, w2_buf)))]

    @pl.when(tile == 0)
    def _():
        for ahead in range(depth - 1):
            for cp in weight_copies(ahead):
                cp.start()

    def row_tile(ref, row0):
        return ref.at[pl.ds(pl.multiple_of(row0, SUBLANE), SUBLANE), :]

    def token_tile(ref, r):
        return ref.at[pl.ds(r * SUBLANE, SUBLANE), :]

    spare_row = tt * SUBLANE

    n_chunks = tt // chunk_rows
    chunk_tiles = chunk_rows * SUBLANE

    def load_chunk(i):
        rows = pl.ds(pl.multiple_of(i * chunk_tiles, chunk_tiles), chunk_tiles)
        return pltpu.make_async_copy(x_hbm.at[tile, rows, :], acc.at[rows, :], load_sems.at[i])

    lists = [pltpu.make_async_copy(tok_hbm.at[tile], tok_s, sems.at[0]),
             pltpu.make_async_copy(wt_hbm.at[tile], wt_s, sems.at[1])]
    for cp in lists:
        cp.start()
    for i in range(n_chunks):
        load_chunk(i).start()
    token_tile(acc, tt)[...] = jnp.zeros((SUBLANE, LANE), F32)

    def ffn_norm(i, c):
        load_chunk(i).wait()
        r0 = i * chunk_rows
        _to_token_tiles(h3, r0, _rms(_from_token_tiles(acc, r0, chunk_rows), gf_ref[...]))
        return c
    lax.fori_loop(0, n_chunks, ffn_norm, 0)
    for cp in lists:
        cp.wait()

    big = xg.shape[0] // SUBLANE
    small = MOE_ROWS

    def expert_step(e, carry):
        step_id = tile * N_EXPERTS + e

        @pl.when(step_id + depth - 1 < n_steps)
        def _():
            for cp in weight_copies(step_id + depth - 1):
                cp.start()

        for cp in weight_copies(step_id):
            cp.wait()
        w_slot = step_id % depth
        start = start_ref[step_id]
        count = cnt_ref[step_id]

        def run_block(first, cnt, rows):
            idx0 = start + first
            for r in range(rows):
                token_tile(xg, r)[...] = row_tile(h3, tok_s[idx0 + r])[...]

            xb = _from_token_tiles(xg, 0, rows).astype(BF16)
            a = _dot(xb, w1_buf[w_slot])
            b = _dot(xb, w3_buf[w_slot])
            hmid = (a * _sigmoid(a) * b).astype(BF16)
            _to_token_tiles(yt, 0, _dot(hmid, w2_buf[w_slot]))

            for g0 in range(0, rows, group):
                dst = [jnp.where(g0 + u < cnt, tok_s[idx0 + g0 + u], spare_row) for u in range(group)]
                new = [row_tile(acc, dst[u])[...] + wt_s[idx0 + g0 + u] * token_tile(yt, g0 + u)[...]
                       for u in range(group)]
                for u in range(group):
                    row_tile(acc, dst[u])[...] = new[u]

        @pl.when(count > 0)
        def _():
            run_block(0, jnp.minimum(big, count), big)

        def small_block(i, c):
            first = big + i * small
            run_block(first, jnp.minimum(small, count - first), small)
            return c

        lax.fori_loop(0, (jnp.maximum(count - big, 0) + small - 1) // small, small_block, 0)
        return carry

    lax.fori_loop(0, N_EXPERTS, expert_step, 0)

    n_slots = stage.shape[0]

    def store_chunk(i, slot):
        r0 = pl.multiple_of(i * chunk_rows, chunk_rows)
        return pltpu.make_async_copy(stage.at[slot], out_hbm.at[tile, pl.ds(r0, chunk_rows), :],
                                     store_sems.at[slot])

    def write_back(p, c):
        for slot in range(n_slots):
            i = p * n_slots + slot

            @pl.when(p > 0)
            def _():
                store_chunk(i - n_slots, slot).wait()

            rows = _from_token_tiles(acc, i * chunk_rows, chunk_rows)
            stage[slot] = _rms(rows, gfin_ref[...]) if final_norm else rows
            store_chunk(i, slot).start()
        return c
    lax.fori_loop(0, n_chunks // n_slots, write_back, 0)
    for slot in range(n_slots):
        store_chunk(n_chunks - n_slots + slot, slot).wait()


def _moe(start, cnt, tok, wt, x_token_tiles, gf, gfin, w1, w3, w2, final_norm):
    ntiles, tile_rows, _ = x_token_tiles.shape
    tt = tile_rows // SUBLANE
    d = SUBLANE * LANE
    de = w1.shape[-1]
    seg = tok.shape[1]
    body = functools.partial(_moe_body, final_norm=final_norm)
    hbm = pl.BlockSpec(memory_space=pl.ANY)
    depth = MOE_WEIGHT_SLOTS
    grid_spec = pltpu.PrefetchScalarGridSpec(
        num_scalar_prefetch=2,
        grid=(ntiles,),
        in_specs=[
            hbm, hbm, hbm,
            pl.BlockSpec((1, d), lambda t, *_: (0, 0)),
            pl.BlockSpec((1, d), lambda t, *_: (0, 0)),
            hbm, hbm, hbm,
        ],
        out_specs=hbm,
        scratch_shapes=[
            pltpu.VMEM((tile_rows + SUBLANE, LANE), F32),
            pltpu.VMEM((tile_rows, LANE), F32),
            pltpu.VMEM((MOE_FIRST_ROWS * SUBLANE, LANE), F32),
            pltpu.VMEM((MOE_FIRST_ROWS * SUBLANE, LANE), F32),
            pltpu.VMEM((2, 2 * MOE_ROWS, d), F32),
            pltpu.SMEM((seg,), jnp.int32),
            pltpu.SMEM((seg,), F32),
            pltpu.VMEM((depth, d, de), BF16),
            pltpu.VMEM((depth, d, de), BF16),
            pltpu.VMEM((depth, de, d), BF16),
            pltpu.SemaphoreType.DMA((2,)),
            pltpu.SemaphoreType.DMA((tt // (2 * MOE_ROWS),)),
            pltpu.SemaphoreType.DMA((2,)),
            pltpu.SemaphoreType.DMA((3 * depth,)),
        ],
    )
    return pl.pallas_call(
        body,
        grid_spec=grid_spec,
        out_shape=jax.ShapeDtypeStruct((ntiles, tt, d), F32),
        compiler_params=_params("arbitrary"),
        name="moe",
    )(start, cnt, x_token_tiles, tok, wt, gf, gfin, w1, w3, w2)


def _rope_tables(seq):
    half = DA_HEAD_DIM // 2
    inv_freq = ROPE_THETA ** (-jnp.arange(0, DA_HEAD_DIM, 2, dtype=F32) / DA_HEAD_DIM)
    ang = jnp.arange(seq, dtype=F32)[:, None] * inv_freq[None, :]
    cos, sin = jnp.cos(ang), jnp.sin(ang)
    reps = LANE // half
    cos_t = jnp.tile(cos, (1, reps))
    sin_t = jnp.tile(jnp.concatenate([-sin, sin], axis=1), (1, reps // 2))
    return cos_t, sin_t


def _sort_assignments(route, tt):
    bsz, _, seq = route.shape
    per = seq // tt
    ntiles = bsz * per
    tile = lambda a: a.reshape(bsz, TOP_K_ROWS, per, tt).transpose(0, 2, 1, 3).reshape(ntiles, TOP_K_ROWS * tt)
    keys = tile(route[:, 0:TOP_K_ROWS, :]).astype(jnp.int32)
    wts = tile(route[:, TOP_K_ROWS:2 * TOP_K_ROWS, :])
    order = jnp.argsort(keys, axis=-1).astype(jnp.int32)
    tok = (order % tt) * SUBLANE
    wt = jnp.take_along_axis(wts, order, axis=-1)
    counts = jnp.sum(keys[:, :, None] == jnp.arange(N_EXPERTS, dtype=jnp.int32)[None, None, :], axis=1,
                     dtype=jnp.int32)
    start = jnp.cumsum(counts, axis=-1, dtype=jnp.int32) - counts
    over = ((0, 0), (0, MOE_FIRST_ROWS))
    return start.reshape(-1), counts.reshape(-1), jnp.pad(tok, over), jnp.pad(wt, over)


TOP_K_ROWS = 2


def kernel(x, mem, norm_mix_g, w_in, w_out, da_lambda, da_subln_g, ml_conv_w, ml_conv_b, ml_gate_b, ml_norm_g, norm_x_g, norm_mem_g, w_xq, w_xk, w_xv, w_xo, norm_ffn_g, w_router_group, b_router_group, w_router_expert, b_router_expert, w1, w3, w2, norm_final_g):
    depth = w_in.shape[0]
    bsz, seq, d = x.shape
    da_qk = DA_HEADS * 2 * DA_HEAD_DIM
    da_width = DA_HEADS * DA_VAL_DIM
    ml_width = ml_norm_g.shape[-1]
    n_main = 2 * da_qk + da_width + 4 * ml_width
    n_in = w_in.shape[-1]
    cos_t, sin_t = _rope_tables(seq)
    row = lambda v: v.reshape(1, -1).astype(F32)

    for l in range(depth):
        lam_init = 0.8 - 0.6 * math.exp(-0.3 * l)
        w_pad = jnp.pad(w_in[l], ((0, 0), (0, n_main + LANE - n_in))).astype(BF16)
        qt, vt, mvt, mot, proj, gcol, grow = _inproj(x, row(norm_mix_g[l]), w_pad, cos_t, sin_t)

        yda = _diffattn(qt, proj, vt, da_lambda[l].astype(F32), da_subln_g[l].astype(F32).reshape(-1, 1), lam_init)

        gb = ml_gate_b[l].astype(F32).reshape(1, 2 * ML_HEADS)
        gb_col = jnp.pad(gb, ((0, 0), (0, LANE - 2 * ML_HEADS)))
        gb_row = gb.reshape(2 * ML_HEADS, 1)
        ng_b = jnp.broadcast_to(ml_norm_g[l].astype(F32)[:, None], (ml_width, LANE))
        yml = _mlstm(proj, mvt, mot, gcol, grow, ml_conv_w[l].reshape(ML_CONV, 2 * ml_width).astype(F32),
                     row(ml_conv_b[l]), gb_col, gb_row, ng_b, da_qk)

        kmem, vmem = _memkv(mem, row(norm_mem_g[l]), w_xk[l].astype(BF16), w_xv[l].astype(BF16))

        wr = jnp.concatenate([w_router_group[l], w_router_expert[l]], axis=1).astype(F32).T
        wr = jnp.pad(wr, ((0, ROUTER_ROWS - wr.shape[0]), (0, 0)))
        wr_hi = wr.astype(BF16)
        wr_lo = (wr - wr_hi.astype(F32)).astype(BF16)
        br = jnp.pad(jnp.concatenate([b_router_group[l], b_router_expert[l]]).astype(F32),
                     (0, ROUTER_ROWS - N_GROUPS - N_EXPERTS)).reshape(ROUTER_ROWS, 1)
        wo = w_out[l].astype(BF16)
        x2, route = _mix(x, yda, yml, wo[:da_width], wo[da_width:], row(norm_x_g[l]), w_xq[l].astype(BF16),
                         kmem, vmem, w_xo[l].astype(BF16), row(norm_ffn_g[l]), wr_hi, wr_lo, br)

        tt = min(MOE_TILE, seq)
        x = _moe(*_sort_assignments(route, tt), x2.reshape(bsz * seq // tt, tt * d // LANE, LANE),
                 row(norm_ffn_g[l]), row(norm_final_g),
                 w1[l].astype(BF16), w3[l].astype(BF16), w2[l].astype(BF16),
                 final_norm=(l + 1 == depth)).reshape(bsz, seq, d)
    return x
```

```python
import functools
import math

import jax
import jax.numpy as jnp
from jax import lax
from jax.experimental import pallas as pl
from jax.experimental.pallas import tpu as pltpu

F32 = jnp.float32
BF16 = jnp.bfloat16

LANE = 128
SUBLANE = 8
VMEM_LIMIT_BYTES = 56 * 1024 * 1024

EPS = 1e-6
ROPE_THETA = 10000.0
DA_HEADS = 4
DA_HEAD_DIM = 64
DA_VAL_DIM = 2 * DA_HEAD_DIM
ML_HEADS = 4
ML_CONV = 4
X_HEADS = 4
N_GROUPS = 4
EXPERTS_PER_GROUP = 8
N_EXPERTS = N_GROUPS * EXPERTS_PER_GROUP
ROUTER_ROWS = 40

ROW_TILE = 512
MIX_CHAINS = 2
DA_HEADS_PER_STEP = 4
ML_CHUNK = 256
ML_BATCH_PER_STEP = 4
MOE_TILE = 4096
MOE_ROWS = 128
MOE_FIRST_ROWS = 288
MOE_WEIGHT_SLOTS = 3

NEG_INF = float("-inf")
LOG2_E = math.log2(math.e)


def _rms(x, g):
    return x * lax.rsqrt(jnp.mean(x * x, axis=-1, keepdims=True) + EPS) * g


def _dot(a, b):
    return jnp.dot(a, b, preferred_element_type=F32)


def _dot_nt(a, b):
    return lax.dot_general(a, b, (((1,), (1,)), ((), ())), preferred_element_type=F32)


def _dot_tn(a, b):
    return lax.dot_general(a, b, (((0,), (0,)), ((), ())), preferred_element_type=F32)


def _split_bf16(x):
    hi = x.astype(BF16)
    lo = (x - hi.astype(F32)).astype(BF16)
    return hi, lo


def _params(*semantics):
    return pltpu.CompilerParams(dimension_semantics=semantics, vmem_limit_bytes=VMEM_LIMIT_BYTES)


def _inproj_body(x_ref, g_ref, w_ref, cos_ref, sin_ref, qt_ref, vt_ref, mvt_ref, mot_ref, proj_ref, gcol_ref,
                 grow_ref):
    hb = _rms(x_ref[0], g_ref[...]).astype(BF16)
    cos = cos_ref[...]
    sin = sin_ref[...]
    lane = lax.broadcasted_iota(jnp.int32, cos.shape, 1)
    first_half = (lane % DA_HEAD_DIM) < (DA_HEAD_DIM // 2)
    n_main = w_ref.shape[1] - LANE
    wide = 4 * LANE
    for c in range(n_main // LANE):
        if c % (wide // LANE) == 0:
            acc = _dot(hb, w_ref[:, c * LANE:c * LANE + wide])
        t = acc[:, (c * LANE) % wide:(c * LANE) % wide + LANE]
        if c < 2 * DA_HEADS:
            rot = jnp.where(first_half, pltpu.roll(t, LANE - DA_HEAD_DIM // 2, 1),
                            pltpu.roll(t, DA_HEAD_DIM // 2, 1))
            t = t * cos + rot * sin
        if c < DA_HEADS:
            qt_ref[0, 0, c * LANE:(c + 1) * LANE, :] = (t * (DA_HEAD_DIM ** -0.5 * LOG2_E)).T.astype(BF16)
        elif c < 2 * DA_HEADS:
            proj_ref[0, :, (c - DA_HEADS) * LANE:(c - DA_HEADS + 1) * LANE] = t.astype(BF16)
        elif c < 3 * DA_HEADS:
            vt_ref[0, 0, (c - 2 * DA_HEADS) * LANE:(c - 2 * DA_HEADS + 1) * LANE, :] = t.T.astype(BF16)
        elif c < 3 * DA_HEADS + 2 * ML_HEADS:
            proj_ref[0, :, (c - 2 * DA_HEADS) * LANE:(c - 2 * DA_HEADS + 1) * LANE] = t.astype(BF16)
        elif c < 3 * DA_HEADS + 3 * ML_HEADS:
            m = c - 3 * DA_HEADS - 2 * ML_HEADS
            mvt_ref[0, 0, m * LANE:(m + 1) * LANE, :] = t.T.astype(BF16)
        else:
            m = c - 3 * DA_HEADS - 3 * ML_HEADS
            mot_ref[0, 0, m * LANE:(m + 1) * LANE, :] = t.T.astype(BF16)
    gates = _dot(hb, w_ref[:, n_main:n_main + LANE])
    gcol_ref[0] = gates
    grow_ref[0] = gates.T[:SUBLANE, :]


def _inproj(x, g, w_pad, cos_t, sin_t):
    bsz, seq, d = x.shape
    tm = min(ROW_TILE, seq)
    nt = seq // tm
    da_w = DA_HEADS * DA_VAL_DIM
    ml_w = ML_HEADS * LANE
    n_proj = w_pad.shape[1] - LANE - 2 * da_w - 2 * ml_w
    return pl.pallas_call(
        _inproj_body,
        grid=(bsz, nt),
        in_specs=[
            pl.BlockSpec((1, tm, d), lambda b, i: (b, i, 0)),
            pl.BlockSpec((1, d), lambda b, i: (0, 0)),
            pl.BlockSpec(w_pad.shape, lambda b, i: (0, 0)),
            pl.BlockSpec((tm, LANE), lambda b, i: (i, 0)),
            pl.BlockSpec((tm, LANE), lambda b, i: (i, 0)),
        ],
        out_specs=[
            pl.BlockSpec((1, 1, da_w, tm), lambda b, i: (b, i, 0, 0)),
            pl.BlockSpec((1, 1, da_w, tm), lambda b, i: (b, i, 0, 0)),
            pl.BlockSpec((1, 1, ml_w, tm), lambda b, i: (b, i, 0, 0)),
            pl.BlockSpec((1, 1, ml_w, tm), lambda b, i: (b, i, 0, 0)),
            pl.BlockSpec((1, tm, n_proj), lambda b, i: (b, i, 0)),
            pl.BlockSpec((1, tm, LANE), lambda b, i: (b, i, 0)),
            pl.BlockSpec((1, SUBLANE, tm), lambda b, i: (b, 0, i)),
        ],
        out_shape=[
            jax.ShapeDtypeStruct((bsz, nt, da_w, tm), BF16),
            jax.ShapeDtypeStruct((bsz, nt, da_w, tm), BF16),
            jax.ShapeDtypeStruct((bsz, nt, ml_w, tm), BF16),
            jax.ShapeDtypeStruct((bsz, nt, ml_w, tm), BF16),
            jax.ShapeDtypeStruct((bsz, seq, n_proj), BF16),
            jax.ShapeDtypeStruct((bsz, seq, LANE), F32),
            jax.ShapeDtypeStruct((bsz, SUBLANE, seq), F32),
        ],
        compiler_params=_params("parallel", "parallel"),
        name="inproj",
    )(x, g, w_pad, cos_t, sin_t)


def _diffattn_body(qt_ref, k_ref, vt_ref, lam_ref, g_ref, o_ref, m_scr, acc_scr, p_scr, *, lam_init):
    qi = pl.program_id(2)
    tq = qt_ref.shape[3]
    heads = qt_ref.shape[2] // DA_VAL_DIM
    q2t = []
    for g in range(heads):
        qt = qt_ref[0, 0, g * DA_VAL_DIM:(g + 1) * DA_VAL_DIM, :]
        row = lax.broadcasted_iota(jnp.int32, qt.shape, 0)
        zero = jnp.zeros_like(qt)
        q2t.append(jnp.concatenate([jnp.where(row < DA_HEAD_DIM, qt, zero),
                                    jnp.where(row >= DA_HEAD_DIM, qt, zero)], axis=1))
    pad_rows = acc_scr.shape[0] - DA_VAL_DIM
    ones_rows = jnp.where(lax.broadcasted_iota(jnp.int32, (pad_rows, tq), 0) == 0, 1.0, 0.0).astype(BF16)

    m_scr[...] = jnp.full(m_scr.shape, NEG_INF, F32)
    acc_scr[...] = jnp.zeros(acc_scr.shape, F32)

    n_chains = 4
    width = 2 * tq // n_chains
    chains = tuple((g, slice(c * width, (c + 1) * width),
                    slice(g * 2 * tq + c * width, g * 2 * tq + (c + 1) * width))
                   for c in range(n_chains) for g in range(heads))

    def scores(j, masked, prev_values):
        out = []
        for g, local, cols in chains:
            k = k_ref[0, pl.ds(pl.multiple_of(j * tq, tq), tq), g * LANE:(g + 1) * LANE]
            s = _dot(k, q2t[g][:, local])
            pv = values(j - 1, g, cols) if prev_values else None
            if masked:
                key = lax.broadcasted_iota(jnp.int32, s.shape, 0)
                qry = lax.broadcasted_iota(jnp.int32, s.shape, 1) + local.start % tq
                s = jnp.where(key <= qry, s, NEG_INF)
            m_prev = m_scr[:, cols]
            m_new = jnp.maximum(m_prev, jnp.max(s, axis=0, keepdims=True))
            p_scr[:, cols] = jnp.exp2(s - m_new).astype(BF16)
            m_scr[:, cols] = m_new
            out.append((jnp.exp2(m_prev - m_new), pv))
        return out

    def values(j, g, cols):
        v_aug = jnp.concatenate([vt_ref[0, j, g * DA_VAL_DIM:(g + 1) * DA_VAL_DIM, :], ones_rows], axis=0)
        return _dot(v_aug, p_scr[:, cols])

    def step(j, masked):
        for (_, _, cols), (alpha, pv) in zip(chains, scores(j, masked, True)):
            acc_scr[:, cols] = alpha * (acc_scr[:, cols] + pv)

    def full_step_pair(i, carry):
        step(1 + 2 * i, False)
        step(2 + 2 * i, False)
        return carry

    @pl.when(qi == 0)
    def _():
        scores(0, True, False)

    @pl.when(qi > 0)
    def _():
        scores(0, False, False)
        lax.fori_loop(0, (qi - 1) // 2, full_step_pair, 0)

        @pl.when((qi - 1) % 2 == 1)
        def _():
            step(qi - 1, False)

        step(qi, True)

    for g, _, cols in chains:
        acc_scr[:, cols] = acc_scr[:, cols] + values(qi, g, cols)

    lv = lam_ref[...]
    lam = (jnp.exp(jnp.sum(lv[0:1, :] * lv[1:2, :], axis=-1, keepdims=True))
           - jnp.exp(jnp.sum(lv[2:3, :] * lv[3:4, :], axis=-1, keepdims=True)) + lam_init)
    for g in range(heads):
        acc = acc_scr[:, g * 2 * tq:(g + 1) * 2 * tq]
        o = acc[0:DA_VAL_DIM, :] / acc[DA_VAL_DIM:DA_VAL_DIM + 1, :]
        o = o[:, :tq] - lam * o[:, tq:]
        y = o * lax.rsqrt(jnp.mean(o * o, axis=0, keepdims=True) + EPS) * g_ref[...] * (1.0 - lam_init)
        o_ref[0, :, g * LANE:(g + 1) * LANE] = y.T.astype(o_ref.dtype)


def _diffattn(qt, proj, vt, da_lambda, subln_g_col, lam_init):
    bsz, nt, _, tq = qt.shape
    seq = nt * tq
    hp = DA_HEADS_PER_STEP
    body = functools.partial(_diffattn_body, lam_init=lam_init)
    return pl.pallas_call(
        body,
        grid=(bsz, DA_HEADS // hp, nt),
        in_specs=[
            pl.BlockSpec((1, 1, hp * DA_VAL_DIM, tq), lambda b, h, i: (b, i, h, 0)),
            pl.BlockSpec((1, seq, hp * LANE), lambda b, h, i: (b, 0, h)),
            pl.BlockSpec((1, nt, hp * DA_VAL_DIM, tq), lambda b, h, i: (b, 0, h, 0)),
            pl.BlockSpec(da_lambda.shape, lambda b, h, i: (0, 0)),
            pl.BlockSpec((DA_VAL_DIM, 1), lambda b, h, i: (0, 0)),
        ],
        out_specs=pl.BlockSpec((1, tq, hp * LANE), lambda b, h, i: (b, i, h)),
        out_shape=jax.ShapeDtypeStruct((bsz, seq, DA_HEADS * DA_VAL_DIM), BF16),
        scratch_shapes=[
            pltpu.VMEM((1, hp * 2 * tq), F32),
            pltpu.VMEM((DA_VAL_DIM + 2 * SUBLANE, hp * 2 * tq), F32),
            pltpu.VMEM((tq, hp * 2 * tq), BF16),
        ],
        compiler_params=_params("parallel", "parallel", "arbitrary"),
        name="diffattn",
    )(qt, proj, vt, da_lambda, subln_g_col)


def _log_sigmoid(x):
    return jnp.minimum(x, 0.0) - jnp.log(1.0 + jnp.exp(-jnp.abs(x)))


def _sigmoid(x):
    return 1.0 / (1.0 + jnp.exp(-x))


def _mlstm_body(q_ref, k_ref, vt_ref, ot_ref, gcol_ref, grow_ref, cw_ref, cb_ref, gbcol_ref, gbrow_ref, ng_ref,
                y_ref, xbuf, c_scr, m_scr):
    @pl.when(pl.program_id(1) == 0)
    def _():
        xbuf[...] = jnp.zeros(xbuf.shape, xbuf.dtype)
        c_scr[...] = jnp.zeros(c_scr.shape, F32)
        m_scr[...] = jnp.zeros(m_scr.shape, F32)

    for bb in range(q_ref.shape[0]):
        _mlstm_row(bb, q_ref, k_ref, vt_ref, ot_ref, gcol_ref, grow_ref, cw_ref, cb_ref, gbcol_ref, gbrow_ref,
                   ng_ref, y_ref, xbuf, c_scr, m_scr)


def _mlstm_row(bb, q_ref, k_ref, vt_ref, ot_ref, gcol_ref, grow_ref, cw_ref, cb_ref, gbcol_ref, gbrow_ref, ng_ref,
               y_ref, xbuf, c_scr, m_scr):
    L = q_ref.shape[1]
    width = q_ref.shape[2]
    dh = width // ML_HEADS
    tail = xbuf.shape[1]
    aug = c_scr.shape[1]

    u = jnp.concatenate([q_ref[bb], k_ref[bb]], axis=1)
    u_ext = jnp.concatenate([xbuf[bb], u], axis=0)
    src = lax.broadcasted_iota(jnp.int32, (L, L + tail), 1) - lax.broadcasted_iota(jnp.int32, (L, L + tail), 0)
    conv = cb_ref[...] + cw_ref[ML_CONV - 1:ML_CONV, :] * u.astype(F32)
    for back in range(1, ML_CONV):
        shift = jnp.where(src == tail - back, 1.0, 0.0).astype(BF16)
        conv = conv + cw_ref[ML_CONV - 1 - back:ML_CONV - back, :] * _dot(shift, u_ext)
    xbuf[bb] = u[L - tail:, :]
    qk = conv * _sigmoid(conv)

    gc = gcol_ref[bb] + gbcol_ref[...]
    gr = grow_ref[bb] + gbrow_ref[...]
    row_id = lax.broadcasted_iota(jnp.int32, (L, L), 0)
    col_id = lax.broadcasted_iota(jnp.int32, (L, L), 1)
    causal = row_id <= col_id
    tri = jnp.where(col_id <= row_id, 1.0, 0.0).astype(BF16)
    tri_t = jnp.where(causal, 1.0, 0.0).astype(BF16)
    lfc_hi, lfc_lo = _split_bf16(_log_sigmoid(gc))
    lfr_hi, lfr_lo = _split_bf16(_log_sigmoid(gr))
    a_c = _dot(tri, lfc_hi) + _dot(tri, lfc_lo)
    a_r = _dot(lfr_hi, tri_t) + _dot(lfr_lo, tri_t)

    ones_rows = jnp.where(lax.broadcasted_iota(jnp.int32, (aug - dh, L), 0) == 0, 1.0, 0.0).astype(BF16)

    for h in range(ML_HEADS):
        fh = ML_HEADS + h
        st = bb * ML_HEADS + h
        m_prev = m_scr[st:st + 1, 0:1]
        a_row = a_r[fh:fh + 1, :]
        ig_row = gr[h:h + 1, :]
        nb = jnp.where(causal, gc[:, h:h + 1] - a_c[:, fh:fh + 1], NEG_INF)
        m_row = a_row + jnp.maximum(m_prev, jnp.max(nb, axis=0, keepdims=True))
        w_intra = jnp.exp(nb + (a_row - m_row))
        w_inter = jnp.exp(a_row + m_prev - m_row)

        qh = qk[:, h * dh:(h + 1) * dh].astype(BF16)
        kh = (qk[:, width + h * dh:width + (h + 1) * dh] * (dh ** -0.5)).astype(BF16)
        p_t = (_dot_nt(kh, qh) * w_intra).astype(BF16)
        v_aug = jnp.concatenate([vt_ref[bb, 0, h * dh:(h + 1) * dh, :], ones_rows], axis=0)
        c_t = c_scr[st]
        num = w_inter * _dot_nt(c_t.astype(BF16), qh) + _dot(v_aug, p_t)
        scale = 1.0 / jnp.maximum(jnp.abs(num[dh:dh + 1, :]), jnp.exp(-m_row))
        hid = num[0:dh, :] * scale

        g_last = a_row[:, L - 1:L]
        log_w = g_last - a_row + ig_row
        m_new = jnp.maximum(g_last + m_prev, jnp.max(log_w, axis=1, keepdims=True))
        w_state = jnp.exp(log_w - m_new)
        decay = jnp.exp(g_last + m_prev - m_new)
        wv = (w_state * v_aug.astype(F32)).astype(BF16)
        c_scr[st] = decay * c_t + _dot(wv, kh)
        m_scr[st:st + 1, :] = jnp.broadcast_to(m_new, (1, m_scr.shape[1]))

        mu = jnp.mean(hid, axis=0, keepdims=True)
        cen = hid - mu
        var = jnp.mean(cen * cen, axis=0, keepdims=True)
        g_b = ng_ref[h * dh:(h + 1) * dh, :]
        hn = cen * lax.rsqrt(var + EPS) * jnp.concatenate([g_b] * (L // LANE), axis=1)
        gate = _sigmoid(ot_ref[bb, 0, h * dh:(h + 1) * dh, :].astype(F32))
        y_ref[bb, :, h * dh:(h + 1) * dh] = (gate * hn).T.astype(y_ref.dtype)


def _mlstm(proj, mvt, mot, gcol, grow, conv_w, conv_b, gb_col, gb_row, norm_g_b, col0):
    bsz, seq, _ = proj.shape
    width = norm_g_b.shape[0]
    dh = width // ML_HEADS
    L = min(ML_CHUNK, seq)
    tm = mvt.shape[-1]
    per = tm // L
    blk0 = col0 // width
    nb = ML_BATCH_PER_STEP if bsz % ML_BATCH_PER_STEP == 0 else 1
    spec = lambda k: pl.BlockSpec((nb, L, width), lambda b, c: (b, c, blk0 + k))
    tspec = pl.BlockSpec((nb, 1, width, L), lambda b, c: (b, c // per, 0, c % per))
    return pl.pallas_call(
        _mlstm_body,
        grid=(bsz // nb, seq // L),
        in_specs=[
            spec(0), spec(1), tspec, tspec,
            pl.BlockSpec((nb, L, LANE), lambda b, c: (b, c, 0)),
            pl.BlockSpec((nb, SUBLANE, L), lambda b, c: (b, 0, c)),
            pl.BlockSpec(conv_w.shape, lambda b, c: (0, 0)),
            pl.BlockSpec(conv_b.shape, lambda b, c: (0, 0)),
            pl.BlockSpec(gb_col.shape, lambda b, c: (0, 0)),
            pl.BlockSpec(gb_row.shape, lambda b, c: (0, 0)),
            pl.BlockSpec(norm_g_b.shape, lambda b, c: (0, 0)),
        ],
        out_specs=pl.BlockSpec((nb, L, width), lambda b, c: (b, c, 0)),
        out_shape=jax.ShapeDtypeStruct((bsz, seq, width), BF16),
        scratch_shapes=[
            pltpu.VMEM((nb, 2 * SUBLANE, 2 * width), BF16),
            pltpu.VMEM((nb * ML_HEADS, dh + 2 * SUBLANE, dh), F32),
            pltpu.VMEM((nb * ML_HEADS, LANE), F32),
        ],
        compiler_params=_params("parallel", "arbitrary"),
        name="mlstm",
    )(proj, proj, mvt, mot, gcol, grow, conv_w, conv_b, gb_col, gb_row, norm_g_b)


def _memkv_body(mem_ref, g_ref, wk_ref, wv_ref, k_ref, v_ref):
    mb = _rms(mem_ref[0], g_ref[...]).astype(BF16)
    k_ref[0] = _dot(mb, wk_ref[...]).astype(BF16)
    v_ref[0] = _dot(mb, wv_ref[...]).astype(BF16)


def _memkv(mem, g, wk, wv):
    bsz, mlen, d = mem.shape
    full = lambda a: pl.BlockSpec(a.shape, lambda b: (0,) * a.ndim)
    blk = pl.BlockSpec((1, mlen, d), lambda b: (b, 0, 0))
    return pl.pallas_call(
        _memkv_body,
        grid=(bsz,),
        in_specs=[blk, full(g), full(wk), full(wv)],
        out_specs=[blk, blk],
        out_shape=[jax.ShapeDtypeStruct((bsz, mlen, d), BF16)] * 2,
        compiler_params=_params("parallel"),
        name="memkv",
    )(mem, g, wk, wv)


def _mix_body(x_ref, yda_ref, yml_ref, wo1_ref, wo2_ref, gx_ref, wq_ref, km_ref, vm_ref, wxo_ref, gf_ref,
              wrh_ref, wrl_ref, br_ref, x2_ref, route_ref):
    chain = min(ROW_TILE, x_ref.shape[1])
    chains = [slice(c * chain, (c + 1) * chain) for c in range(x_ref.shape[1] // chain)]
    d = x_ref.shape[2]
    dh = d // X_HEADS

    x1 = [x_ref[0, r, :] + _dot(yda_ref[0, r, :], wo1_ref[...]) + _dot(yml_ref[0, r, :], wo2_ref[...])
          for r in chains]
    q = [_dot(_rms(v, gx_ref[...]).astype(BF16), wq_ref[...]).astype(BF16) for v in x1]
    heads = [[] for _ in chains]
    for h in range(X_HEADS):
        sl = slice(h * dh, (h + 1) * dh)
        for c, qc in enumerate(q):
            s = _dot_nt(qc[:, sl], km_ref[0, :, sl]) * (dh ** -0.5)
            p = jnp.exp(s - jnp.max(s, axis=-1, keepdims=True))
            o = _dot(p.astype(BF16), vm_ref[0, :, sl]) / jnp.sum(p, axis=-1, keepdims=True)
            heads[c].append(o.astype(BF16))
    x2 = [v + _dot(jnp.concatenate(hs, axis=1), wxo_ref[...]) for v, hs in zip(x1, heads)]
    for r, v in zip(chains, x2):
        _to_token_tiles(x2_ref.at[0], r.start, v)
    for r, v in zip(chains, x2):
        route_ref[0, :, r] = _route(v, gf_ref, wrh_ref, wrl_ref, br_ref)


def _route(x2, gf_ref, wrh_ref, wrl_ref, br_ref):
    h3 = _rms(x2, gf_ref[...])
    h_hi, h_lo = _split_bf16(h3)
    logits = (_dot_nt(wrh_ref[...], h_hi) + _dot_nt(wrh_ref[...], h_lo) + _dot_nt(wrl_ref[...], h_hi)
              + br_ref[...])
    tm = logits.shape[1]
    gl = logits[0:N_GROUPS, :]
    g_iota = lax.broadcasted_iota(jnp.int32, gl.shape, 0)
    g_max = jnp.max(gl, axis=0, keepdims=True)
    g_idx = jnp.min(jnp.where(gl == g_max, g_iota, N_GROUPS), axis=0, keepdims=True)
    g_w = 1.0 / jnp.sum(jnp.exp(gl - g_max), axis=0, keepdims=True)
    el = jnp.zeros((EXPERTS_PER_GROUP, tm), F32)
    for g in range(N_GROUPS):
        lo = N_GROUPS + g * EXPERTS_PER_GROUP
        el = jnp.where(g_idx == g, logits[lo:lo + EXPERTS_PER_GROUP, :], el)
    e_iota = lax.broadcasted_iota(jnp.int32, el.shape, 0)
    e1 = jnp.max(el, axis=0, keepdims=True)
    i1 = jnp.min(jnp.where(el == e1, e_iota, EXPERTS_PER_GROUP), axis=0, keepdims=True)
    el2 = jnp.where(e_iota == i1, NEG_INF, el)
    e2 = jnp.max(el2, axis=0, keepdims=True)
    i2 = jnp.min(jnp.where(el2 == e2, e_iota, EXPERTS_PER_GROUP), axis=0, keepdims=True)
    p2 = jnp.exp(e2 - e1)
    w1 = g_w / (1.0 + p2)
    w2 = g_w * p2 / (1.0 + p2)
    base = g_idx * EXPERTS_PER_GROUP
    zero = jnp.zeros((1, tm), F32)
    return jnp.concatenate(
        [(base + i1).astype(F32), (base + i2).astype(F32), w1, w2, zero, zero, zero, zero], axis=0)


def _mix(x, yda, yml, wo1, wo2, gx, wq, kmem, vmem, wxo, gf, wr_hi, wr_lo, br):
    bsz, seq, d = x.shape
    tm = min(MIX_CHAINS * ROW_TILE, seq)
    mlen = kmem.shape[1]
    full = lambda a: pl.BlockSpec(a.shape, lambda b, i: (0,) * a.ndim)
    tok = lambda w: pl.BlockSpec((1, tm, w), lambda b, i: (b, i, 0))
    memspec = pl.BlockSpec((1, mlen, d), lambda b, i: (b, 0, 0))
    return pl.pallas_call(
        _mix_body,
        grid=(bsz, seq // tm),
        in_specs=[tok(d), tok(yda.shape[-1]), tok(yml.shape[-1]), full(wo1), full(wo2), full(gx), full(wq),
                  memspec, memspec, full(wxo), full(gf), full(wr_hi), full(wr_lo), full(br)],
        out_specs=[pl.BlockSpec((1, tm * SUBLANE, LANE), lambda b, i: (b, i, 0)),
                   pl.BlockSpec((1, SUBLANE, tm), lambda b, i: (b, 0, i))],
        out_shape=[
            jax.ShapeDtypeStruct((bsz, seq * d // LANE, LANE), F32),
            jax.ShapeDtypeStruct((bsz, SUBLANE, seq), F32),
        ],
        compiler_params=_params("parallel", "parallel"),
        name="mix_xattn_router",
    )(x, yda, yml, wo1, wo2, gx, wq, kmem, vmem, wxo, gf, wr_hi, wr_lo, br)


def _to_token_tiles(ref, first_row, dense):
    n, d = dense.shape
    for c in range(d // LANE):
        ref[pl.ds(first_row * SUBLANE + c, n, stride=SUBLANE), :] = dense[:, c * LANE:(c + 1) * LANE]


def _from_token_tiles(ref, first_row, n):
    return jnp.concatenate(
        [ref[pl.ds(first_row * SUBLANE + c, n, stride=SUBLANE), :] for c in range(SUBLANE)], axis=1)


def _moe_body(start_ref, cnt_ref, x_hbm, tok_hbm, wt_hbm, gf_ref, gfin_ref, w1_hbm, w3_hbm, w2_hbm, out_hbm,
              acc, h3, xg, yt, stage, tok_s, wt_s, w1_buf, w3_buf, w2_buf, sems, load_sems, store_sems, w_sems,
              *, final_norm):
    tile = pl.program_id(0)
    tt = h3.shape[0] // SUBLANE
    chunk_rows = stage.shape[1]
    group = 2 * SUBLANE
    depth = w1_buf.shape[0]
    n_steps = pl.num_programs(0) * N_EXPERTS

    def weight_copies(g):
        expert = g % N_EXPERTS
        slot = g % depth
        return [pltpu.make_async_copy(src.at[expert], dst.at[slot], w_sems.at[slot * 3 + k])
                for k, (src, dst) in enumerate(((w1_hbm, w1_buf), (w3_hbm, w3_buf), (w2_hbm, w2_buf)))]

    @pl.when(tile == 0)
    def _():
        for ahead in range(depth - 1):
            for cp in weight_copies(ahead):
                cp.start()

    def row_tile(ref, row0):
        return ref.at[pl.ds(pl.multiple_of(row0, SUBLANE), SUBLANE), :]

    def token_tile(ref, r):
        return ref.at[pl.ds(r * SUBLANE, SUBLANE), :]

    spare_row = tt * SUBLANE

    n_chunks = tt // chunk_rows
    chunk_tiles = chunk_rows * SUBLANE

    def load_chunk(i):
        rows = pl.ds(pl.multiple_of(i * chunk_tiles, chunk_tiles), chunk_tiles)
        return pltpu.make_async_copy(x_hbm.at[tile, rows, :], acc.at[rows, :], load_sems.at[i])

    lists = [pltpu.make_async_copy(tok_hbm.at[tile], tok_s, sems.at[0]),
             pltpu.make_async_copy(wt_hbm.at[tile], wt_s, sems.at[1])]
    for cp in lists:
        cp.start()
    for i in range(n_chunks):
        load_chunk(i).start()
    token_tile(acc, tt)[...] = jnp.zeros((SUBLANE, LANE), F32)

    def ffn_norm(i, c):
        load_chunk(i).wait()
        r0 = i * chunk_rows
        _to_token_tiles(h3, r0, _rms(_from_token_tiles(acc, r0, chunk_rows), gf_ref[...]))
        return c
    lax.fori_loop(0, n_chunks, ffn_norm, 0)
    for cp in lists:
        cp.wait()

    big = xg.shape[0] // SUBLANE
    small = MOE_ROWS

    def expert_step(e, carry):
        step_id = tile * N_EXPERTS + e

        @pl.when(step_id + depth - 1 < n_steps)
        def _():
            for cp in weight_copies(step_id + depth - 1):
                cp.start()

        for cp in weight_copies(step_id):
            cp.wait()
        w_slot = step_id % depth
        start = start_ref[step_id]
        count = cnt_ref[step_id]

        def run_block(first, cnt, rows):
            idx0 = start + first
            for r in range(rows):
                token_tile(xg, r)[...] = row_tile(h3, tok_s[idx0 + r])[...]

            xb = _from_token_tiles(xg, 0, rows).astype(BF16)
            a = _dot(xb, w1_buf[w_slot])
            b = _dot(xb, w3_buf[w_slot])
            hmid = (a * _sigmoid(a) * b).astype(BF16)
            _to_token_tiles(yt, 0, _dot(hmid, w2_buf[w_slot]))

            for g0 in range(0, rows, group):
                dst = [jnp.where(g0 + u < cnt, tok_s[idx0 + g0 + u], spare_row) for u in range(group)]
                new = [row_tile(acc, dst[u])[...] + wt_s[idx0 + g0 + u] * token_tile(yt, g0 + u)[...]
                       for u in range(group)]
                for u in range(group):
                    row_tile(acc, dst[u])[...] = new[u]

        @pl.when(count > 0)
        def _():
            run_block(0, jnp.minimum(big, count), big)

        def small_block(i, c):
            first = big + i * small
            run_block(first, jnp.minimum(small, count - first), small)
            return c

        lax.fori_loop(0, (jnp.maximum(count - big, 0) + small - 1) // small, small_block, 0)
        return carry

    lax.fori_loop(0, N_EXPERTS, expert_step, 0)

    n_slots = stage.shape[0]

    def store_chunk(i, slot):
        r0 = pl.multiple_of(i * chunk_rows, chunk_rows)
        return pltpu.make_async_copy(stage.at[slot], out_hbm.at[tile, pl.ds(r0, chunk_rows), :],
                                     store_sems.at[slot])

    def write_back(p, c):
        for slot in range(n_slots):
            i = p * n_slots + slot

            @pl.when(p > 0)
            def _():
                store_chunk(i - n_slots, slot).wait()

            rows = _from_token_tiles(acc, i * chunk_rows, chunk_rows)
            stage[slot] = _rms(rows, gfin_ref[...]) if final_norm else rows
            store_chunk(i, slot).start()
        return c
    lax.fori_loop(0, n_chunks // n_slots, write_back, 0)
    for slot in range(n_slots):
        store_chunk(n_chunks - n_slots + slot, slot).wait()


def _moe(start, cnt, tok, wt, x_token_tiles, gf, gfin, w1, w3, w2, final_norm):
    ntiles, tile_rows, _ = x_token_tiles.shape
    tt = tile_rows // SUBLANE
    d = SUBLANE * LANE
    de = w1.shape[-1]
    seg = tok.shape[1]
    body = functools.partial(_moe_body, final_norm=final_norm)
    hbm = pl.BlockSpec(memory_space=pl.ANY)
    depth = MOE_WEIGHT_SLOTS
    grid_spec = pltpu.PrefetchScalarGridSpec(
        num_scalar_prefetch=2,
        grid=(ntiles,),
        in_specs=[
            hbm, hbm, hbm,
            pl.BlockSpec((1, d), lambda t, *_: (0, 0)),
            pl.BlockSpec((1, d), lambda t, *_: (0, 0)),
            hbm, hbm, hbm,
        ],
        out_specs=hbm,
        scratch_shapes=[
            pltpu.VMEM((tile_rows + SUBLANE, LANE), F32),
            pltpu.VMEM((tile_rows, LANE), F32),
            pltpu.VMEM((MOE_FIRST_ROWS * SUBLANE, LANE), F32),
            pltpu.VMEM((MOE_FIRST_ROWS * SUBLANE, LANE), F32),
            pltpu.VMEM((2, 2 * MOE_ROWS, d), F32),
            pltpu.SMEM((seg,), jnp.int32),
            pltpu.SMEM((seg,), F32),
            pltpu.VMEM((depth, d, de), BF16),
            pltpu.VMEM((depth, d, de), BF16),
            pltpu.VMEM((depth, de, d), BF16),
            pltpu.SemaphoreType.DMA((2,)),
            pltpu.SemaphoreType.DMA((tt // (2 * MOE_ROWS),)),
            pltpu.SemaphoreType.DMA((2,)),
            pltpu.SemaphoreType.DMA((3 * depth,)),
        ],
    )
    return pl.pallas_call(
        body,
        grid_spec=grid_spec,
        out_shape=jax.ShapeDtypeStruct((ntiles, tt, d), F32),
        compiler_params=_params("arbitrary"),
        name="moe",
    )(start, cnt, x_token_tiles, tok, wt, gf, gfin, w1, w3, w2)


def _rope_tables(seq):
    half = DA_HEAD_DIM // 2
    inv_freq = ROPE_THETA ** (-jnp.arange(0, DA_HEAD_DIM, 2, dtype=F32) / DA_HEAD_DIM)
    ang = jnp.arange(seq, dtype=F32)[:, None] * inv_freq[None, :]
    cos, sin = jnp.cos(ang), jnp.sin(ang)
    reps = LANE // half
    cos_t = jnp.tile(cos, (1, reps))
    sin_t = jnp.tile(jnp.concatenate([-sin, sin], axis=1), (1, reps // 2))
    return cos_t, sin_t


def _sort_assignments(route, tt):
    bsz, _, seq = route.shape
    per = seq // tt
    ntiles = bsz * per
    tile = lambda a: a.reshape(bsz, TOP_K_ROWS, per, tt).transpose(0, 2, 1, 3).reshape(ntiles, TOP_K_ROWS * tt)
    keys = tile(route[:, 0:TOP_K_ROWS, :]).astype(jnp.int32)
    wts = tile(route[:, TOP_K_ROWS:2 * TOP_K_ROWS, :])
    order = jnp.argsort(keys, axis=-1).astype(jnp.int32)
    tok = (order % tt) * SUBLANE
    wt = jnp.take_along_axis(wts, order, axis=-1)
    counts = jnp.sum(keys[:, :, None] == jnp.arange(N_EXPERTS, dtype=jnp.int32)[None, None, :], axis=1,
                     dtype=jnp.int32)
    start = jnp.cumsum(counts, axis=-1, dtype=jnp.int32) - counts
    over = ((0, 0), (0, MOE_FIRST_ROWS))
    return start.reshape(-1), counts.reshape(-1), jnp.pad(tok, over), jnp.pad(wt, over)


TOP_K_ROWS = 2


def kernel(x, mem, norm_mix_g, w_in, w_out, da_lambda, da_subln_g, ml_conv_w, ml_conv_b, ml_gate_b, ml_norm_g, norm_x_g, norm_mem_g, w_xq, w_xk, w_xv, w_xo, norm_ffn_g, w_router_group, b_router_group, w_router_expert, b_router_expert, w1, w3, w2, norm_final_g):
    depth = w_in.shape[0]
    bsz, seq, d = x.shape
    da_qk = DA_HEADS * 2 * DA_HEAD_DIM
    da_width = DA_HEADS * DA_VAL_DIM
    ml_width = ml_norm_g.shape[-1]
    n_main = 2 * da_qk + da_width + 4 * ml_width
    n_in = w_in.shape[-1]
    cos_t, sin_t = _rope_tables(seq)
    row = lambda v: v.reshape(1, -1).astype(F32)

    for l in range(depth):
        lam_init = 0.8 - 0.6 * math.exp(-0.3 * l)
        w_pad = jnp.pad(w_in[l], ((0, 0), (0, n_main + LANE - n_in))).astype(BF16)
        qt, vt, mvt, mot, proj, gcol, grow = _inproj(x, row(norm_mix_g[l]), w_pad, cos_t, sin_t)

        yda = _diffattn(qt, proj, vt, da_lambda[l].astype(F32), da_subln_g[l].astype(F32).reshape(-1, 1), lam_init)

        gb = ml_gate_b[l].astype(F32).reshape(1, 2 * ML_HEADS)
        gb_col = jnp.pad(gb, ((0, 0), (0, LANE - 2 * ML_HEADS)))
        gb_row = gb.reshape(2 * ML_HEADS, 1)
        ng_b = jnp.broadcast_to(ml_norm_g[l].astype(F32)[:, None], (ml_width, LANE))
        yml = _mlstm(proj, mvt, mot, gcol, grow, ml_conv_w[l].reshape(ML_CONV, 2 * ml_width).astype(F32),
                     row(ml_conv_b[l]), gb_col, gb_row, ng_b, da_qk)

        kmem, vmem = _memkv(mem, row(norm_mem_g[l]), w_xk[l].astype(BF16), w_xv[l].astype(BF16))

        wr = jnp.concatenate([w_router_group[l], w_router_expert[l]], axis=1).astype(F32).T
        wr = jnp.pad(wr, ((0, ROUTER_ROWS - wr.shape[0]), (0, 0)))
        wr_hi = wr.astype(BF16)
        wr_lo = (wr - wr_hi.astype(F32)).astype(BF16)
        br = jnp.pad(jnp.concatenate([b_router_group[l], b_router_expert[l]]).astype(F32),
                     (0, ROUTER_ROWS - N_GROUPS - N_EXPERTS)).reshape(ROUTER_ROWS, 1)
        wo = w_out[l].astype(BF16)
        x2, route = _mix(x, yda, yml, wo[:da_width], wo[da_width:], row(norm_x_g[l]), w_xq[l].astype(BF16),
                         kmem, vmem, w_xo[l].astype(BF16), row(norm_ffn_g[l]), wr_hi, wr_lo, br)

        tt = min(MOE_TILE, seq)
        x = _moe(*_sort_assignments(route, tt), x2.reshape(bsz * seq // tt, tt * d // LANE, LANE),
                 row(norm_ffn_g[l]), row(norm_final_g),
                 w1[l].astype(BF16), w3[l].astype(BF16), w2[l].astype(BF16),
                 final_norm=(l + 1 == depth)).reshape(bsz, seq, d)
    return x
```
